```python
import jax, jax.numpy as jnp
from jax import lax
import numpy as np

D_MODEL = 1024
BATCH = 8
SEQ = 2048
DEPTH = 2
DEC_BATCH = 128
DEC_SEQ = 1
PAST_LEN = 16384
PAGE_SIZE = 128

D_MIX = D_MODEL
D_GMLP = D_MIX // 2
D_LRU = D_MIX - D_GMLP
GMLP_GROUPS = 4
GMLP_GROUP_DIM = D_GMLP // GMLP_GROUPS
CHUNK = 128
LRU_HEADS = 8
LRU_HEAD_DIM = D_LRU // LRU_HEADS
CONV_W = 4
LRU_C = 8.0
N_EXPERTS = 32
TOP_K = 4
D_FF_EXPERT = D_MODEL
SWIGLU_LIMIT = 7.0
SWIGLU_ALPHA = 1.702
PLE_DIM = 256
EPS = 1e-6
D_IN = 2 * D_GMLP + 2 * D_LRU

kernel_name = "hymba_gmlp_rglru_moe_decode_step"

F32 = jnp.float32


def rmsnorm(x, g):
    xf = x.astype(F32)
    y = xf * lax.rsqrt(jnp.mean(xf * xf, axis=-1, keepdims=True) + EPS)
    return (y * g.astype(F32)).astype(x.dtype)


def group_rmsnorm(x, g, n_groups):
    shp = x.shape
    xf = x.astype(F32).reshape(shp[:-1] + (n_groups, shp[-1] // n_groups))
    y = xf * lax.rsqrt(jnp.mean(xf * xf, axis=-1, keepdims=True) + EPS)
    return (y.reshape(shp) * g.astype(F32)).astype(x.dtype)


def gmlp_mixer(u, v, w_s, b_s, g_v):
    B, T, _ = v.shape
    cl = min(T, CHUNK)
    nc = T // cl
    v = group_rmsnorm(v, g_v, GMLP_GROUPS)
    mask = jnp.tril(jnp.ones((cl, cl), dtype=bool))
    w = jnp.where(mask, w_s[:, :cl, :cl], 0).astype(v.dtype)
    vc = v.reshape(B, nc, cl, GMLP_GROUPS, GMLP_GROUP_DIM)
    s = jnp.einsum('gts,bnsgd->bntgd', w, vc) + b_s[:, :cl].T[:, :, None].astype(v.dtype)
    y = u * s.reshape(B, T, D_GMLP)
    return y, v[:, T - cl:]


def causal_conv(x, buf, w, b):
    T = x.shape[1]
    xp = jnp.concatenate([buf.astype(x.dtype), x], axis=1)
    y = b.astype(x.dtype)
    for k in range(CONV_W):
        y = y + xp[:, k:k + T] * w[k]
    return y, xp[:, -(CONV_W - 1):]


def rglru(x, h0, w_a, b_a, w_x, b_x, lam, reset_first):
    B, T, C = x.shape
    xf = x.astype(F32)
    xh = xf.reshape(B, T, LRU_HEADS, LRU_HEAD_DIM)
    r = jax.nn.sigmoid(jnp.einsum('bthi,hij->bthj', xh, w_a.astype(F32)) + b_a.astype(F32)).reshape(B, T, C)
    i = jax.nn.sigmoid(jnp.einsum('bthi,hij->bthj', xh, w_x.astype(F32)) + b_x.astype(F32)).reshape(B, T, C)
    log_a = -LRU_C * r * jax.nn.softplus(-lam.astype(F32))
    a = jnp.exp(log_a)
    mult = jnp.sqrt(-jnp.expm1(2.0 * log_a))
    if reset_first:
        mult = mult.at[:, 0].set(1.0)
    bx = mult * i * xf

    def step(h, ab):
        a_t, b_t = ab
        h = a_t * h + b_t
        return h, h

    hT, hs = lax.scan(step, h0.astype(F32), (jnp.swapaxes(a, 0, 1), jnp.swapaxes(bx, 0, 1)))
    return jnp.swapaxes(hs, 0, 1).astype(x.dtype), hT.astype(h0.dtype)


def moe(x, w_r, b_r, w1, b1, w2, b2):
    B, T, D = x.shape
    xt = x.reshape(B * T, D)
    logits = xt.astype(F32) @ w_r.astype(F32) + b_r.astype(F32)
    top_v, top_i = lax.top_k(logits, TOP_K)
    gw = jax.nn.softmax(top_v, axis=-1)
    combine = jnp.sum(jax.nn.one_hot(top_i, N_EXPERTS, dtype=F32) * gw[..., None], axis=1).astype(x.dtype)
    out = jnp.zeros_like(xt)
    for e in range(N_EXPERTS):
        hdn = xt @ w1[e] + b1[e]
        gate = jnp.minimum(hdn[:, :D_FF_EXPERT], SWIGLU_LIMIT)
        up = jnp.clip(hdn[:, D_FF_EXPERT:], -SWIGLU_LIMIT, SWIGLU_LIMIT)
        act = (up + 1.0) * gate * jax.nn.sigmoid(SWIGLU_ALPHA * gate)
        out = out + combine[:, e:e + 1] * (act @ w2[e] + b2[e])
    return out.reshape(B, T, D)


def trunk(x, p, conv_bufs, lru_states, reset_first,
          g_mix, w_in, g_v, w_s, b_s, conv_w, conv_b, w_a, b_a, w_x, b_x, lam,
          g_oa, g_ob, w_out, g_ffn, w_r, b_r, w1, b1, w2, b2, g_ple, w_pg, b_pg, w_pp, g_final):
    h = x
    new_v, new_conv, new_lru = [], [], []
    for l in range(DEPTH):
        n = rmsnorm(h, g_mix[l])
        z = n @ w_in[l]
        u_a, v_a, x_b, g_b = jnp.split(z, [D_GMLP, 2 * D_GMLP, 2 * D_GMLP + D_LRU], axis=-1)
        y_a, v_rows = gmlp_mixer(jax.nn.gelu(u_a), jax.nn.gelu(v_a), w_s[l], b_s[l], g_v[l])
        xc, buf = causal_conv(x_b, conv_bufs[l], conv_w[l], conv_b[l])
        hs, hT = rglru(xc, lru_states[l], w_a[l], b_a[l], w_x[l], b_x[l], lam[l], reset_first)
        y_b = hs * jax.nn.gelu(g_b)
        y = jnp.concatenate([group_rmsnorm(y_a, g_oa[l], GMLP_GROUPS),
                             group_rmsnorm(y_b, g_ob[l], LRU_HEADS)], axis=-1)
        h = h + y @ w_out[l]
        h = h + moe(rmsnorm(h, g_ffn[l]), w_r[l], b_r[l], w1[l], b1[l], w2[l], b2[l])
        gate = jax.nn.sigmoid(rmsnorm(h, g_ple[l]) @ w_pg[l] + b_pg[l])
        h = h + gate * (p[l] @ w_pp[l])
        new_v.append(v_rows)
        new_conv.append(buf)
        new_lru.append(hT)
    return rmsnorm(h, g_final), jnp.stack(new_v), jnp.stack(new_conv), jnp.stack(new_lru)


def setup_inputs(seed: int = 0) -> dict:
    key = jax.random.key(seed)
    ks = iter(jax.random.split(key, 40))
    nrm = lambda shape, s=1.0: jax.random.normal(next(ks), shape, F32) * s
    gain = lambda shape: 1.0 + nrm(shape, 0.05)
    u = jax.random.uniform(next(ks), (DEPTH, D_LRU), F32, 0.9, 0.999)
    s_root = u ** (1.0 / LRU_C)
    lam = jnp.log(s_root) - jnp.log1p(-s_root)
    return {
        "x_prompt": nrm((BATCH, SEQ, D_MODEL)),
        "x_sample": nrm((DEC_BATCH, DEC_SEQ, D_MODEL)),
        "state_conv": nrm((DEPTH, DEC_BATCH, CONV_W - 1, D_LRU)),
        "state_lru": nrm((DEPTH, DEC_BATCH, D_LRU), 0.5),
        "p_prompt": nrm((DEPTH, BATCH, SEQ, PLE_DIM)),
        "p_sample": nrm((DEPTH, DEC_BATCH, DEC_SEQ, PLE_DIM)),
        "g_mix": gain((DEPTH, D_MODEL)),
        "w_in": nrm((DEPTH, D_MODEL, D_IN), D_MODEL ** -0.5),
        "g_v": gain((DEPTH, D_GMLP)),
        "w_s": nrm((DEPTH, GMLP_GROUPS, CHUNK, CHUNK), 0.5 * CHUNK ** -0.5),
        "b_s": 1.0 + nrm((DEPTH, GMLP_GROUPS, CHUNK), 0.1),
        "conv_w": nrm((DEPTH, CONV_W, D_LRU), CONV_W ** -0.5),
        "conv_b": nrm((DEPTH, D_LRU), 0.01),
        "w_a": nrm((DEPTH, LRU_HEADS, LRU_HEAD_DIM, LRU_HEAD_DIM), LRU_HEAD_DIM ** -0.5),
        "b_a": nrm((DEPTH, LRU_HEADS, LRU_HEAD_DIM), 0.01),
        "w_x": nrm((DEPTH, LRU_HEADS, LRU_HEAD_DIM, LRU_HEAD_DIM), LRU_HEAD_DIM ** -0.5),
        "b_x": nrm((DEPTH, LRU_HEADS, LRU_HEAD_DIM), 0.01),
        "lam": lam,
        "g_oa": gain((DEPTH, D_GMLP)),
        "g_ob": gain((DEPTH, D_LRU)),
        "w_out": nrm((DEPTH, D_MIX, D_MODEL), D_MIX ** -0.5),
        "g_ffn": gain((DEPTH, D_MODEL)),
        "w_r": nrm((DEPTH, D_MODEL, N_EXPERTS), D_MODEL ** -0.5),
        "b_r": nrm((DEPTH, N_EXPERTS), 0.01),
        "w1": nrm((DEPTH, N_EXPERTS, D_MODEL, 2 * D_FF_EXPERT), D_MODEL ** -0.5),
        "b1": nrm((DEPTH, N_EXPERTS, 2 * D_FF_EXPERT), 0.01),
        "w2": nrm((DEPTH, N_EXPERTS, D_FF_EXPERT, D_MODEL), D_FF_EXPERT ** -0.5),
        "b2": nrm((DEPTH, N_EXPERTS, D_MODEL), 0.01),
        "g_ple": gain((DEPTH, D_MODEL)),
        "w_pg": nrm((DEPTH, D_MODEL, D_MODEL), D_MODEL ** -0.5),
        "b_pg": nrm((DEPTH, D_MODEL), 0.01),
        "w_pp": nrm((DEPTH, PLE_DIM, D_MODEL), PLE_DIM ** -0.5),
        "g_final": gain((D_MODEL,)),
    }


def reference(x_prompt, x_sample, state_conv, state_lru, p_prompt, p_sample,
              g_mix, w_in, g_v, w_s, b_s, conv_w, conv_b, w_a, b_a, w_x, b_x, lam,
              g_oa, g_ob, w_out, g_ffn, w_r, b_r, w1, b1, w2, b2, g_ple, w_pg, b_pg, w_pp, g_final):
    weights = (g_mix, w_in, g_v, w_s, b_s, conv_w, conv_b, w_a, b_a, w_x, b_x, lam,
               g_oa, g_ob, w_out, g_ffn, w_r, b_r, w1, b1, w2, b2, g_ple, w_pg, b_pg, w_pp, g_final)
    bp = x_prompt.shape[0]
    conv0 = jnp.zeros((DEPTH, bp, CONV_W - 1, D_LRU), x_prompt.dtype)
    lru0 = jnp.zeros((DEPTH, bp, D_LRU), state_lru.dtype)
    y_prompt, v_prompt, conv_prompt, lru_prompt = trunk(x_prompt, p_prompt, conv0, lru0, True, *weights)
    y_sample, v_sample, conv_sample, lru_sample = trunk(x_sample, p_sample, state_conv, state_lru, False, *weights)
    return (y_prompt, y_sample, v_prompt, conv_prompt, lru_prompt, v_sample, conv_sample, lru_sample)
```

```python
import functools

import jax
import jax.numpy as jnp
from jax import lax
from jax.experimental import pallas as pl
from jax.experimental.pallas import tpu as pltpu

F32 = jnp.float32
BF16 = jnp.bfloat16
I32 = jnp.int32

CHUNK = 128
HALF = CHUNK // 2
GMLP_GROUPS = 4
LRU_HEADS = 8
CONV_W = 4
LRU_C = 8.0
TOP_K = 4
SWIGLU_LIMIT = 7.0
SWIGLU_ALPHA = 1.702
EPS = 1e-6

LANES = 128
SUBLANES = 8
TOKEN_BLOCK = 256
EXPERT_TILE = 256
SEG_ALIGN = SUBLANES
VMEM_LIMIT = 56 * 1024 * 1024


def _round_up(x, m):
    return (x + m - 1) // m * m


def _dot(a, b):
    return jnp.dot(a, b, preferred_element_type=F32)


def _gelu(x):
    return 0.5 * x * (1.0 + jnp.tanh(0.7978845608028654 * (x + 0.044715 * (x * x * x))))


def _sigmoid(x):
    return 1.0 / (1.0 + jnp.exp(-x))


def _softplus(x):
    return jnp.maximum(x, 0.0) + jnp.log1p(jnp.exp(-jnp.abs(x)))


def _rms(x, g):
    ms = jnp.mean(x * x, axis=-1, keepdims=True)
    return x * lax.rsqrt(ms + EPS) * g


def _group_norm_128(x, g):
    outs = []
    for j in range(x.shape[1] // LANES):
        blk = x[:, j * LANES:(j + 1) * LANES]
        ms = jnp.mean(blk * blk, axis=-1, keepdims=True)
        outs.append(blk * lax.rsqrt(ms + EPS))
    return jnp.concatenate(outs, axis=1) * g


def _group_norm_64(x, g):
    half = LANES // 2
    lo_mask = lax.broadcasted_iota(I32, (1, LANES), 1) < half
    outs = []
    for j in range(x.shape[1] // LANES):
        blk = x[:, j * LANES:(j + 1) * LANES]
        sq = blk * blk
        lo = jnp.sum(jnp.where(lo_mask, sq, 0.0), axis=-1, keepdims=True)
        hi = jnp.sum(jnp.where(lo_mask, 0.0, sq), axis=-1, keepdims=True)
        ms = jnp.where(lo_mask, lo, hi) * (1.0 / half)
        outs.append(blk * lax.rsqrt(ms + EPS))
    return jnp.concatenate(outs, axis=1) * g


def _split_bf16(x):
    hi = x.astype(BF16)
    lo = (x - hi.astype(F32)).astype(BF16)
    return hi, lo


def _dot_bf16(x, w):
    return _dot(x.astype(BF16), w)


def _dot_3pass(x, w):
    x_hi, x_lo = _split_bf16(x)
    w_hi, w_lo = _split_bf16(w)
    return _dot(x_hi, w_hi) + (_dot(x_lo, w_hi) + _dot(x_hi, w_lo))


def _lru_gates(xc, wa_ref, ba_ref, wx_ref, bx_ref, lam_ref, mm):
    r = _sigmoid(mm(xc, wa_ref[...]) + ba_ref[...])
    i = _sigmoid(mm(xc, wx_ref[...]) + bx_ref[...])
    log_a = (-LRU_C * r) * _softplus(-lam_ref[...])
    a = jnp.exp(log_a)
    mult = jnp.sqrt(-jnp.tanh(log_a) * (a * a + 1.0))
    return a, mult, i


def _mix_prompt_kernel(h_ref, gmix_ref, win_ref, gv_ref, kd_ref, k10_ref, bs_ref, cw_ref, cb_ref,
                       wa_ref, ba_ref, wx_ref, bx_ref, lam_ref, goa_ref, gob_ref,
                       y_ref, v_ref, conv_ref, lru_ref,
                       hstate, xpad, vprev, a_s, b_s, hs_s, *, batch):
    c = pl.program_id(0)
    n_steps = pl.num_programs(0)
    rows = h_ref.shape[0]
    dh = gv_ref.shape[1]
    tail = (CONV_W - 1) * batch
    par = c % 2

    @pl.when(c == 0)
    def _():
        hstate[...] = jnp.zeros_like(hstate)
        xpad[0:tail, :] = jnp.zeros((tail, dh), F32)

    @pl.when(par == 0)
    def _():
        vprev[...] = jnp.zeros_like(vprev)

    n = _rms(h_ref[...], gmix_ref[...]).astype(BF16)

    v = _gelu(_dot(n, win_ref[:, dh:2 * dh]))
    vn = _group_norm_128(v, gv_ref[...])
    vb = vn.astype(BF16)
    s_parts = []
    for g in range(GMLP_GROUPS):
        sl = slice(g * LANES, (g + 1) * LANES)
        s_parts.append(_dot(kd_ref[par, g], vb[:, sl]) + _dot(k10_ref[g], vprev[:, sl]))
    s = jnp.concatenate(s_parts, axis=1) + bs_ref[par]
    vprev[...] = vb

    @pl.when(c >= n_steps - 2)
    def _():
        v_ref[...] = vn

    u = _gelu(_dot(n, win_ref[:, 0:dh]))
    y_ref[:, 0:dh] = _group_norm_128(u * s, goa_ref[...]).astype(BF16)

    xpad[tail:tail + rows, :] = _dot(n, win_ref[:, 2 * dh:3 * dh])
    xc = cb_ref[...] + cw_ref[0:1, :] * xpad[0:rows, :]
    for k in range(1, CONV_W):
        xc = xc + cw_ref[k:k + 1, :] * xpad[k * batch:k * batch + rows, :]
    new_tail = xpad[rows:rows + tail, :]
    xpad[0:tail, :] = new_tail

    a, mult, i = _lru_gates(xc, wa_ref, ba_ref, wx_ref, bx_ref, lam_ref, _dot_bf16)
    row = lax.broadcasted_iota(I32, (rows, 1), 0)
    mult = jnp.where((c == 0) & (row < batch), 1.0, mult)
    a_s[...] = a
    b_s[...] = mult * i * xc

    def step(t, h):
        r0 = pl.multiple_of(t * batch, batch)
        h = a_s[pl.ds(r0, batch), :] * h + b_s[pl.ds(r0, batch), :]
        hs_s[pl.ds(r0, batch), :] = h
        return h

    h_last = lax.fori_loop(0, rows // batch, step, hstate[...], unroll=8)
    hstate[...] = h_last

    gb = _gelu(_dot(n, win_ref[:, 3 * dh:4 * dh]))
    y_ref[:, dh:2 * dh] = _group_norm_64(hs_s[...] * gb, gob_ref[...]).astype(BF16)

    @pl.when(c == n_steps - 1)
    def _():
        conv_ref[...] = new_tail
        lru_ref[...] = h_last


def _mix_prompt(h, n_rows, batch, w):
    d = h.shape[1]
    dh = d // 2
    rows = HALF * batch
    n_steps = n_rows // rows
    tail = (CONV_W - 1) * batch
    const2 = lambda c: (0, 0)
    const3 = lambda c: (0, 0, 0)
    const4 = lambda c: (0, 0, 0, 0)
    return pl.pallas_call(
        functools.partial(_mix_prompt_kernel, batch=batch),
        grid=(n_steps,),
        in_specs=[
            pl.BlockSpec((rows, d), lambda c: (c, 0)),
            pl.BlockSpec((1, d), const2),
            pl.BlockSpec((d, 2 * d), const2),
            pl.BlockSpec((1, dh), const2),
            pl.BlockSpec((2, GMLP_GROUPS, rows, rows), const4),
            pl.BlockSpec((GMLP_GROUPS, rows, rows), const3),
            pl.BlockSpec((2, rows, dh), const3),
            pl.BlockSpec((CONV_W, dh), const2),
            pl.BlockSpec((1, dh), const2),
            pl.BlockSpec((dh, dh), const2),
            pl.BlockSpec((1, dh), const2),
            pl.BlockSpec((dh, dh), const2),
            pl.BlockSpec((1, dh), const2),
            pl.BlockSpec((1, dh), const2),
            pl.BlockSpec((1, dh), const2),
            pl.BlockSpec((1, dh), const2),
        ],
        out_specs=[
            pl.BlockSpec((rows, d), lambda c: (c, 0)),
            pl.BlockSpec((rows, dh), lambda c: (jnp.maximum(c - (n_steps - 2), 0), 0)),
            pl.BlockSpec((tail, dh), const2),
            pl.BlockSpec((batch, dh), const2),
        ],
        out_shape=[
            jax.ShapeDtypeStruct((n_rows, d), BF16),
            jax.ShapeDtypeStruct((2 * rows, dh), F32),
            jax.ShapeDtypeStruct((tail, dh), F32),
            jax.ShapeDtypeStruct((batch, dh), F32),
        ],
        scratch_shapes=[
            pltpu.VMEM((batch, dh), F32),
            pltpu.VMEM((tail + rows, dh), F32),
            pltpu.VMEM((rows, dh), BF16),
            pltpu.VMEM((rows, dh), F32),
            pltpu.VMEM((rows, dh), F32),
            pltpu.VMEM((rows, dh), F32),
        ],
        compiler_params=pltpu.CompilerParams(dimension_semantics=("arbitrary",), vmem_limit_bytes=VMEM_LIMIT),
        name="mix_prompt",
    )(h, w["g_mix"], w["w_in"], w["g_v"], w["kd"], w["k10"], w["bs_rows"], w["conv_w"], w["conv_b"],
      w["wa"], w["b_a"], w["wx"], w["b_x"], w["lam"], w["g_oa"], w["g_ob"])


def _sample_front_kernel(h_ref, h2_any, n2_any, idx_any, gw_any, rank_any, cnt_any,
                         gmix_ref, win_ref, gv_ref, w0_ref, b0_ref, cw_ref, cb_ref,
                         wa_ref, ba_ref, wx_ref, bx_ref, lam_ref, goa_ref, gob_ref, sconv_ref, slru_ref,
                         wout_ref, gffn_ref, wr_ref, br_ref,
                         h2_ref, n2_ref, idx_ref, gw_ref, rank_ref, cnt_ref, v_ref, conv_ref, lru_ref,
                         *, block, n_valid, n_exp):
    del h2_any, n2_any, idx_any, gw_any, rank_any, cnt_any
    dh = gv_ref.shape[1]
    h = h_ref[...]
    n = _rms(h, gmix_ref[...])
    proj = lambda lo: _dot_3pass(n, win_ref[:, lo:lo + dh])

    vn = _group_norm_128(_gelu(proj(dh)), gv_ref[...])
    v_ref[...] = vn
    s = vn * w0_ref[...] + b0_ref[...]
    ya = _group_norm_128(_gelu(proj(0)) * s, goa_ref[...])

    xb = proj(2 * dh)
    xc = cb_ref[...] + cw_ref[CONV_W - 1:CONV_W, :] * xb
    for k in range(CONV_W - 1):
        xc = xc + cw_ref[k:k + 1, :] * sconv_ref[k]
    for k in range(CONV_W - 2):
        conv_ref[k] = sconv_ref[k + 1]
    conv_ref[CONV_W - 2] = xb

    a, mult, i = _lru_gates(xc, wa_ref, ba_ref, wx_ref, bx_ref, lam_ref, _dot_3pass)
    h_new = a * slru_ref[...] + mult * i * xc
    lru_ref[...] = h_new
    yb = _group_norm_64(h_new * _gelu(proj(3 * dh)), gob_ref[...])

    h2 = h + _dot_3pass(ya, wout_ref[0:dh, :]) + _dot_3pass(yb, wout_ref[dh:2 * dh, :])
    _route_block(h2, gffn_ref, wr_ref, br_ref, h2_ref, n2_ref, idx_ref, gw_ref, rank_ref, cnt_ref,
                 block=block, n_valid=n_valid, n_exp=n_exp)


def _sample_front(h, post_out, block, w, sconv, slru, n_valid, n_exp):
    d = h.shape[1]
    dh = d // 2
    rows = TOKEN_BLOCK
    const2 = lambda i: (0, 0)
    const3 = lambda i: (0, 0, 0)
    vec = pl.BlockSpec((1, dh), const2)
    anyspace = pl.BlockSpec(memory_space=pl.ANY)
    blk_row = pl.BlockSpec((rows, d), lambda i: (block, 0))
    blk_tok = pl.BlockSpec((SUBLANES, rows), lambda i: (0, block))
    return pl.pallas_call(
        functools.partial(_sample_front_kernel, block=block, n_valid=n_valid, n_exp=n_exp),
        grid=(1,),
        in_specs=[
            blk_row,
            anyspace, anyspace, anyspace, anyspace, anyspace, anyspace,
            pl.BlockSpec((1, d), const2),
            pl.BlockSpec((d, 2 * d), const2),
            vec, vec, vec,
            pl.BlockSpec((CONV_W, dh), const2),
            vec,
            pl.BlockSpec((dh, dh), const2),
            vec,
            pl.BlockSpec((dh, dh), const2),
            vec, vec, vec, vec,
            pl.BlockSpec((CONV_W - 1, rows, dh), const3),
            pl.BlockSpec((rows, dh), const2),
            pl.BlockSpec((d, d), const2),
            pl.BlockSpec((1, d), const2),
            pl.BlockSpec((d, LANES), const2),
            pl.BlockSpec((n_exp, 1), const2),
        ],
        out_specs=[
            blk_row, blk_row, blk_tok, blk_tok, blk_tok,
            pl.BlockSpec((1, SUBLANES, n_exp), lambda i: (block, 0, 0)),
            pl.BlockSpec((rows, dh), const2),
            pl.BlockSpec((CONV_W - 1, rows, dh), const3),
            pl.BlockSpec((rows, dh), const2),
        ],
        out_shape=[jax.ShapeDtypeStruct(a.shape, a.dtype) for a in post_out] + [
            jax.ShapeDtypeStruct((rows, dh), F32),
            jax.ShapeDtypeStruct((CONV_W - 1, rows, dh), F32),
            jax.ShapeDtypeStruct((rows, dh), F32),
        ],
        input_output_aliases={1 + k: k for k in range(6)},
        compiler_params=pltpu.CompilerParams(dimension_semantics=("arbitrary",), vmem_limit_bytes=VMEM_LIMIT),
        name="sample_front",
    )(h, *post_out, w["g_mix"], w["w_in_f32"], w["g_v"], w["w0_row"], w["b0_row"], w["conv_w"], w["conv_b"],
      w["wa_f32"], w["b_a"], w["wx_f32"], w["b_x"], w["lam"], w["g_oa"], w["g_ob"], sconv, slru,
      w["w_out_f32"], w["g_ffn"], w["w_r"], w["b_r"])


def _route_block(h2, gffn_ref, wr_ref, br_ref, h2_ref, n2_ref, idx_ref, gw_ref, rank_ref, cnt_ref,
                 *, block, n_valid, n_exp):
    tb = h2.shape[0]
    h2_ref[...] = h2
    n2 = _rms(h2, gffn_ref[...])
    n2_ref[...] = n2.astype(BF16)

    logits = _dot_3pass(n2, wr_ref[...])
    lt = logits.T[0:n_exp, :] + br_ref[...]

    eio = lax.broadcasted_iota(I32, (n_exp, tb), 0).astype(F32)
    col = block * tb + lax.broadcasted_iota(I32, (1, tb), 1)
    valid = col < n_valid
    sels, vals, hots = [], [], []
    for _ in range(TOP_K):
        m = jnp.max(lt, axis=0, keepdims=True)
        sel = jnp.min(jnp.where(lt == m, eio, float(n_exp)), axis=0, keepdims=True)
        hot = eio == sel
        lt = jnp.where(hot, -jnp.inf, lt)
        sels.append(sel.astype(I32))
        vals.append(m)
        hots.append(hot)
    exps = [jnp.exp(v - vals[0]) for v in vals]
    den = exps[0]
    for e in exps[1:]:
        den = den + e

    occ = jnp.zeros((n_exp, tb), F32)
    for hot in hots:
        occ = occ + jnp.where(hot & valid, 1.0, 0.0)
    occ_b = occ.astype(BF16)
    before = (lax.broadcasted_iota(I32, (tb, tb), 0) < lax.broadcasted_iota(I32, (tb, tb), 1))
    ranks_e = _dot(occ_b, jnp.where(before, 1.0, 0.0).astype(BF16))
    cnt_ref[0] = lax.dot_general(jnp.ones((SUBLANES, tb), BF16), occ_b, (((1,), (1,)), ((), ())),
                                 preferred_element_type=F32)

    rio = lax.broadcasted_iota(I32, (SUBLANES, tb), 0)
    idx_out = jnp.full((SUBLANES, tb), -1, I32)
    gw_out = jnp.zeros((SUBLANES, tb), F32)
    rank_out = jnp.zeros((SUBLANES, tb), I32)
    for k in range(TOP_K):
        rank_k = jnp.sum(jnp.where(hots[k], ranks_e, 0.0), axis=0, keepdims=True).astype(I32)
        idx_out = jnp.where(rio == k, jnp.where(valid, sels[k], -1), idx_out)
        gw_out = jnp.where(rio == k, exps[k] / den, gw_out)
        rank_out = jnp.where(rio == k, rank_k, rank_out)
    idx_ref[...] = idx_out
    gw_ref[...] = gw_out
    rank_ref[...] = rank_out


def _post_kernel(y_ref, h_ref, wout_ref, gffn_ref, wr_ref, br_ref,
                 h2_ref, n2_ref, idx_ref, gw_ref, rank_ref, cnt_ref, *, n_valid, n_exp):
    h2 = h_ref[...] + _dot(y_ref[...], wout_ref[...])
    _route_block(h2, gffn_ref, wr_ref, br_ref, h2_ref, n2_ref, idx_ref, gw_ref, rank_ref, cnt_ref,
                 block=pl.program_id(0), n_valid=n_valid, n_exp=n_exp)


def _post(y, h, w, n_valid, n_exp):
    n_total, d = h.shape
    tb = TOKEN_BLOCK
    nb = n_total // tb
    const2 = lambda b: (0, 0)
    tok = pl.BlockSpec((SUBLANES, tb), lambda b: (0, b))
    return pl.pallas_call(
        functools.partial(_post_kernel, n_valid=n_valid, n_exp=n_exp),
        grid=(y.shape[0] // tb,),
        in_specs=[
            pl.BlockSpec((tb, d), lambda b: (b, 0)),
            pl.BlockSpec((tb, d), lambda b: (b, 0)),
            pl.BlockSpec((d, d), const2),
            pl.BlockSpec((1, d), const2),
            pl.BlockSpec((d, LANES), const2),
            pl.BlockSpec((n_exp, 1), const2),
        ],
        out_specs=[
            pl.BlockSpec((tb, d), lambda b: (b, 0)),
            pl.BlockSpec((tb, d), lambda b: (b, 0)),
            tok, tok, tok,
            pl.BlockSpec((1, SUBLANES, n_exp), lambda b: (b, 0, 0)),
        ],
        out_shape=[
            jax.ShapeDtypeStruct((n_total, d), F32),
            jax.ShapeDtypeStruct((n_total, d), BF16),
            jax.ShapeDtypeStruct((SUBLANES, n_total), I32),
            jax.ShapeDtypeStruct((SUBLANES, n_total), F32),
            jax.ShapeDtypeStruct((SUBLANES, n_total), I32),
            jax.ShapeDtypeStruct((nb, SUBLANES, n_exp), F32),
        ],
        compiler_params=pltpu.CompilerParams(dimension_semantics=("arbitrary",), vmem_limit_bytes=VMEM_LIMIT),
        name="post_router",
    )(y, h, w["w_out"], w["g_ffn"], w["w_r"], w["b_r"])


def _max_sorted_rows(n_valid, nb, n_exp):
    worst = TOP_K * n_valid + (SEG_ALIGN - 1) * nb * n_exp + (EXPERT_TILE - SEG_ALIGN) * n_exp
    return _round_up(worst, EXPERT_TILE)


def _route_meta(cnt, max_tiles):
    n_exp = cnt.shape[1]
    p8 = (cnt + SEG_ALIGN - 1) // SEG_ALIGN * SEG_ALIGN
    off = jnp.cumsum(p8, axis=1) - p8
    tot = jnp.sum(p8, axis=1)
    seg = jnp.sum(p8, axis=0)
    reg = (seg + EXPERT_TILE - 1) // EXPERT_TILE * EXPERT_TILE
    reg_start = jnp.cumsum(reg) - reg
    start = reg_start[None, :] + jnp.cumsum(p8, axis=0) - p8
    tiles_end = jnp.cumsum(reg // EXPERT_TILE)
    tile_ids = jnp.arange(max_tiles, dtype=I32)
    tile_expert = jnp.minimum(jnp.sum((tiles_end[None, :] <= tile_ids[:, None]).astype(I32), axis=1), n_exp - 1)
    return dict(off=off.reshape(-1).astype(I32), start=start.reshape(-1).astype(I32),
                p8=p8.reshape(-1).astype(I32), tot=tot.astype(I32),
                tail_start=(reg_start + seg).astype(I32), tail_len=(reg - seg).astype(I32),
                tile_expert=tile_expert, n_tiles=tiles_end[-1:].astype(I32))


def _local_positions(idx, rank, off_ref, b, n_exp):
    offv = jnp.zeros(idx.shape, I32)
    for e in range(n_exp):
        offv = jnp.where(idx == e, off_ref[b * n_exp + e], offv)
    return jnp.where(idx >= 0, offv + rank, -1)


def _wait_rows(n_rows, src, dst, sem):
    def body(j, carry):
        pltpu.make_async_copy(src, dst, sem).wait()
        return carry
    lax.fori_loop(0, n_rows // SEG_ALIGN, body, 0)


def _dispatch_kernel(off_ref, start_ref, p8_ref, tot_ref, tstart_ref, tlen_ref,
                     n2_ref, idx_ref, rank_ref, xs_hbm, xbuf, zbuf, sems, zsem, *, n_exp, mp):
    b = pl.program_id(0)
    nb = pl.num_programs(0)
    slot = b % 2
    tb = n2_ref.shape[0]
    piece_src = xbuf.at[0, pl.ds(0, SEG_ALIGN), :]
    piece_dst = xs_hbm.at[pl.ds(0, SEG_ALIGN), :]

    @pl.when(b >= 2)
    def _():
        _wait_rows(tot_ref[jnp.maximum(b - 2, 0)], piece_src, piece_dst, sems.at[slot])

    pos = _local_positions(idx_ref[...], rank_ref[...], off_ref, b, n_exp)
    jio = lax.broadcasted_iota(I32, (mp, tb), 0)
    hit = jio == pos[0:1, :]
    for k in range(1, TOP_K):
        hit = hit | (jio == pos[k:k + 1, :])
    xbuf[slot] = _dot(jnp.where(hit, 1.0, 0.0).astype(BF16), n2_ref[...])

    for e in range(n_exp):
        o = off_ref[b * n_exp + e]
        s = start_ref[b * n_exp + e]

        def issue(j, carry, o=o, s=s):
            src = xbuf.at[slot, pl.ds(pl.multiple_of(o + j * SEG_ALIGN, SEG_ALIGN), SEG_ALIGN), :]
            dst = xs_hbm.at[pl.ds(pl.multiple_of(s + j * SEG_ALIGN, SEG_ALIGN), SEG_ALIGN), :]
            pltpu.make_async_copy(src, dst, sems.at[slot]).start()
            return carry
        lax.fori_loop(0, p8_ref[b * n_exp + e] // SEG_ALIGN, issue, 0)

    @pl.when(b == nb - 1)
    def _():
        zbuf[...] = jnp.zeros_like(zbuf)
        for e in range(n_exp):
            ts = tstart_ref[e]

            def issue_zero(j, carry, ts=ts):
                dst = xs_hbm.at[pl.ds(pl.multiple_of(ts + j * SEG_ALIGN, SEG_ALIGN), SEG_ALIGN), :]
                pltpu.make_async_copy(zbuf, dst, zsem).start()
                return carry
            lax.fori_loop(0, tlen_ref[e] // SEG_ALIGN, issue_zero, 0)

        @pl.when(nb >= 2)
        def _():
            _wait_rows(tot_ref[jnp.maximum(nb - 2, 0)], piece_src, piece_dst, sems.at[1 - slot])
        _wait_rows(tot_ref[nb - 1], piece_src, piece_dst, sems.at[slot])
        for e in range(n_exp):
            _wait_rows(tlen_ref[e], zbuf, piece_dst, zsem)


def _dispatch(n2, idx, rank, meta, n_exp, max_rows):
    n_total, d = n2.shape
    tb = TOKEN_BLOCK
    nb = n_total // tb
    mp = _round_up(TOP_K * tb + (SEG_ALIGN - 1) * n_exp, LANES)
    tok = pl.BlockSpec((SUBLANES, tb), lambda b, *_: (0, b))
    return pl.pallas_call(
        functools.partial(_dispatch_kernel, n_exp=n_exp, mp=mp),
        grid_spec=pltpu.PrefetchScalarGridSpec(
            num_scalar_prefetch=6,
            grid=(nb,),
            in_specs=[pl.BlockSpec((tb, d), lambda b, *_: (b, 0)), tok, tok],
            out_specs=pl.BlockSpec(memory_space=pl.ANY),
            scratch_shapes=[
                pltpu.VMEM((2, mp, d), F32),
                pltpu.VMEM((SEG_ALIGN, d), F32),
                pltpu.SemaphoreType.DMA((2,)),
                pltpu.SemaphoreType.DMA,
            ],
        ),
        out_shape=jax.ShapeDtypeStruct((max_rows, d), F32),
        compiler_params=pltpu.CompilerParams(dimension_semantics=("arbitrary",), vmem_limit_bytes=VMEM_LIMIT),
        name="moe_dispatch",
    )(meta["off"], meta["start"], meta["p8"], meta["tot"], meta["tail_start"], meta["tail_len"], n2, idx, rank)


def _expert_kernel(te_ref, nt_ref, x_ref, w1_ref, b1_ref, w2_ref, b2_ref, y_ref, w1b, w2b):
    i = pl.program_id(0)
    dff = w2_ref.shape[1]

    @pl.when(i < nt_ref[0])
    def _():
        changed = (i == 0) | (te_ref[i] != te_ref[jnp.maximum(i - 1, 0)])

        @pl.when(changed)
        def _():
            w1b[...] = w1_ref[0].astype(BF16)
            w2b[...] = w2_ref[0].astype(BF16)

        hdn = _dot(x_ref[...].astype(BF16), w1b[...]) + b1_ref[0]
        gate = jnp.minimum(hdn[:, 0:dff], SWIGLU_LIMIT)
        up = jnp.clip(hdn[:, dff:2 * dff], -SWIGLU_LIMIT, SWIGLU_LIMIT)
        act = (up + 1.0) * gate * _sigmoid(SWIGLU_ALPHA * gate)
        y_ref[...] = _dot(act.astype(BF16), w2b[...]) + b2_ref[0]


def _experts(xs, meta, w1, b1, w2, b2):
    max_rows, d = xs.shape
    n_exp, _, dff2 = w1.shape
    dff = dff2 // 2
    tm = EXPERT_TILE
    max_tiles = max_rows // tm
    row_map = lambda i, te, nt: (jnp.minimum(i, nt[0] - 1), 0)
    exp_map = lambda i, te, nt: (te[jnp.minimum(i, nt[0] - 1)], 0, 0)
    return pl.pallas_call(
        _expert_kernel,
        grid_spec=pltpu.PrefetchScalarGridSpec(
            num_scalar_prefetch=2,
            grid=(max_tiles,),
            in_specs=[
                pl.BlockSpec((tm, d), row_map),
                pl.BlockSpec((1, d, dff2), exp_map),
                pl.BlockSpec((1, 1, dff2), exp_map),
                pl.BlockSpec((1, dff, d), exp_map),
                pl.BlockSpec((1, 1, d), exp_map),
            ],
            out_specs=pl.BlockSpec((tm, d), row_map),
            scratch_shapes=[pltpu.VMEM((d, dff2), BF16), pltpu.VMEM((dff, d), BF16)],
        ),
        out_shape=jax.ShapeDtypeStruct((max_rows, d), F32),
        compiler_params=pltpu.CompilerParams(dimension_semantics=("arbitrary",), vmem_limit_bytes=VMEM_LIMIT),
        name="moe_experts",
    )(meta["tile_expert"], meta["n_tiles"], xs, w1, b1, w2, b2)


def _combine_kernel(off_ref, start_ref, p8_ref, tot_ref,
                    h2_ref, idx_ref, rank_ref, gw_ref, p_ref, wpg_ref, bpg_ref, wpp_ref, gple_ref, gfin_ref,
                    ys_hbm, *rest, final, n_exp, mp):
    if final:
        hn_ref, out_ref, ybuf, sems = rest
    else:
        hn_ref, ybuf, sems = rest
    b = pl.program_id(0)
    nb = pl.num_programs(0)
    slot = b % 2
    tb = h2_ref.shape[0]

    def fetch(blk, sl):
        for e in range(n_exp):
            o = off_ref[blk * n_exp + e]
            s = start_ref[blk * n_exp + e]

            def issue(j, carry, o=o, s=s):
                src = ys_hbm.at[pl.ds(pl.multiple_of(s + j * SEG_ALIGN, SEG_ALIGN), SEG_ALIGN), :]
                dst = ybuf.at[sl, pl.ds(pl.multiple_of(o + j * SEG_ALIGN, SEG_ALIGN), SEG_ALIGN), :]
                pltpu.make_async_copy(src, dst, sems.at[sl]).start()
                return carry
            lax.fori_loop(0, p8_ref[blk * n_exp + e] // SEG_ALIGN, issue, 0)

    @pl.when(b == 0)
    def _():
        ybuf[...] = jnp.zeros_like(ybuf)
        fetch(0, 0)

    @pl.when(b + 1 < nb)
    def _():
        fetch(b + 1, 1 - slot)

    _wait_rows(tot_ref[b], ys_hbm.at[pl.ds(0, SEG_ALIGN), :], ybuf.at[0, pl.ds(0, SEG_ALIGN), :], sems.at[slot])

    pos = _local_positions(idx_ref[...], rank_ref[...], off_ref, b, n_exp)
    gw = gw_ref[...]
    jio = lax.broadcasted_iota(I32, (mp, tb), 0)
    cg = jnp.where(jio == pos[0:1, :], gw[0:1, :], 0.0)
    for k in range(1, TOP_K):
        cg = cg + jnp.where(jio == pos[k:k + 1, :], gw[k:k + 1, :], 0.0)
    moe = lax.dot_general(cg.astype(BF16), ybuf[slot].astype(BF16), (((0,), (0,)), ((), ())),
                          preferred_element_type=F32)
    h3 = h2_ref[...] + moe
    n3 = _rms(h3, gple_ref[...]).astype(BF16)
    gate = _sigmoid(_dot(n3, wpg_ref[...]) + bpg_ref[...])
    h4 = h3 + gate * _dot(p_ref[...], wpp_ref[...])
    hn_ref[...] = h4
    if final:
        out_ref[...] = _rms(h4, gfin_ref[...])


def _combine(h2, idx, rank, gw, p, ys, meta, w, g_final, n_exp, final):
    n_total, d = h2.shape
    ple = p.shape[1]
    tb = TOKEN_BLOCK
    nb = n_total // tb
    mp = _round_up(TOP_K * tb + (SEG_ALIGN - 1) * n_exp, LANES)
    const2 = lambda b, *_: (0, 0)
    tok = pl.BlockSpec((SUBLANES, tb), lambda b, *_: (0, b))
    row = pl.BlockSpec((tb, d), lambda b, *_: (b, 0))
    n_out = 2 if final else 1
    return pl.pallas_call(
        functools.partial(_combine_kernel, final=final, n_exp=n_exp, mp=mp),
        grid_spec=pltpu.PrefetchScalarGridSpec(
            num_scalar_prefetch=4,
            grid=(nb,),
            in_specs=[
                row, tok, tok, tok,
                pl.BlockSpec((tb, ple), lambda b, *_: (b, 0)),
                pl.BlockSpec((d, d), const2),
                pl.BlockSpec((1, d), const2),
                pl.BlockSpec((ple, d), const2),
                pl.BlockSpec((1, d), const2),
                pl.BlockSpec((1, d), const2),
                pl.BlockSpec(memory_space=pl.ANY),
            ],
            out_specs=[row] * n_out,
            scratch_shapes=[pltpu.VMEM((2, mp, d), F32), pltpu.SemaphoreType.DMA((2,))],
        ),
        out_shape=[jax.ShapeDtypeStruct((n_total, d), F32)] * n_out,
        compiler_params=pltpu.CompilerParams(dimension_semantics=("arbitrary",), vmem_limit_bytes=VMEM_LIMIT),
        name="moe_combine_ple",
    )(meta["off"], meta["start"], meta["p8"], meta["tot"],
      h2, idx, rank, gw, p, w["w_pg"], w["b_pg"], w["w_pp"], w["g_ple"], g_final, ys)


def _layer_weights(l, batch, g_mix, w_in, g_v, w_s, b_s, conv_w, conv_b, w_a, b_a, w_x, b_x, lam,
                   g_oa, g_ob, w_out, g_ffn, w_r, b_r, g_ple, w_pg, b_pg, w_pp):
    d = w_in.shape[1]
    dh = d // 2
    gd = dh // GMLP_GROUPS
    n_exp = w_r.shape[2]
    row = lambda a: a[l].reshape(1, -1).astype(F32)
    wt = jnp.where(jnp.tril(jnp.ones((CHUNK, CHUNK), bool)), w_s[l], 0.0).astype(BF16)
    kron = jnp.einsum("gts,bc->gtbsc", wt, jnp.eye(batch, dtype=BF16)).reshape(
        GMLP_GROUPS, CHUNK * batch, CHUNK * batch)
    hr = HALF * batch
    kd = jnp.stack([kron[:, :hr, :hr], kron[:, hr:, hr:]])
    k10 = kron[:, hr:, :hr]
    bs_rows = jnp.broadcast_to(b_s[l].T[:, None, :, None], (CHUNK, batch, GMLP_GROUPS, gd)).reshape(
        2, hr, dh).astype(F32)
    eye_h = jnp.eye(LRU_HEADS, dtype=F32)
    block_diag = lambda w: jnp.einsum("hij,hk->hikj", w.astype(F32), eye_h).reshape(dh, dh)
    wa_f32, wx_f32 = block_diag(w_a[l]), block_diag(w_x[l])
    w_r_pad = jnp.zeros((d, LANES), F32).at[:, :n_exp].set(w_r[l])
    return dict(
        g_mix=row(g_mix), w_in=w_in[l].astype(BF16), w_in_f32=w_in[l].astype(F32), g_v=row(g_v),
        kd=kd, k10=k10, bs_rows=bs_rows,
        w0_row=jnp.repeat(w_s[l][:, 0, 0], gd).reshape(1, dh).astype(F32),
        b0_row=jnp.repeat(b_s[l][:, 0], gd).reshape(1, dh).astype(F32),
        conv_w=conv_w[l].astype(F32), conv_b=row(conv_b),
        wa=wa_f32.astype(BF16), wa_f32=wa_f32, b_a=row(b_a), wx=wx_f32.astype(BF16), wx_f32=wx_f32,
        b_x=row(b_x), lam=row(lam),
        g_oa=row(g_oa), g_ob=row(g_ob), w_out=w_out[l].astype(BF16), w_out_f32=w_out[l].astype(F32),
        g_ffn=row(g_ffn),
        w_r=w_r_pad, b_r=b_r[l].reshape(n_exp, 1).astype(F32),
        g_ple=row(g_ple), w_pg=w_pg[l].astype(BF16), b_pg=row(b_pg), w_pp=w_pp[l].astype(BF16))


def kernel(x_prompt, x_sample, state_conv, state_lru, p_prompt, p_sample, g_mix, w_in, g_v, w_s, b_s, conv_w, conv_b, w_a, b_a, w_x, b_x, lam, g_oa, g_ob, w_out, g_ffn, w_r, b_r, w1, b1, w2, b2, g_ple, w_pg, b_pg, w_pp, g_final):
    batch, seq, d = x_prompt.shape
    dec_batch, dec_seq, _ = x_sample.shape
    depth = w_in.shape[0]
    n_exp = w_r.shape[2]
    dh = d // 2
    ple = p_prompt.shape[-1]
    assert dec_seq == 1 and seq % CHUNK == 0 and batch % SUBLANES == 0
    assert dh == GMLP_GROUPS * LANES and dh % (LRU_HEADS * (LANES // 2)) == 0
    n_prompt = seq * batch
    assert n_prompt % TOKEN_BLOCK == 0
    n_valid = n_prompt + dec_batch
    n_total = _round_up(n_valid, TOKEN_BLOCK)
    s_rows = n_total - n_prompt
    assert s_rows == TOKEN_BLOCK
    nb = n_total // TOKEN_BLOCK
    max_rows = _max_sorted_rows(n_valid, nb, n_exp)
    pad = n_total - n_valid

    h = jnp.concatenate([x_prompt.transpose(1, 0, 2).reshape(n_prompt, d), x_sample.reshape(dec_batch, d),
                         jnp.zeros((pad, d), F32)], axis=0)
    p_all = jnp.concatenate([p_prompt.transpose(0, 2, 1, 3).reshape(depth, n_prompt, ple),
                             p_sample.reshape(depth, dec_batch, ple),
                             jnp.zeros((depth, pad, ple), F32)], axis=1).astype(BF16)
    sconv = jnp.pad(state_conv.transpose(0, 2, 1, 3), ((0, 0), (0, 0), (0, s_rows - dec_batch), (0, 0)))
    slru = jnp.pad(state_lru, ((0, 0), (0, s_rows - dec_batch), (0, 0)))
    g_fin = g_final.reshape(1, d).astype(F32)

    v_p, conv_p, lru_p, v_s, conv_s, lru_s = [], [], [], [], [], []
    out = None
    for l in range(depth):
        w = _layer_weights(l, batch, g_mix, w_in, g_v, w_s, b_s, conv_w, conv_b, w_a, b_a, w_x, b_x, lam,
                           g_oa, g_ob, w_out, g_ffn, w_r, b_r, g_ple, w_pg, b_pg, w_pp)
        y, vp, cp, lp = _mix_prompt(h, n_prompt, batch, w)
        post_out = _post(y, h, w, n_valid, n_exp)
        h2, n2, idx, gw, rank, cnt, vs, cs, ls = _sample_front(
            h, post_out, nb - 1, w, sconv[l], slru[l], n_valid, n_exp)
        meta = _route_meta(cnt[:, 0, :].astype(I32), max_rows // EXPERT_TILE)
        xs = _dispatch(n2, idx, rank, meta, n_exp, max_rows)
        ys = _experts(xs, meta, w1[l], b1[l].reshape(n_exp, 1, -1), w2[l], b2[l].reshape(n_exp, 1, -1))
        final = l == depth - 1
        res = _combine(h2, idx, rank, gw, p_all[l], ys, meta, w, g_fin, n_exp, final)
        h = res[0]
        if final:
            out = res[1]
        v_p.append(vp.reshape(CHUNK, batch, dh).transpose(1, 0, 2))
        conv_p.append(cp.reshape(CONV_W - 1, batch, dh).transpose(1, 0, 2))
        lru_p.append(lp)
        v_s.append(vs[:dec_batch].reshape(dec_batch, 1, dh))
        conv_s.append(cs[:, :dec_batch].transpose(1, 0, 2))
        lru_s.append(ls[:dec_batch])

    y_prompt = out[:n_prompt].reshape(seq, batch, d).transpose(1, 0, 2)
    y_sample = out[n_prompt:n_valid].reshape(dec_batch, 1, d)
    return (y_prompt, y_sample, jnp.stack(v_p), jnp.stack(conv_p), jnp.stack(lru_p),
            jnp.stack(v_s), jnp.stack(conv_s), jnp.stack(lru_s))
```

```python
import functools

import jax
import jax.numpy as jnp
from jax import lax
from jax.experimental import pallas as pl
from jax.experimental.pallas import tpu as pltpu

F32 = jnp.float32
BF16 = jnp.bfloat16
I32 = jnp.int32

CHUNK = 128
HALF = CHUNK // 2
GMLP_GROUPS = 4
LRU_HEADS = 8
CONV_W = 4
LRU_C = 8.0
TOP_K = 4
SWIGLU_LIMIT = 7.0
SWIGLU_ALPHA = 1.702
EPS = 1e-6

LANES = 128
SUBLANES = 8
TOKEN_BLOCK = 256
EXPERT_TILE = 512
SEG_ALIGN = SUBLANES
VMEM_LIMIT = 56 * 1024 * 1024


def _round_up(x, m):
    return (x + m - 1) // m * m


def _dot(a, b):
    return jnp.dot(a, b, preferred_element_type=F32)


def _gelu(x):
    return 0.5 * x * (1.0 + jnp.tanh(0.7978845608028654 * (x + 0.044715 * (x * x * x))))


def _sigmoid(x):
    return 1.0 / (1.0 + jnp.exp(-x))


def _softplus(x):
    return jnp.maximum(x, 0.0) + jnp.log1p(jnp.exp(-jnp.abs(x)))


def _rms(x, g):
    ms = jnp.mean(x * x, axis=-1, keepdims=True)
    return x * lax.rsqrt(ms + EPS) * g


def _group_norm_128(x, g):
    outs = []
    for j in range(x.shape[1] // LANES):
        blk = x[:, j * LANES:(j + 1) * LANES]
        ms = jnp.mean(blk * blk, axis=-1, keepdims=True)
        outs.append(blk * lax.rsqrt(ms + EPS))
    return jnp.concatenate(outs, axis=1) * g


def _group_norm_64(x, g):
    half = LANES // 2
    lo_mask = lax.broadcasted_iota(I32, (1, LANES), 1) < half
    outs = []
    for j in range(x.shape[1] // LANES):
        blk = x[:, j * LANES:(j + 1) * LANES]
        sq = blk * blk
        lo = jnp.sum(jnp.where(lo_mask, sq, 0.0), axis=-1, keepdims=True)
        hi = jnp.sum(jnp.where(lo_mask, 0.0, sq), axis=-1, keepdims=True)
        ms = jnp.where(lo_mask, lo, hi) * (1.0 / half)
        outs.append(blk * lax.rsqrt(ms + EPS))
    return jnp.concatenate(outs, axis=1) * g


def _split_bf16(x):
    hi = x.astype(BF16)
    lo = (x - hi.astype(F32)).astype(BF16)
    return hi, lo


def _dot_bf16(x, w):
    return _dot(x.astype(BF16), w)


def _dot_3pass(x, w):
    x_hi, x_lo = _split_bf16(x)
    w_hi, w_lo = _split_bf16(w)
    return _dot(x_hi, w_hi) + (_dot(x_lo, w_hi) + _dot(x_hi, w_lo))


def _lru_gates(xc, wa_ref, ba_ref, wx_ref, bx_ref, lam_ref, mm):
    r = _sigmoid(mm(xc, wa_ref[...]) + ba_ref[...])
    i = _sigmoid(mm(xc, wx_ref[...]) + bx_ref[...])
    log_a = (-LRU_C * r) * _softplus(-lam_ref[...])
    a = jnp.exp(log_a)
    mult = jnp.sqrt(-jnp.tanh(log_a) * (a * a + 1.0))
    return a, mult, i


def _mix_prompt_kernel(h_ref, gmix_ref, win_ref, gv_ref, kd_ref, k10_ref, bs_ref, cw_ref, cb_ref,
                       wa_ref, ba_ref, wx_ref, bx_ref, lam_ref, goa_ref, gob_ref,
                       y_ref, v_ref, conv_ref, lru_ref,
                       hstate, xpad, vprev, a_s, b_s, hs_s, *, batch):
    c = pl.program_id(0)
    n_steps = pl.num_programs(0)
    rows = h_ref.shape[0]
    dh = gv_ref.shape[1]
    tail = (CONV_W - 1) * batch
    par = c % 2

    @pl.when(c == 0)
    def _():
        hstate[...] = jnp.zeros_like(hstate)
        xpad[0:tail, :] = jnp.zeros((tail, dh), F32)

    @pl.when(par == 0)
    def _():
        vprev[...] = jnp.zeros_like(vprev)

    n = _rms(h_ref[...], gmix_ref[...]).astype(BF16)

    v = _gelu(_dot(n, win_ref[:, dh:2 * dh]))
    vn = _group_norm_128(v, gv_ref[...])
    vb = vn.astype(BF16)
    s_parts = []
    for g in range(GMLP_GROUPS):
        sl = slice(g * LANES, (g + 1) * LANES)
        s_parts.append(_dot(kd_ref[par, g], vb[:, sl]) + _dot(k10_ref[g], vprev[:, sl]))
    s = jnp.concatenate(s_parts, axis=1) + bs_ref[par]
    vprev[...] = vb

    @pl.when(c >= n_steps - 2)
    def _():
        v_ref[...] = vn

    u = _gelu(_dot(n, win_ref[:, 0:dh]))
    y_ref[:, 0:dh] = _group_norm_128(u * s, goa_ref[...]).astype(BF16)

    xpad[tail:tail + rows, :] = _dot(n, win_ref[:, 2 * dh:3 * dh])
    xc = cb_ref[...] + cw_ref[0:1, :] * xpad[0:rows, :]
    for k in range(1, CONV_W):
        xc = xc + cw_ref[k:k + 1, :] * xpad[k * batch:k * batch + rows, :]
    new_tail = xpad[rows:rows + tail, :]
    xpad[0:tail, :] = new_tail

    a, mult, i = _lru_gates(xc, wa_ref, ba_ref, wx_ref, bx_ref, lam_ref, _dot_bf16)
    row = lax.broadcasted_iota(I32, (rows, 1), 0)
    mult = jnp.where((c == 0) & (row < batch), 1.0, mult)
    a_s[...] = a
    b_s[...] = mult * i * xc

    def step(t, h):
        r0 = pl.multiple_of(t * batch, batch)
        h = a_s[pl.ds(r0, batch), :] * h + b_s[pl.ds(r0, batch), :]
        hs_s[pl.ds(r0, batch), :] = h
        return h

    h_last = lax.fori_loop(0, rows // batch, step, hstate[...], unroll=8)
    hstate[...] = h_last

    gb = _gelu(_dot(n, win_ref[:, 3 * dh:4 * dh]))
    y_ref[:, dh:2 * dh] = _group_norm_64(hs_s[...] * gb, gob_ref[...]).astype(BF16)

    @pl.when(c == n_steps - 1)
    def _():
        conv_ref[...] = new_tail
        lru_ref[...] = h_last


def _mix_prompt(h, n_rows, batch, w):
    d = h.shape[1]
    dh = d // 2
    rows = HALF * batch
    n_steps = n_rows // rows
    tail = (CONV_W - 1) * batch
    const2 = lambda c: (0, 0)
    const3 = lambda c: (0, 0, 0)
    const4 = lambda c: (0, 0, 0, 0)
    return pl.pallas_call(
        functools.partial(_mix_prompt_kernel, batch=batch),
        grid=(n_steps,),
        in_specs=[
            pl.BlockSpec((rows, d), lambda c: (c, 0)),
            pl.BlockSpec((1, d), const2),
            pl.BlockSpec((d, 2 * d), const2),
            pl.BlockSpec((1, dh), const2),
            pl.BlockSpec((2, GMLP_GROUPS, rows, rows), const4),
            pl.BlockSpec((GMLP_GROUPS, rows, rows), const3),
            pl.BlockSpec((2, rows, dh), const3),
            pl.BlockSpec((CONV_W, dh), const2),
            pl.BlockSpec((1, dh), const2),
            pl.BlockSpec((dh, dh), const2),
            pl.BlockSpec((1, dh), const2),
            pl.BlockSpec((dh, dh), const2),
            pl.BlockSpec((1, dh), const2),
            pl.BlockSpec((1, dh), const2),
            pl.BlockSpec((1, dh), const2),
            pl.BlockSpec((1, dh), const2),
        ],
        out_specs=[
            pl.BlockSpec((rows, d), lambda c: (c, 0)),
            pl.BlockSpec((rows, dh), lambda c: (jnp.maximum(c - (n_steps - 2), 0), 0)),
            pl.BlockSpec((tail, dh), const2),
            pl.BlockSpec((batch, dh), const2),
        ],
        out_shape=[
            jax.ShapeDtypeStruct((n_rows, d), BF16),
            jax.ShapeDtypeStruct((2 * rows, dh), F32),
            jax.ShapeDtypeStruct((tail, dh), F32),
            jax.ShapeDtypeStruct((batch, dh), F32),
        ],
        scratch_shapes=[
            pltpu.VMEM((batch, dh), F32),
            pltpu.VMEM((tail + rows, dh), F32),
            pltpu.VMEM((rows, dh), BF16),
            pltpu.VMEM((rows, dh), F32),
            pltpu.VMEM((rows, dh), F32),
            pltpu.VMEM((rows, dh), F32),
        ],
        compiler_params=pltpu.CompilerParams(dimension_semantics=("arbitrary",), vmem_limit_bytes=VMEM_LIMIT),
        name="mix_prompt",
    )(h, w["g_mix"], w["w_in"], w["g_v"], w["kd"], w["k10"], w["bs_rows"], w["conv_w"], w["conv_b"],
      w["wa"], w["b_a"], w["wx"], w["b_x"], w["lam"], w["g_oa"], w["g_ob"])


def _sample_front_kernel(h_ref, h2_any, n2_any, gw_any, pos_any, cnt_any,
                         gmix_ref, win_ref, gv_ref, w0_ref, b0_ref, cw_ref, cb_ref,
                         wa_ref, ba_ref, wx_ref, bx_ref, lam_ref, goa_ref, gob_ref, sconv_ref, slru_ref,
                         wout_ref, gffn_ref, wr_ref, br_ref,
                         h2_ref, n2_ref, gw_ref, pos_ref, cnt_ref, v_ref, conv_ref, lru_ref,
                         *, block, n_valid, n_exp):
    del h2_any, n2_any, gw_any, pos_any, cnt_any
    dh = gv_ref.shape[1]
    h = h_ref[...]
    n = _rms(h, gmix_ref[...])
    proj = lambda lo: _dot_3pass(n, win_ref[:, lo:lo + dh])

    vn = _group_norm_128(_gelu(proj(dh)), gv_ref[...])
    v_ref[...] = vn
    s = vn * w0_ref[...] + b0_ref[...]
    ya = _group_norm_128(_gelu(proj(0)) * s, goa_ref[...])

    xb = proj(2 * dh)
    xc = cb_ref[...] + cw_ref[CONV_W - 1:CONV_W, :] * xb
    for k in range(CONV_W - 1):
        xc = xc + cw_ref[k:k + 1, :] * sconv_ref[k]
    for k in range(CONV_W - 2):
        conv_ref[k] = sconv_ref[k + 1]
    conv_ref[CONV_W - 2] = xb

    a, mult, i = _lru_gates(xc, wa_ref, ba_ref, wx_ref, bx_ref, lam_ref, _dot_3pass)
    h_new = a * slru_ref[...] + mult * i * xc
    lru_ref[...] = h_new
    yb = _group_norm_64(h_new * _gelu(proj(3 * dh)), gob_ref[...])

    h2 = h + _dot_3pass(ya, wout_ref[0:dh, :]) + _dot_3pass(yb, wout_ref[dh:2 * dh, :])
    _route_block(h2, gffn_ref, wr_ref, br_ref, h2_ref, n2_ref, gw_ref, pos_ref, cnt_ref,
                 block=block, n_valid=n_valid, n_exp=n_exp)


def _sample_front(h, post_out, block, w, sconv, slru, n_valid, n_exp):
    d = h.shape[1]
    dh = d // 2
    rows = TOKEN_BLOCK
    const2 = lambda i: (0, 0)
    const3 = lambda i: (0, 0, 0)
    vec = pl.BlockSpec((1, dh), const2)
    anyspace = pl.BlockSpec(memory_space=pl.ANY)
    blk_row = pl.BlockSpec((rows, d), lambda i: (block, 0))
    blk_tok = pl.BlockSpec((SUBLANES, rows), lambda i: (0, block))
    return pl.pallas_call(
        functools.partial(_sample_front_kernel, block=block, n_valid=n_valid, n_exp=n_exp),
        grid=(1,),
        in_specs=[
            blk_row,
            anyspace, anyspace, anyspace, anyspace, anyspace,
            pl.BlockSpec((1, d), const2),
            pl.BlockSpec((d, 2 * d), const2),
            vec, vec, vec,
            pl.BlockSpec((CONV_W, dh), const2),
            vec,
            pl.BlockSpec((dh, dh), const2),
            vec,
            pl.BlockSpec((dh, dh), const2),
            vec, vec, vec, vec,
            pl.BlockSpec((CONV_W - 1, rows, dh), const3),
            pl.BlockSpec((rows, dh), const2),
            pl.BlockSpec((d, d), const2),
            pl.BlockSpec((1, d), const2),
            pl.BlockSpec((d, LANES), const2),
            pl.BlockSpec((n_exp, 1), const2),
        ],
        out_specs=[
            blk_row, blk_row, blk_tok, blk_tok,
            pl.BlockSpec((1, SUBLANES, n_exp), lambda i: (block, 0, 0)),
            pl.BlockSpec((rows, dh), const2),
            pl.BlockSpec((CONV_W - 1, rows, dh), const3),
            pl.BlockSpec((rows, dh), const2),
        ],
        out_shape=[jax.ShapeDtypeStruct(a.shape, a.dtype) for a in post_out] + [
            jax.ShapeDtypeStruct((rows, dh), F32),
            jax.ShapeDtypeStruct((CONV_W - 1, rows, dh), F32),
            jax.ShapeDtypeStruct((rows, dh), F32),
        ],
        input_output_aliases={1 + k: k for k in range(len(post_out))},
        compiler_params=pltpu.CompilerParams(dimension_semantics=("arbitrary",), vmem_limit_bytes=VMEM_LIMIT),
        name="sample_front",
    )(h, *post_out, w["g_mix"], w["w_in_f32"], w["g_v"], w["w0_row"], w["b0_row"], w["conv_w"], w["conv_b"],
      w["wa_f32"], w["b_a"], w["wx_f32"], w["b_x"], w["lam"], w["g_oa"], w["g_ob"], sconv, slru,
      w["w_out_f32"], w["g_ffn"], w["w_r"], w["b_r"])


def _route_block(h2, gffn_ref, wr_ref, br_ref, h2_ref, n2_ref, gw_ref, pos_ref, cnt_ref,
                 *, block, n_valid, n_exp):
    tb = h2.shape[0]
    h2_ref[...] = h2
    n2 = _rms(h2, gffn_ref[...])
    n2_ref[...] = n2.astype(BF16)

    logits = _dot_3pass(n2, wr_ref[...])
    lt = logits.T[0:n_exp, :] + br_ref[...]

    eio = lax.broadcasted_iota(I32, (n_exp, tb), 0).astype(F32)
    col = block * tb + lax.broadcasted_iota(I32, (1, tb), 1)
    valid = col < n_valid
    vals, hots = [], []
    for _ in range(TOP_K):
        m = jnp.max(lt, axis=0, keepdims=True)
        sel = jnp.min(jnp.where(lt == m, eio, float(n_exp)), axis=0, keepdims=True)
        hot = eio == sel
        lt = jnp.where(hot, -jnp.inf, lt)
        vals.append(m)
        hots.append(hot)
    exps = [jnp.exp(v - vals[0]) for v in vals]
    den = exps[0]
    for e in exps[1:]:
        den = den + e

    occ = jnp.zeros((n_exp, tb), F32)
    for hot in hots:
        occ = occ + jnp.where(hot & valid, 1.0, 0.0)
    occ_b = occ.astype(BF16)
    before = (lax.broadcasted_iota(I32, (tb, tb), 0) < lax.broadcasted_iota(I32, (tb, tb), 1))
    ranks_e = _dot(occ_b, jnp.where(before, 1.0, 0.0).astype(BF16))
    cnt_ref[0] = lax.dot_general(jnp.ones((SUBLANES, tb), BF16), occ_b, (((1,), (1,)), ((), ())),
                                 preferred_element_type=F32)
    cnt_col = jnp.sum(occ, axis=1, keepdims=True)
    units = jnp.floor((cnt_col + (SEG_ALIGN - 1.0)) * (1.0 / SEG_ALIGN))
    earlier = (lax.broadcasted_iota(I32, (n_exp, n_exp), 1) < lax.broadcasted_iota(I32, (n_exp, n_exp), 0))
    seg_off = _dot(jnp.where(earlier, 1.0, 0.0).astype(BF16),
                   jnp.broadcast_to(units, (n_exp, LANES)).astype(BF16))[:, 0:1] * float(SEG_ALIGN)
    rows_e = ranks_e + seg_off

    rio = lax.broadcasted_iota(I32, (SUBLANES, tb), 0)
    gw_out = jnp.zeros((SUBLANES, tb), F32)
    pos_out = jnp.full((SUBLANES, tb), -1, I32)
    for k in range(TOP_K):
        pos_k = jnp.sum(jnp.where(hots[k], rows_e, 0.0), axis=0, keepdims=True).astype(I32)
        gw_out = jnp.where(rio == k, exps[k] / den, gw_out)
        pos_out = jnp.where(rio == k, jnp.where(valid, pos_k, -1), pos_out)
    gw_ref[...] = gw_out
    pos_ref[...] = pos_out


def _post_kernel(y_ref, h_ref, wout_ref, gffn_ref, wr_ref, br_ref,
                 h2_ref, n2_ref, gw_ref, pos_ref, cnt_ref, *, n_valid, n_exp):
    h2 = h_ref[...] + _dot(y_ref[...], wout_ref[...])
    _route_block(h2, gffn_ref, wr_ref, br_ref, h2_ref, n2_ref, gw_ref, pos_ref, cnt_ref,
                 block=pl.program_id(0), n_valid=n_valid, n_exp=n_exp)


def _post(y, h, w, n_valid, n_exp):
    n_total, d = h.shape
    tb = TOKEN_BLOCK
    nb = n_total // tb
    const2 = lambda b: (0, 0)
    tok = pl.BlockSpec((SUBLANES, tb), lambda b: (0, b))
    return pl.pallas_call(
        functools.partial(_post_kernel, n_valid=n_valid, n_exp=n_exp),
        grid=(y.shape[0] // tb,),
        in_specs=[
            pl.BlockSpec((tb, d), lambda b: (b, 0)),
            pl.BlockSpec((tb, d), lambda b: (b, 0)),
            pl.BlockSpec((d, d), const2),
            pl.BlockSpec((1, d), const2),
            pl.BlockSpec((d, LANES), const2),
            pl.BlockSpec((n_exp, 1), const2),
        ],
        out_specs=[
            pl.BlockSpec((tb, d), lambda b: (b, 0)),
            pl.BlockSpec((tb, d), lambda b: (b, 0)),
            tok, tok,
            pl.BlockSpec((1, SUBLANES, n_exp), lambda b: (b, 0, 0)),
        ],
        out_shape=[
            jax.ShapeDtypeStruct((n_total, d), F32),
            jax.ShapeDtypeStruct((n_total, d), BF16),
            jax.ShapeDtypeStruct((SUBLANES, n_total), F32),
            jax.ShapeDtypeStruct((SUBLANES, n_total), I32),
            jax.ShapeDtypeStruct((nb, SUBLANES, n_exp), F32),
        ],
        compiler_params=pltpu.CompilerParams(dimension_semantics=("arbitrary",), vmem_limit_bytes=VMEM_LIMIT),
        name="post_router",
    )(y, h, w["w_out"], w["g_ffn"], w["w_r"], w["b_r"])


def _max_sorted_rows(n_valid, nb, n_exp):
    worst = TOP_K * n_valid + (SEG_ALIGN - 1) * nb * n_exp + (EXPERT_TILE - SEG_ALIGN) * n_exp
    return _round_up(worst, EXPERT_TILE)


def _local_rows(n_exp):
    return _round_up(TOP_K * TOKEN_BLOCK + (SEG_ALIGN - 1) * n_exp, LANES)


def _route_meta(cnt, max_tiles):
    nb, n_exp = cnt.shape
    p8 = (cnt + SEG_ALIGN - 1) // SEG_ALIGN * SEG_ALIGN
    off = jnp.cumsum(p8, axis=1) - p8
    tot = jnp.sum(p8, axis=1)
    seg = jnp.sum(p8, axis=0)
    reg = (seg + EXPERT_TILE - 1) // EXPERT_TILE * EXPERT_TILE
    reg_start = jnp.cumsum(reg) - reg
    start = reg_start[None, :] + jnp.cumsum(p8, axis=0) - p8
    piece_row = jnp.arange(_local_rows(n_exp) // SEG_ALIGN, dtype=I32) * SEG_ALIGN
    inside = (off[:, None, :] <= piece_row[None, :, None]) & (piece_row[None, :, None] < (off + p8)[:, None, :])
    piece_dst = piece_row[None, :] + jnp.sum(jnp.where(inside, (start - off)[:, None, :], 0), axis=2)
    tiles_end = jnp.cumsum(reg // EXPERT_TILE)
    tile_ids = jnp.arange(max_tiles, dtype=I32)
    tile_expert = jnp.minimum(jnp.sum((tiles_end[None, :] <= tile_ids[:, None]).astype(I32), axis=1), n_exp - 1)
    return dict(piece_dst=piece_dst.reshape(-1).astype(I32), tot=tot.astype(I32),
                tail_start=(reg_start + seg).astype(I32), tail_len=(reg - seg).astype(I32),
                tile_expert=tile_expert.astype(I32), n_tiles=tiles_end[-1:].astype(I32))


def _wait_pieces(n_rows, src, dst, sem):
    def body(j, carry):
        pltpu.make_async_copy(src, dst, sem).wait()
        return carry
    lax.fori_loop(0, n_rows // SEG_ALIGN, body, 0)


def _wait_rows(n_rows, max_rows, src_rows, dst_rows, sem):
    size = SEG_ALIGN
    while size <= max_rows:
        @pl.when((n_rows & size) != 0)
        def _(size=size):
            pltpu.make_async_copy(src_rows(size), dst_rows(size), sem).wait()
        size *= 2


def _dispatch_kernel(dst_ref, tot_ref, tstart_ref, tlen_ref,
                     n2_ref, pos_ref, xs_hbm, xbuf, zbuf, sems, zsem, *, n_exp, mp):
    b = pl.program_id(0)
    nb = pl.num_programs(0)
    slot = b % 2
    tb = n2_ref.shape[0]
    pieces = mp // SEG_ALIGN
    vmem_rows = lambda n: xbuf.at[0, pl.ds(0, n), :]
    hbm_rows = lambda n: xs_hbm.at[pl.ds(0, n), :]

    @pl.when(b >= 2)
    def _():
        _wait_rows(tot_ref[jnp.maximum(b - 2, 0)], mp, vmem_rows, hbm_rows, sems.at[slot])

    pos = pos_ref[...]
    jio = lax.broadcasted_iota(I32, (mp, tb), 0)
    hit = jio == pos[0:1, :]
    for k in range(1, TOP_K):
        hit = hit | (jio == pos[k:k + 1, :])
    xbuf[slot] = _dot(jnp.where(hit, 1.0, 0.0).astype(BF16), n2_ref[...])

    def issue(j, carry):
        src = xbuf.at[slot, pl.ds(pl.multiple_of(j * SEG_ALIGN, SEG_ALIGN), SEG_ALIGN), :]
        dst = xs_hbm.at[pl.ds(pl.multiple_of(dst_ref[b * pieces + j], SEG_ALIGN), SEG_ALIGN), :]
        pltpu.make_async_copy(src, dst, sems.at[slot]).start()
        return carry
    lax.fori_loop(0, tot_ref[b] // SEG_ALIGN, issue, 0)

    @pl.when(b == nb - 1)
    def _():
        zbuf[...] = jnp.zeros_like(zbuf)
        for e in range(n_exp):
            ts = tstart_ref[e]

            def issue_zero(j, carry, ts=ts):
                dst = xs_hbm.at[pl.ds(pl.multiple_of(ts + j * SEG_ALIGN, SEG_ALIGN), SEG_ALIGN), :]
                pltpu.make_async_copy(zbuf, dst, zsem).start()
                return carry
            lax.fori_loop(0, tlen_ref[e] // SEG_ALIGN, issue_zero, 0)

        @pl.when(nb >= 2)
        def _():
            _wait_rows(tot_ref[jnp.maximum(nb - 2, 0)], mp, vmem_rows, hbm_rows, sems.at[1 - slot])
        _wait_rows(tot_ref[nb - 1], mp, vmem_rows, hbm_rows, sems.at[slot])
        for e in range(n_exp):
            _wait_pieces(tlen_ref[e], zbuf, hbm_rows(SEG_ALIGN), zsem)


def _dispatch(n2, pos, meta, n_exp, max_rows):
    n_total, d = n2.shape
    tb = TOKEN_BLOCK
    nb = n_total // tb
    mp = _local_rows(n_exp)
    tok = pl.BlockSpec((SUBLANES, tb), lambda b, *_: (0, b))
    return pl.pallas_call(
        functools.partial(_dispatch_kernel, n_exp=n_exp, mp=mp),
        grid_spec=pltpu.PrefetchScalarGridSpec(
            num_scalar_prefetch=4,
            grid=(nb,),
            in_specs=[pl.BlockSpec((tb, d), lambda b, *_: (b, 0)), tok],
            out_specs=pl.BlockSpec(memory_space=pl.ANY),
            scratch_shapes=[
                pltpu.VMEM((2, mp, d), F32),
                pltpu.VMEM((SEG_ALIGN, d), F32),
                pltpu.SemaphoreType.DMA((2,)),
                pltpu.SemaphoreType.DMA,
            ],
        ),
        out_shape=jax.ShapeDtypeStruct((max_rows, d), F32),
        compiler_params=pltpu.CompilerParams(dimension_semantics=("arbitrary",), vmem_limit_bytes=VMEM_LIMIT),
        name="moe_dispatch",
    )(meta["piece_dst"], meta["tot"], meta["tail_start"], meta["tail_len"], n2, pos)


def _expert_kernel(te_ref, nt_ref, x_ref, w1_ref, b1_ref, w2_ref, b2_ref, y_ref, w1b, w2b):
    i = pl.program_id(0)
    dff = w2_ref.shape[2]

    @pl.when(i < nt_ref[0])
    def _():
        changed = (i == 0) | (te_ref[i] != te_ref[jnp.maximum(i - 1, 0)])

        @pl.when(changed)
        def _():
            w1b[...] = w1_ref[0, 0].astype(BF16)
            w2b[...] = w2_ref[0, 0].astype(BF16)

        hdn = _dot(x_ref[...].astype(BF16), w1b[...]) + b1_ref[0, 0]
        gate = jnp.minimum(hdn[:, 0:dff], SWIGLU_LIMIT)
        up = jnp.clip(hdn[:, dff:2 * dff], -SWIGLU_LIMIT, SWIGLU_LIMIT)
        act = (up + 1.0) * gate * _sigmoid(SWIGLU_ALPHA * gate)
        y_ref[...] = _dot(act.astype(BF16), w2b[...]) + b2_ref[0, 0]


def _experts(xs, meta, layer, w1, b1, w2, b2):
    max_rows, d = xs.shape
    _, n_exp, _, dff2 = w1.shape
    dff = dff2 // 2
    tm = EXPERT_TILE
    max_tiles = max_rows // tm
    last = lambda i, nt: jnp.minimum(i, jnp.maximum(nt[0] - 1, 0))
    row_map = lambda i, te, nt: (last(i, nt), 0)
    exp_map = lambda i, te, nt: (layer, te[last(i, nt)], 0, 0)
    return pl.pallas_call(
        _expert_kernel,
        grid_spec=pltpu.PrefetchScalarGridSpec(
            num_scalar_prefetch=2,
            grid=(max_tiles,),
            in_specs=[
                pl.BlockSpec((tm, d), row_map),
                pl.BlockSpec((1, 1, d, dff2), exp_map),
                pl.BlockSpec((1, 1, 1, dff2), exp_map),
                pl.BlockSpec((1, 1, dff, d), exp_map),
                pl.BlockSpec((1, 1, 1, d), exp_map),
            ],
            out_specs=pl.BlockSpec((tm, d), row_map),
            scratch_shapes=[pltpu.VMEM((d, dff2), BF16), pltpu.VMEM((dff, d), BF16)],
        ),
        out_shape=jax.ShapeDtypeStruct((max_rows, d), F32),
        compiler_params=pltpu.CompilerParams(dimension_semantics=("arbitrary",), vmem_limit_bytes=VMEM_LIMIT),
        name="moe_experts",
    )(meta["tile_expert"], meta["n_tiles"], xs, w1, b1, w2, b2)


def _combine_kernel(src_ref, tot_ref,
                    h2_ref, gw_ref, pos_ref, p_ref, wpg_ref, bpg_ref, wpp_ref, gple_ref, gfin_ref,
                    ys_hbm, *rest, final, mp):
    if final:
        hn_ref, out_ref, ybuf, sems = rest
    else:
        hn_ref, ybuf, sems = rest
    b = pl.program_id(0)
    nb = pl.num_programs(0)
    slot = b % 2
    tb = h2_ref.shape[0]
    pieces = mp // SEG_ALIGN

    def fetch(blk, sl):
        def issue(j, carry):
            src = ys_hbm.at[pl.ds(pl.multiple_of(src_ref[blk * pieces + j], SEG_ALIGN), SEG_ALIGN), :]
            dst = ybuf.at[sl, pl.ds(pl.multiple_of(j * SEG_ALIGN, SEG_ALIGN), SEG_ALIGN), :]
            pltpu.make_async_copy(src, dst, sems.at[sl]).start()
            return carry
        lax.fori_loop(0, tot_ref[blk] // SEG_ALIGN, issue, 0)

    @pl.when(b == 0)
    def _():
        ybuf[...] = jnp.zeros_like(ybuf)
        fetch(0, 0)

    @pl.when(b + 1 < nb)
    def _():
        fetch(jnp.minimum(b + 1, nb - 1), 1 - slot)

    _wait_rows(tot_ref[b], mp, lambda n: ys_hbm.at[pl.ds(0, n), :], lambda n: ybuf.at[0, pl.ds(0, n), :],
               sems.at[slot])

    pos = pos_ref[...]
    gw = gw_ref[...]
    jio = lax.broadcasted_iota(I32, (mp, tb), 0)
    cg = jnp.where(jio == pos[0:1, :], gw[0:1, :], 0.0)
    for k in range(1, TOP_K):
        cg = cg + jnp.where(jio == pos[k:k + 1, :], gw[k:k + 1, :], 0.0)
    moe = lax.dot_general(cg.astype(BF16), ybuf[slot].astype(BF16), (((0,), (0,)), ((), ())),
                          preferred_element_type=F32)
    h3 = h2_ref[...] + moe
    n3 = _rms(h3, gple_ref[...]).astype(BF16)
    gate = _sigmoid(_dot(n3, wpg_ref[...]) + bpg_ref[...])
    h4 = h3 + gate * _dot(p_ref[...], wpp_ref[...])
    hn_ref[...] = h4
    if final:
        out_ref[...] = _rms(h4, gfin_ref[...])


def _combine(h2, gw, pos, p, ys, meta, w, g_final, n_exp, final):
    n_total, d = h2.shape
    ple = p.shape[1]
    tb = TOKEN_BLOCK
    nb = n_total // tb
    mp = _local_rows(n_exp)
    const2 = lambda b, *_: (0, 0)
    tok = pl.BlockSpec((SUBLANES, tb), lambda b, *_: (0, b))
    row = pl.BlockSpec((tb, d), lambda b, *_: (b, 0))
    n_out = 2 if final else 1
    return pl.pallas_call(
        functools.partial(_combine_kernel, final=final, mp=mp),
        grid_spec=pltpu.PrefetchScalarGridSpec(
            num_scalar_prefetch=2,
            grid=(nb,),
            in_specs=[
                row, tok, tok,
                pl.BlockSpec((tb, ple), lambda b, *_: (b, 0)),
                pl.BlockSpec((d, d), const2),
                pl.BlockSpec((1, d), const2),
                pl.BlockSpec((ple, d), const2),
                pl.BlockSpec((1, d), const2),
                pl.BlockSpec((1, d), const2),
                pl.BlockSpec(memory_space=pl.ANY),
            ],
            out_specs=[row] * n_out,
            scratch_shapes=[pltpu.VMEM((2, mp, d), F32), pltpu.SemaphoreType.DMA((2,))],
        ),
        out_shape=[jax.ShapeDtypeStruct((n_total, d), F32)] * n_out,
        compiler_params=pltpu.CompilerParams(dimension_semantics=("arbitrary",), vmem_limit_bytes=VMEM_LIMIT),
        name="moe_combine_ple",
    )(meta["piece_dst"], meta["tot"],
      h2, gw, pos, p, w["w_pg"], w["b_pg"], w["w_pp"], w["g_ple"], g_final, ys)


def _layer_weights(l, batch, g_mix, w_in, g_v, w_s, b_s, conv_w, conv_b, w_a, b_a, w_x, b_x, lam,
                   g_oa, g_ob, w_out, g_ffn, w_r, b_r, g_ple, w_pg, b_pg, w_pp):
    d = w_in.shape[1]
    dh = d // 2
    gd = dh // GMLP_GROUPS
    n_exp = w_r.shape[2]
    row = lambda a: a[l].reshape(1, -1).astype(F32)
    wt = jnp.where(jnp.tril(jnp.ones((CHUNK, CHUNK), bool)), w_s[l], 0.0).astype(BF16)
    hr = HALF * batch
    row_t = jnp.arange(hr, dtype=I32) // batch
    row_b = jnp.arange(hr, dtype=I32) % batch
    expand_t = (row_t[:, None] == jnp.arange(HALF, dtype=I32)[None, :]).astype(BF16)
    same_b = row_b[:, None] == row_b[None, :]

    def kron_block(blk):
        rows = jnp.einsum("it,gts->gis", expand_t, blk, preferred_element_type=F32).astype(BF16)
        full = jnp.einsum("gis,js->gij", rows, expand_t, preferred_element_type=F32)
        return jnp.where(same_b, full, 0.0).astype(BF16)

    kd = jnp.stack([kron_block(wt[:, :HALF, :HALF]), kron_block(wt[:, HALF:, HALF:])])
    k10 = kron_block(wt[:, HALF:, :HALF])
    bs_rows = jnp.repeat(jnp.repeat(b_s[l].T.astype(F32), gd, axis=1), batch, axis=0).reshape(2, hr, dh)
    hd = dh // LRU_HEADS
    head_of = jnp.arange(dh, dtype=I32) // hd
    same_head = head_of[:, None] == head_of[None, :]
    block_diag = lambda w: jnp.where(same_head, jnp.tile(w.astype(F32).reshape(dh, hd), (1, LRU_HEADS)), 0.0)
    wa_f32, wx_f32 = block_diag(w_a[l]), block_diag(w_x[l])
    w_r_pad = jnp.zeros((d, LANES), F32).at[:, :n_exp].set(w_r[l])
    return dict(
        g_mix=row(g_mix), w_in=w_in[l].astype(BF16), w_in_f32=w_in[l].astype(F32), g_v=row(g_v),
        kd=kd, k10=k10, bs_rows=bs_rows,
        w0_row=jnp.repeat(w_s[l][:, 0, 0], gd).reshape(1, dh).astype(F32),
        b0_row=jnp.repeat(b_s[l][:, 0], gd).reshape(1, dh).astype(F32),
        conv_w=conv_w[l].astype(F32), conv_b=row(conv_b),
        wa=wa_f32.astype(BF16), wa_f32=wa_f32, b_a=row(b_a), wx=wx_f32.astype(BF16), wx_f32=wx_f32,
        b_x=row(b_x), lam=row(lam),
        g_oa=row(g_oa), g_ob=row(g_ob), w_out=w_out[l].astype(BF16), w_out_f32=w_out[l].astype(F32),
        g_ffn=row(g_ffn),
        w_r=w_r_pad, b_r=b_r[l].reshape(n_exp, 1).astype(F32),
        g_ple=row(g_ple), w_pg=w_pg[l].astype(BF16), b_pg=row(b_pg), w_pp=w_pp[l].astype(BF16))


def kernel(x_prompt, x_sample, state_conv, state_lru, p_prompt, p_sample, g_mix, w_in, g_v, w_s, b_s, conv_w, conv_b, w_a, b_a, w_x, b_x, lam, g_oa, g_ob, w_out, g_ffn, w_r, b_r, w1, b1, w2, b2, g_ple, w_pg, b_pg, w_pp, g_final):
    batch, seq, d = x_prompt.shape
    dec_batch, dec_seq, _ = x_sample.shape
    depth = w_in.shape[0]
    n_exp = w_r.shape[2]
    dh = d // 2
    ple = p_prompt.shape[-1]
    assert dec_seq == 1 and seq % CHUNK == 0 and batch % SUBLANES == 0
    assert dh == GMLP_GROUPS * LANES and dh % (LRU_HEADS * (LANES // 2)) == 0
    n_prompt = seq * batch
    assert n_prompt % TOKEN_BLOCK == 0
    n_valid = n_prompt + dec_batch
    n_total = _round_up(n_valid, TOKEN_BLOCK)
    s_rows = n_total - n_prompt
    assert s_rows == TOKEN_BLOCK
    nb = n_total // TOKEN_BLOCK
    max_rows = _max_sorted_rows(n_valid, nb, n_exp)
    pad = n_total - n_valid

    h = jnp.concatenate([x_prompt.transpose(1, 0, 2).reshape(n_prompt, d), x_sample.reshape(dec_batch, d),
                         jnp.zeros((pad, d), F32)], axis=0)
    p_all = jnp.concatenate([p_prompt.transpose(0, 2, 1, 3).reshape(depth, n_prompt, ple),
                             p_sample.reshape(depth, dec_batch, ple),
                             jnp.zeros((depth, pad, ple), F32)], axis=1).astype(BF16)
    sconv = jnp.pad(state_conv.transpose(0, 2, 1, 3), ((0, 0), (0, 0), (0, s_rows - dec_batch), (0, 0)))
    slru = jnp.pad(state_lru, ((0, 0), (0, s_rows - dec_batch), (0, 0)))
    g_fin = g_final.reshape(1, d).astype(F32)
    b1_rows = b1.reshape(depth, n_exp, 1, -1)
    b2_rows = b2.reshape(depth, n_exp, 1, -1)

    v_p, conv_p, lru_p, v_s, conv_s, lru_s = [], [], [], [], [], []
    out = None
    for l in range(depth):
        w = _layer_weights(l, batch, g_mix, w_in, g_v, w_s, b_s, conv_w, conv_b, w_a, b_a, w_x, b_x, lam,
                           g_oa, g_ob, w_out, g_ffn, w_r, b_r, g_ple, w_pg, b_pg, w_pp)
        y, vp, cp, lp = _mix_prompt(h, n_prompt, batch, w)
        post_out = _post(y, h, w, n_valid, n_exp)
        h2, n2, gw, pos, cnt, vs, cs, ls = _sample_front(
            h, post_out, nb - 1, w, sconv[l], slru[l], n_valid, n_exp)
        meta = _route_meta(cnt[:, 0, :].astype(I32), max_rows // EXPERT_TILE)
        xs = _dispatch(n2, pos, meta, n_exp, max_rows)
        ys = _experts(xs, meta, l, w1, b1_rows, w2, b2_rows)
        final = l == depth - 1
        res = _combine(h2, gw, pos, p_all[l], ys, meta, w, g_fin, n_exp, final)
        h = res[0]
        if final:
            out = res[1]
        v_p.append(vp.reshape(CHUNK, batch, dh).transpose(1, 0, 2))
        conv_p.append(cp.reshape(CONV_W - 1, batch, dh).transpose(1, 0, 2))
        lru_p.append(lp)
        v_s.append(vs[:dec_batch].reshape(dec_batch, 1, dh))
        conv_s.append(cs[:, :dec_batch].transpose(1, 0, 2))
        lru_s.append(ls[:dec_batch])

    y_prompt = out[:n_prompt].reshape(seq, batch, d).transpose(1, 0, 2)
    y_sample = out[n_prompt:n_valid].reshape(dec_batch, 1, d)
    return (y_prompt, y_sample, jnp.stack(v_p), jnp.stack(conv_p), jnp.stack(lru_p),
            jnp.stack(v_s), jnp.stack(conv_s), jnp.stack(lru_s))
```

```python
import functools

import jax
import jax.numpy as jnp
from jax import lax
from jax.experimental import pallas as pl
from jax.experimental.pallas import tpu as pltpu

F32 = jnp.float32
BF16 = jnp.bfloat16
I32 = jnp.int32

CHUNK = 128
HALF = CHUNK // 2
GMLP_GROUPS = 4
LRU_HEADS = 8
CONV_W = 4
LRU_C = 8.0
TOP_K = 4
SWIGLU_LIMIT = 7.0
SWIGLU_ALPHA = 1.702
EPS = 1e-6

LANES = 128
SUBLANES = 8
TOKEN_BLOCK = 256
EXPERT_TILE = 512
SEG_ALIGN = SUBLANES
VMEM_LIMIT = 56 * 1024 * 1024


def _round_up(x, m):
    return (x + m - 1) // m * m


def _dot(a, b):
    return jnp.dot(a, b, preferred_element_type=F32)


def _gelu(x):
    return 0.5 * x * (1.0 + jnp.tanh(0.7978845608028654 * (x + 0.044715 * (x * x * x))))


def _sigmoid(x):
    return 1.0 / (1.0 + jnp.exp(-x))


def _softplus(x):
    return jnp.maximum(x, 0.0) + jnp.log1p(jnp.exp(-jnp.abs(x)))


def _rms(x, g):
    ms = jnp.mean(x * x, axis=-1, keepdims=True)
    return x * lax.rsqrt(ms + EPS) * g


def _group_norm_128(x, g):
    outs = []
    for j in range(x.shape[1] // LANES):
        blk = x[:, j * LANES:(j + 1) * LANES]
        ms = jnp.mean(blk * blk, axis=-1, keepdims=True)
        outs.append(blk * lax.rsqrt(ms + EPS))
    return jnp.concatenate(outs, axis=1) * g


def _group_norm_64(x, g):
    half = LANES // 2
    lo_mask = lax.broadcasted_iota(I32, (1, LANES), 1) < half
    outs = []
    for j in range(x.shape[1] // LANES):
        blk = x[:, j * LANES:(j + 1) * LANES]
        sq = blk * blk
        lo = jnp.sum(jnp.where(lo_mask, sq, 0.0), axis=-1, keepdims=True)
        hi = jnp.sum(jnp.where(lo_mask, 0.0, sq), axis=-1, keepdims=True)
        ms = jnp.where(lo_mask, lo, hi) * (1.0 / half)
        outs.append(blk * lax.rsqrt(ms + EPS))
    return jnp.concatenate(outs, axis=1) * g


def _time_major_rows(src_ref, slabs_ref, batch):
    steps = src_ref.shape[1]
    n_slab = slabs_ref.shape[0]
    for b in range(batch):
        blk = src_ref[b]
        for j in range(n_slab):
            slabs_ref[j, pl.ds(b, steps, stride=batch), :] = blk[:, j * LANES:(j + 1) * LANES]
    return jnp.concatenate([slabs_ref[j] for j in range(n_slab)], axis=1)


def _batch_major_store(rows, slabs_ref, dst_ref, batch):
    steps = dst_ref.shape[1]
    n_slab = slabs_ref.shape[0]
    for j in range(n_slab):
        slabs_ref[j] = rows[:, j * LANES:(j + 1) * LANES]
    for b in range(batch):
        dst_ref[b] = jnp.concatenate(
            [slabs_ref[j, pl.ds(b, steps, stride=batch), :] for j in range(n_slab)], axis=1)


def _split_bf16(x):
    hi = x.astype(BF16)
    lo = (x - hi.astype(F32)).astype(BF16)
    return hi, lo


def _dot_bf16(x, w):
    return _dot(x.astype(BF16), w)


def _dot_3pass(x, w):
    x_hi, x_lo = _split_bf16(x)
    w_hi, w_lo = _split_bf16(w)
    return _dot(x_hi, w_hi) + (_dot(x_lo, w_hi) + _dot(x_hi, w_lo))


def _lru_gates(xc, wa_ref, ba_ref, wx_ref, bx_ref, lam_ref, mm):
    r = _sigmoid(mm(xc, wa_ref[...]) + ba_ref[...])
    i = _sigmoid(mm(xc, wx_ref[...]) + bx_ref[...])
    log_a = (-LRU_C * r) * _softplus(-lam_ref[...])
    a = jnp.exp(log_a)
    mult = jnp.sqrt(-jnp.tanh(log_a) * (a * a + 1.0))
    return a, mult, i


def _mix_prompt_kernel(h_ref, gmix_ref, win_ref, gv_ref, kd_ref, k10_ref, bs_ref, cw_ref, cb_ref,
                       wa_ref, ba_ref, wx_ref, bx_ref, lam_ref, goa_ref, gob_ref,
                       y_ref, v_ref, conv_ref, lru_ref,
                       hstate, xpad, vprev, a_s, b_s, hs_s, hrows, *, batch, batch_major):
    c = pl.program_id(0)
    n_steps = pl.num_programs(0)
    rows = y_ref.shape[0]
    dh = gv_ref.shape[1]
    tail = (CONV_W - 1) * batch
    par = c % 2

    @pl.when(c == 0)
    def _():
        hstate[...] = jnp.zeros_like(hstate)
        xpad[0:tail, :] = jnp.zeros((tail, dh), F32)

    @pl.when(par == 0)
    def _():
        vprev[...] = jnp.zeros_like(vprev)

    n = _rms(_time_major_rows(h_ref, hrows, batch) if batch_major else h_ref[...], gmix_ref[...]).astype(BF16)

    v = _gelu(_dot(n, win_ref[:, dh:2 * dh]))
    vn = _group_norm_128(v, gv_ref[...])
    vb = vn.astype(BF16)
    s_parts = []
    for g in range(GMLP_GROUPS):
        sl = slice(g * LANES, (g + 1) * LANES)
        s_parts.append(_dot(kd_ref[par, g], vb[:, sl]) + _dot(k10_ref[g], vprev[:, sl]))
    s = jnp.concatenate(s_parts, axis=1) + bs_ref[par]
    vprev[...] = vb

    @pl.when(c >= n_steps - 2)
    def _():
        v_ref[...] = vn

    u = _gelu(_dot(n, win_ref[:, 0:dh]))
    y_ref[:, 0:dh] = _group_norm_128(u * s, goa_ref[...]).astype(BF16)

    xpad[tail:tail + rows, :] = _dot(n, win_ref[:, 2 * dh:3 * dh])
    xc = cb_ref[...] + cw_ref[0:1, :] * xpad[0:rows, :]
    for k in range(1, CONV_W):
        xc = xc + cw_ref[k:k + 1, :] * xpad[k * batch:k * batch + rows, :]
    new_tail = xpad[rows:rows + tail, :]
    xpad[0:tail, :] = new_tail

    a, mult, i = _lru_gates(xc, wa_ref, ba_ref, wx_ref, bx_ref, lam_ref, _dot_bf16)
    row = lax.broadcasted_iota(I32, (rows, 1), 0)
    mult = jnp.where((c == 0) & (row < batch), 1.0, mult)
    a_s[...] = a
    b_s[...] = mult * i * xc

    def step(t, h):
        r0 = pl.multiple_of(t * batch, batch)
        h = a_s[pl.ds(r0, batch), :] * h + b_s[pl.ds(r0, batch), :]
        hs_s[pl.ds(r0, batch), :] = h
        return h

    h_last = lax.fori_loop(0, rows // batch, step, hstate[...], unroll=8)
    hstate[...] = h_last

    gb = _gelu(_dot(n, win_ref[:, 3 * dh:4 * dh]))
    y_ref[:, dh:2 * dh] = _group_norm_64(hs_s[...] * gb, gob_ref[...]).astype(BF16)

    @pl.when(c == n_steps - 1)
    def _():
        conv_ref[...] = new_tail
        lru_ref[...] = h_last


def _mix_prompt(h, n_rows, batch, w):
    batch_major = h.ndim == 3
    d = h.shape[-1]
    dh = d // 2
    rows = HALF * batch
    n_steps = n_rows // rows
    tail = (CONV_W - 1) * batch
    const2 = lambda c: (0, 0)
    const3 = lambda c: (0, 0, 0)
    const4 = lambda c: (0, 0, 0, 0)
    h_spec = (pl.BlockSpec((batch, HALF, d), lambda c: (0, c, 0)) if batch_major
              else pl.BlockSpec((rows, d), lambda c: (c, 0)))
    return pl.pallas_call(
        functools.partial(_mix_prompt_kernel, batch=batch, batch_major=batch_major),
        grid=(n_steps,),
        in_specs=[
            h_spec,
            pl.BlockSpec((1, d), const2),
            pl.BlockSpec((d, 2 * d), const2),
            pl.BlockSpec((1, dh), const2),
            pl.BlockSpec((2, GMLP_GROUPS, rows, rows), const4),
            pl.BlockSpec((GMLP_GROUPS, rows, rows), const3),
            pl.BlockSpec((2, rows, dh), const3),
            pl.BlockSpec((CONV_W, dh), const2),
            pl.BlockSpec((1, dh), const2),
            pl.BlockSpec((dh, dh), const2),
            pl.BlockSpec((1, dh), const2),
            pl.BlockSpec((dh, dh), const2),
            pl.BlockSpec((1, dh), const2),
            pl.BlockSpec((1, dh), const2),
            pl.BlockSpec((1, dh), const2),
            pl.BlockSpec((1, dh), const2),
        ],
        out_specs=[
            pl.BlockSpec((rows, d), lambda c: (c, 0)),
            pl.BlockSpec((rows, dh), lambda c: (jnp.maximum(c - (n_steps - 2), 0), 0)),
            pl.BlockSpec((tail, dh), const2),
            pl.BlockSpec((batch, dh), const2),
        ],
        out_shape=[
            jax.ShapeDtypeStruct((n_rows, d), BF16),
            jax.ShapeDtypeStruct((2 * rows, dh), F32),
            jax.ShapeDtypeStruct((tail, dh), F32),
            jax.ShapeDtypeStruct((batch, dh), F32),
        ],
        scratch_shapes=[
            pltpu.VMEM((batch, dh), F32),
            pltpu.VMEM((tail + rows, dh), F32),
            pltpu.VMEM((rows, dh), BF16),
            pltpu.VMEM((rows, dh), F32),
            pltpu.VMEM((rows, dh), F32),
            pltpu.VMEM((rows, dh), F32),
            pltpu.VMEM((d // LANES, rows, LANES), F32),
        ],
        compiler_params=pltpu.CompilerParams(dimension_semantics=("arbitrary",), vmem_limit_bytes=VMEM_LIMIT),
        name="mix_prompt",
    )(h, w["g_mix"], w["w_in"], w["g_v"], w["kd"], w["k10"], w["bs_rows"], w["conv_w"], w["conv_b"],
      w["wa"], w["b_a"], w["wx"], w["b_x"], w["lam"], w["g_oa"], w["g_ob"])


def _sample_front_kernel(h_ref, h2_any, n2_any, gw_any, pos_any, cnt_any,
                         gmix_ref, win_ref, gv_ref, w0_ref, b0_ref, cw_ref, cb_ref,
                         wa_ref, ba_ref, wx_ref, bx_ref, lam_ref, goa_ref, gob_ref, sconv_ref, slru_ref,
                         wout_ref, gffn_ref, wr_ref, br_ref,
                         h2_ref, n2_ref, gw_ref, pos_ref, cnt_ref, v_ref, conv_ref, lru_ref,
                         *, block, n_valid, n_exp):
    del h2_any, n2_any, gw_any, pos_any, cnt_any
    dh = gv_ref.shape[1]
    h = h_ref[...]
    n = _rms(h, gmix_ref[...])
    proj = lambda lo: _dot_3pass(n, win_ref[:, lo:lo + dh])

    vn = _group_norm_128(_gelu(proj(dh)), gv_ref[...])
    v_ref[...] = vn
    s = vn * w0_ref[...] + b0_ref[...]
    ya = _group_norm_128(_gelu(proj(0)) * s, goa_ref[...])

    xb = proj(2 * dh)
    xc = cb_ref[...] + cw_ref[CONV_W - 1:CONV_W, :] * xb
    for k in range(CONV_W - 1):
        xc = xc + cw_ref[k:k + 1, :] * sconv_ref[k]
    for k in range(CONV_W - 2):
        conv_ref[k] = sconv_ref[k + 1]
    conv_ref[CONV_W - 2] = xb

    a, mult, i = _lru_gates(xc, wa_ref, ba_ref, wx_ref, bx_ref, lam_ref, _dot_3pass)
    h_new = a * slru_ref[...] + mult * i * xc
    lru_ref[...] = h_new
    yb = _group_norm_64(h_new * _gelu(proj(3 * dh)), gob_ref[...])

    h2 = h + _dot_3pass(ya, wout_ref[0:dh, :]) + _dot_3pass(yb, wout_ref[dh:2 * dh, :])
    _route_block(h2, gffn_ref, wr_ref, br_ref, h2_ref, n2_ref, gw_ref, pos_ref, cnt_ref,
                 block=block, n_valid=n_valid, n_exp=n_exp)


def _sample_front(h, h_block, post_out, block, w, sconv, slru, n_valid, n_exp):
    d = h.shape[1]
    dh = d // 2
    rows = TOKEN_BLOCK
    const2 = lambda i: (0, 0)
    const3 = lambda i: (0, 0, 0)
    vec = pl.BlockSpec((1, dh), const2)
    anyspace = pl.BlockSpec(memory_space=pl.ANY)
    blk_row = pl.BlockSpec((rows, d), lambda i: (block, 0))
    blk_tok = pl.BlockSpec((SUBLANES, rows), lambda i: (0, block))
    return pl.pallas_call(
        functools.partial(_sample_front_kernel, block=block, n_valid=n_valid, n_exp=n_exp),
        grid=(1,),
        in_specs=[
            pl.BlockSpec((rows, d), lambda i: (h_block, 0)),
            anyspace, anyspace, anyspace, anyspace, anyspace,
            pl.BlockSpec((1, d), const2),
            pl.BlockSpec((d, 2 * d), const2),
            vec, vec, vec,
            pl.BlockSpec((CONV_W, dh), const2),
            vec,
            pl.BlockSpec((dh, dh), const2),
            vec,
            pl.BlockSpec((dh, dh), const2),
            vec, vec, vec, vec,
            pl.BlockSpec((CONV_W - 1, rows, dh), const3),
            pl.BlockSpec((rows, dh), const2),
            pl.BlockSpec((d, d), const2),
            pl.BlockSpec((1, d), const2),
            pl.BlockSpec((d, LANES), const2),
            pl.BlockSpec((n_exp, 1), const2),
        ],
        out_specs=[
            blk_row, blk_row, blk_tok, blk_tok,
            pl.BlockSpec((1, SUBLANES, n_exp), lambda i: (block, 0, 0)),
            pl.BlockSpec((rows, dh), const2),
            pl.BlockSpec((CONV_W - 1, rows, dh), const3),
            pl.BlockSpec((rows, dh), const2),
        ],
        out_shape=[jax.ShapeDtypeStruct(a.shape, a.dtype) for a in post_out] + [
            jax.ShapeDtypeStruct((rows, dh), F32),
            jax.ShapeDtypeStruct((CONV_W - 1, rows, dh), F32),
            jax.ShapeDtypeStruct((rows, dh), F32),
        ],
        input_output_aliases={1 + k: k for k in range(len(post_out))},
        compiler_params=pltpu.CompilerParams(dimension_semantics=("arbitrary",), vmem_limit_bytes=VMEM_LIMIT),
        name="sample_front",
    )(h, *post_out, w["g_mix"], w["w_in_f32"], w["g_v"], w["w0_row"], w["b0_row"], w["conv_w"], w["conv_b"],
      w["wa_f32"], w["b_a"], w["wx_f32"], w["b_x"], w["lam"], w["g_oa"], w["g_ob"], sconv, slru,
      w["w_out_f32"], w["g_ffn"], w["w_r"], w["b_r"])


def _route_block(h2, gffn_ref, wr_ref, br_ref, h2_ref, n2_ref, gw_ref, pos_ref, cnt_ref,
                 *, block, n_valid, n_exp):
    tb = h2.shape[0]
    h2_ref[...] = h2
    n2 = _rms(h2, gffn_ref[...])
    n2_ref[...] = n2.astype(BF16)

    logits = _dot_3pass(n2, wr_ref[...])
    lt = logits.T[0:n_exp, :] + br_ref[...]

    eio = lax.broadcasted_iota(I32, (n_exp, tb), 0).astype(F32)
    col = block * tb + lax.broadcasted_iota(I32, (1, tb), 1)
    valid = col < n_valid
    vals, hots = [], []
    for _ in range(TOP_K):
        m = jnp.max(lt, axis=0, keepdims=True)
        sel = jnp.min(jnp.where(lt == m, eio, float(n_exp)), axis=0, keepdims=True)
        hot = eio == sel
        lt = jnp.where(hot, -jnp.inf, lt)
        vals.append(m)
        hots.append(hot)
    exps = [jnp.exp(v - vals[0]) for v in vals]
    den = exps[0]
    for e in exps[1:]:
        den = den + e

    occ = jnp.zeros((n_exp, tb), F32)
    for hot in hots:
        occ = occ + jnp.where(hot & valid, 1.0, 0.0)
    occ_b = occ.astype(BF16)
    before = (lax.broadcasted_iota(I32, (tb, tb), 0) < lax.broadcasted_iota(I32, (tb, tb), 1))
    ranks_e = _dot(occ_b, jnp.where(before, 1.0, 0.0).astype(BF16))
    cnt_ref[0] = lax.dot_general(jnp.ones((SUBLANES, tb), BF16), occ_b, (((1,), (1,)), ((), ())),
                                 preferred_element_type=F32)
    cnt_col = jnp.sum(occ, axis=1, keepdims=True)
    units = jnp.floor((cnt_col + (SEG_ALIGN - 1.0)) * (1.0 / SEG_ALIGN))
    earlier = (lax.broadcasted_iota(I32, (n_exp, n_exp), 1) < lax.broadcasted_iota(I32, (n_exp, n_exp), 0))
    seg_off = _dot(jnp.where(earlier, 1.0, 0.0).astype(BF16),
                   jnp.broadcast_to(units, (n_exp, LANES)).astype(BF16))[:, 0:1] * float(SEG_ALIGN)
    rows_e = ranks_e + seg_off

    rio = lax.broadcasted_iota(I32, (SUBLANES, tb), 0)
    gw_out = jnp.zeros((SUBLANES, tb), F32)
    pos_out = jnp.full((SUBLANES, tb), -1, I32)
    for k in range(TOP_K):
        pos_k = jnp.sum(jnp.where(hots[k], rows_e, 0.0), axis=0, keepdims=True).astype(I32)
        gw_out = jnp.where(rio == k, exps[k] / den, gw_out)
        pos_out = jnp.where(rio == k, jnp.where(valid, pos_k, -1), pos_out)
    gw_ref[...] = gw_out
    pos_ref[...] = pos_out


def _post_kernel(y_ref, h_ref, wout_ref, gffn_ref, wr_ref, br_ref,
                 h2_ref, n2_ref, gw_ref, pos_ref, cnt_ref, hrows, *, n_valid, n_exp, batch, batch_major):
    h = _time_major_rows(h_ref, hrows, batch) if batch_major else h_ref[...]
    h2 = h + _dot(y_ref[...], wout_ref[...])
    _route_block(h2, gffn_ref, wr_ref, br_ref, h2_ref, n2_ref, gw_ref, pos_ref, cnt_ref,
                 block=pl.program_id(0), n_valid=n_valid, n_exp=n_exp)


def _post(y, h, w, n_total, n_valid, n_exp, batch):
    batch_major = h.ndim == 3
    d = h.shape[-1]
    tb = TOKEN_BLOCK
    nb = n_total // tb
    const2 = lambda b: (0, 0)
    tok = pl.BlockSpec((SUBLANES, tb), lambda b: (0, b))
    h_spec = (pl.BlockSpec((batch, tb // batch, d), lambda b: (0, b, 0)) if batch_major
              else pl.BlockSpec((tb, d), lambda b: (b, 0)))
    return pl.pallas_call(
        functools.partial(_post_kernel, n_valid=n_valid, n_exp=n_exp, batch=batch, batch_major=batch_major),
        grid=(y.shape[0] // tb,),
        in_specs=[
            pl.BlockSpec((tb, d), lambda b: (b, 0)),
            h_spec,
            pl.BlockSpec((d, d), const2),
            pl.BlockSpec((1, d), const2),
            pl.BlockSpec((d, LANES), const2),
            pl.BlockSpec((n_exp, 1), const2),
        ],
        out_specs=[
            pl.BlockSpec((tb, d), lambda b: (b, 0)),
            pl.BlockSpec((tb, d), lambda b: (b, 0)),
            tok, tok,
            pl.BlockSpec((1, SUBLANES, n_exp), lambda b: (b, 0, 0)),
        ],
        out_shape=[
            jax.ShapeDtypeStruct((n_total, d), F32),
            jax.ShapeDtypeStruct((n_total, d), BF16),
            jax.ShapeDtypeStruct((SUBLANES, n_total), F32),
            jax.ShapeDtypeStruct((SUBLANES, n_total), I32),
            jax.ShapeDtypeStruct((nb, SUBLANES, n_exp), F32),
        ],
        scratch_shapes=[pltpu.VMEM((d // LANES, tb, LANES), F32)],
        compiler_params=pltpu.CompilerParams(dimension_semantics=("arbitrary",), vmem_limit_bytes=VMEM_LIMIT),
        name="post_router",
    )(y, h, w["w_out"], w["g_ffn"], w["w_r"], w["b_r"])


def _max_sorted_rows(n_valid, nb, n_exp):
    worst = TOP_K * n_valid + (SEG_ALIGN - 1) * nb * n_exp + (EXPERT_TILE - SEG_ALIGN) * n_exp
    return _round_up(worst, EXPERT_TILE)


def _local_rows(n_exp):
    return _round_up(TOP_K * TOKEN_BLOCK + (SEG_ALIGN - 1) * n_exp, LANES)


def _route_meta(cnt, max_tiles):
    nb, n_exp = cnt.shape
    p8 = (cnt + SEG_ALIGN - 1) // SEG_ALIGN * SEG_ALIGN
    off = jnp.cumsum(p8, axis=1) - p8
    tot = jnp.sum(p8, axis=1)
    seg = jnp.sum(p8, axis=0)
    reg = (seg + EXPERT_TILE - 1) // EXPERT_TILE * EXPERT_TILE
    reg_start = jnp.cumsum(reg) - reg
    start = reg_start[None, :] + jnp.cumsum(p8, axis=0) - p8
    piece_row = jnp.arange(_local_rows(n_exp) // SEG_ALIGN, dtype=I32) * SEG_ALIGN
    inside = (off[:, None, :] <= piece_row[None, :, None]) & (piece_row[None, :, None] < (off + p8)[:, None, :])
    piece_dst = piece_row[None, :] + jnp.sum(jnp.where(inside, (start - off)[:, None, :], 0), axis=2)
    tiles_end = jnp.cumsum(reg // EXPERT_TILE)
    tile_ids = jnp.arange(max_tiles, dtype=I32)
    tile_expert = jnp.minimum(jnp.sum((tiles_end[None, :] <= tile_ids[:, None]).astype(I32), axis=1), n_exp - 1)
    return dict(piece_dst=piece_dst.reshape(-1).astype(I32), tot=tot.astype(I32),
                tail_start=(reg_start + seg).astype(I32), tail_len=(reg - seg).astype(I32),
                tile_expert=tile_expert.astype(I32), n_tiles=tiles_end[-1:].astype(I32),
                tiles_end=tiles_end.astype(I32))


def _wait_pieces(n_rows, src, dst, sem):
    def body(j, carry):
        pltpu.make_async_copy(src, dst, sem).wait()
        return carry
    lax.fori_loop(0, n_rows // SEG_ALIGN, body, 0)


def _wait_rows(n_rows, max_rows, src_rows, dst_rows, sem):
    size = SEG_ALIGN
    while size <= max_rows:
        @pl.when((n_rows & size) != 0)
        def _(size=size):
            pltpu.make_async_copy(src_rows(size), dst_rows(size), sem).wait()
        size *= 2


def _dispatch_kernel(dst_ref, tot_ref, tstart_ref, tlen_ref,
                     n2_ref, pos_ref, xs_hbm, xbuf, zbuf, sems, zsem, *, n_exp, mp):
    b = pl.program_id(0)
    nb = pl.num_programs(0)
    slot = b % 2
    tb = n2_ref.shape[0]
    pieces = mp // SEG_ALIGN
    vmem_rows = lambda n: xbuf.at[0, pl.ds(0, n), :]
    hbm_rows = lambda n: xs_hbm.at[pl.ds(0, n), :]

    @pl.when(b >= 2)
    def _():
        _wait_rows(tot_ref[jnp.maximum(b - 2, 0)], mp, vmem_rows, hbm_rows, sems.at[slot])

    pos = pos_ref[...]
    jio = lax.broadcasted_iota(I32, (mp, tb), 0)
    hit = jio == pos[0:1, :]
    for k in range(1, TOP_K):
        hit = hit | (jio == pos[k:k + 1, :])
    xbuf[slot] = _dot(jnp.where(hit, 1.0, 0.0).astype(BF16), n2_ref[...])

    def issue(j, carry):
        src = xbuf.at[slot, pl.ds(pl.multiple_of(j * SEG_ALIGN, SEG_ALIGN), SEG_ALIGN), :]
        dst = xs_hbm.at[pl.ds(pl.multiple_of(dst_ref[b * pieces + j], SEG_ALIGN), SEG_ALIGN), :]
        pltpu.make_async_copy(src, dst, sems.at[slot]).start()
        return carry
    lax.fori_loop(0, tot_ref[b] // SEG_ALIGN, issue, 0)

    @pl.when(b == nb - 1)
    def _():
        zbuf[...] = jnp.zeros_like(zbuf)
        for e in range(n_exp):
            ts = tstart_ref[e]

            def issue_zero(j, carry, ts=ts):
                dst = xs_hbm.at[pl.ds(pl.multiple_of(ts + j * SEG_ALIGN, SEG_ALIGN), SEG_ALIGN), :]
                pltpu.make_async_copy(zbuf, dst, zsem).start()
                return carry
            lax.fori_loop(0, tlen_ref[e] // SEG_ALIGN, issue_zero, 0)

        @pl.when(nb >= 2)
        def _():
            _wait_rows(tot_ref[jnp.maximum(nb - 2, 0)], mp, vmem_rows, hbm_rows, sems.at[1 - slot])
        _wait_rows(tot_ref[nb - 1], mp, vmem_rows, hbm_rows, sems.at[slot])
        for e in range(n_exp):
            _wait_pieces(tlen_ref[e], zbuf, hbm_rows(SEG_ALIGN), zsem)


def _dispatch(n2, pos, meta, n_exp, max_rows):
    n_total, d = n2.shape
    tb = TOKEN_BLOCK
    nb = n_total // tb
    mp = _local_rows(n_exp)
    tok = pl.BlockSpec((SUBLANES, tb), lambda b, *_: (0, b))
    return pl.pallas_call(
        functools.partial(_dispatch_kernel, n_exp=n_exp, mp=mp),
        grid_spec=pltpu.PrefetchScalarGridSpec(
            num_scalar_prefetch=4,
            grid=(nb,),
            in_specs=[pl.BlockSpec((tb, d), lambda b, *_: (b, 0)), tok],
            out_specs=pl.BlockSpec(memory_space=pl.ANY),
            scratch_shapes=[
                pltpu.VMEM((2, mp, d), F32),
                pltpu.VMEM((SEG_ALIGN, d), F32),
                pltpu.SemaphoreType.DMA((2,)),
                pltpu.SemaphoreType.DMA,
            ],
        ),
        out_shape=jax.ShapeDtypeStruct((max_rows, d), F32),
        compiler_params=pltpu.CompilerParams(dimension_semantics=("arbitrary",), vmem_limit_bytes=VMEM_LIMIT),
        name="moe_dispatch",
    )(meta["piece_dst"], meta["tot"], meta["tail_start"], meta["tail_len"], n2, pos)


def _expert_kernel(te_ref, nt_ref, tend_ref, x_ref, w1_hbm, b1_ref, w2_hbm, b2_ref, y_ref,
                   w1s, w2s, w1b, w2b, sems, n_loaded, *, layer):
    i = pl.program_id(0)
    n_tiles = nt_ref[0]
    dff = w2b.shape[0]

    def weight_copies(expert, slot):
        return (pltpu.make_async_copy(w1_hbm.at[layer, expert], w1s.at[slot], sems.at[0, slot]),
                pltpu.make_async_copy(w2_hbm.at[layer, expert], w2s.at[slot], sems.at[1, slot]))

    @pl.when(i < n_tiles)
    def _():
        expert = te_ref[i]

        @pl.when(i == 0)
        def _():
            n_loaded[0] = 0
            for c in weight_copies(expert, 0):
                c.start()

        @pl.when((i == 0) | (expert != te_ref[jnp.maximum(i - 1, 0)]))
        def _():
            slot = n_loaded[0] % 2
            n_loaded[0] = n_loaded[0] + 1
            for c in weight_copies(expert, slot):
                c.wait()
            nxt = tend_ref[expert]

            @pl.when(nxt < n_tiles)
            def _():
                for c in weight_copies(te_ref[jnp.minimum(nxt, n_tiles - 1)], 1 - slot):
                    c.start()
            w1b[...] = w1s[slot].astype(BF16)
            w2b[...] = w2s[slot].astype(BF16)

        hdn = _dot(x_ref[...].astype(BF16), w1b[...]) + b1_ref[0, 0]
        gate = jnp.minimum(hdn[:, 0:dff], SWIGLU_LIMIT)
        up = jnp.clip(hdn[:, dff:2 * dff], -SWIGLU_LIMIT, SWIGLU_LIMIT)
        act = (up + 1.0) * gate * _sigmoid(SWIGLU_ALPHA * gate)
        y_ref[...] = _dot(act.astype(BF16), w2b[...]) + b2_ref[0, 0]


def _experts(xs, meta, layer, w1, b1, w2, b2):
    max_rows, d = xs.shape
    _, n_exp, _, dff2 = w1.shape
    dff = dff2 // 2
    tm = EXPERT_TILE
    max_tiles = max_rows // tm
    last = lambda i, nt: jnp.minimum(i, jnp.maximum(nt[0] - 1, 0))
    row_map = lambda i, te, nt, tend: (last(i, nt), 0)
    exp_map = lambda i, te, nt, tend: (layer, te[last(i, nt)], 0, 0)
    return pl.pallas_call(
        functools.partial(_expert_kernel, layer=layer),
        grid_spec=pltpu.PrefetchScalarGridSpec(
            num_scalar_prefetch=3,
            grid=(max_tiles,),
            in_specs=[
                pl.BlockSpec((tm, d), row_map),
                pl.BlockSpec(memory_space=pl.ANY),
                pl.BlockSpec((1, 1, 1, dff2), exp_map),
                pl.BlockSpec(memory_space=pl.ANY),
                pl.BlockSpec((1, 1, 1, d), exp_map),
            ],
            out_specs=pl.BlockSpec((tm, d), row_map),
            scratch_shapes=[
                pltpu.VMEM((2, d, dff2), F32), pltpu.VMEM((2, dff, d), F32),
                pltpu.VMEM((d, dff2), BF16), pltpu.VMEM((dff, d), BF16),
                pltpu.SemaphoreType.DMA((2, 2)),
                pltpu.SMEM((1,), I32),
            ],
        ),
        out_shape=jax.ShapeDtypeStruct((max_rows, d), F32),
        compiler_params=pltpu.CompilerParams(dimension_semantics=("arbitrary",), vmem_limit_bytes=VMEM_LIMIT),
        name="moe_experts",
    )(meta["tile_expert"], meta["n_tiles"], meta["tiles_end"], xs, w1, b1, w2, b2)


def _combine_kernel(src_ref, tot_ref,
                    h2_ref, gw_ref, pos_ref, pp_ref, ps_ref, wpg_ref, bpg_ref, wpp_ref, gple_ref, gfin_ref,
                    ys_hbm, *rest, final, mp, batch):
    if final:
        outp_ref, outs_ref, ybuf, sems, prows, orows = rest
    else:
        hn_ref, ybuf, sems, prows = rest
    b = pl.program_id(0)
    nb = pl.num_programs(0)
    slot = b % 2
    tb = h2_ref.shape[0]
    pieces = mp // SEG_ALIGN

    def fetch(blk, sl):
        def issue(j, carry):
            src = ys_hbm.at[pl.ds(pl.multiple_of(src_ref[blk * pieces + j], SEG_ALIGN), SEG_ALIGN), :]
            dst = ybuf.at[sl, pl.ds(pl.multiple_of(j * SEG_ALIGN, SEG_ALIGN), SEG_ALIGN), :]
            pltpu.make_async_copy(src, dst, sems.at[sl]).start()
            return carry
        lax.fori_loop(0, tot_ref[blk] // SEG_ALIGN, issue, 0)

    @pl.when(b == 0)
    def _():
        ybuf[...] = jnp.zeros_like(ybuf)
        fetch(0, 0)

    @pl.when(b + 1 < nb)
    def _():
        fetch(jnp.minimum(b + 1, nb - 1), 1 - slot)

    _wait_rows(tot_ref[b], mp, lambda n: ys_hbm.at[pl.ds(0, n), :], lambda n: ybuf.at[0, pl.ds(0, n), :],
               sems.at[slot])

    pos = pos_ref[...]
    gw = gw_ref[...]
    jio = lax.broadcasted_iota(I32, (mp, tb), 0)
    cg = jnp.where(jio == pos[0:1, :], gw[0:1, :], 0.0)
    for k in range(1, TOP_K):
        cg = cg + jnp.where(jio == pos[k:k + 1, :], gw[k:k + 1, :], 0.0)
    moe = lax.dot_general(cg.astype(BF16), ybuf[slot].astype(BF16), (((0,), (0,)), ((), ())),
                          preferred_element_type=F32)
    h3 = h2_ref[...] + moe
    n3 = _rms(h3, gple_ref[...]).astype(BF16)
    gate = _sigmoid(_dot(n3, wpg_ref[...]) + bpg_ref[...])
    p_rows = jnp.where(b == nb - 1, ps_ref[0], _time_major_rows(pp_ref.at[0], prows, batch))
    h4 = h3 + gate * _dot(p_rows.astype(BF16), wpp_ref[...])
    if final:
        out = _rms(h4, gfin_ref[...])

        @pl.when(b < nb - 1)
        def _():
            _batch_major_store(out, orows, outp_ref, batch)

        @pl.when(b == nb - 1)
        def _():
            outs_ref[...] = out
    else:
        hn_ref[...] = h4


def _combine(h2, gw, pos, p_prompt, p_sample, layer, ys, meta, w, g_final, n_exp, final):
    n_total, d = h2.shape
    _, batch, seq, ple = p_prompt.shape
    tb = TOKEN_BLOCK
    nb = n_total // tb
    steps = tb // batch
    mp = _local_rows(n_exp)
    const2 = lambda b, *_: (0, 0)
    tok = pl.BlockSpec((SUBLANES, tb), lambda b, *_: (0, b))
    row = pl.BlockSpec((tb, d), lambda b, *_: (b, 0))
    prompt_blk = lambda b: jnp.minimum(b, nb - 2)
    if final:
        out_specs = [pl.BlockSpec((batch, steps, d), lambda b, *_: (0, prompt_blk(b), 0)),
                     pl.BlockSpec((tb, d), const2)]
        out_shape = [jax.ShapeDtypeStruct((batch, seq, d), F32), jax.ShapeDtypeStruct((tb, d), F32)]
        extra_scratch = [pltpu.VMEM((d // LANES, tb, LANES), F32)]
    else:
        out_specs = [row]
        out_shape = [jax.ShapeDtypeStruct((n_total, d), F32)]
        extra_scratch = []
    return pl.pallas_call(
        functools.partial(_combine_kernel, final=final, mp=mp, batch=batch),
        grid_spec=pltpu.PrefetchScalarGridSpec(
            num_scalar_prefetch=2,
            grid=(nb,),
            in_specs=[
                row, tok, tok,
                pl.BlockSpec((1, batch, steps, ple), lambda b, *_: (layer, 0, prompt_blk(b), 0)),
                pl.BlockSpec((1, tb, ple), lambda b, *_: (layer, 0, 0)),
                pl.BlockSpec((d, d), const2),
                pl.BlockSpec((1, d), const2),
                pl.BlockSpec((ple, d), const2),
                pl.BlockSpec((1, d), const2),
                pl.BlockSpec((1, d), const2),
                pl.BlockSpec(memory_space=pl.ANY),
            ],
            out_specs=out_specs,
            scratch_shapes=[pltpu.VMEM((2, mp, d), F32), pltpu.SemaphoreType.DMA((2,)),
                            pltpu.VMEM((ple // LANES, tb, LANES), F32)] + extra_scratch,
        ),
        out_shape=out_shape,
        compiler_params=pltpu.CompilerParams(dimension_semantics=("arbitrary",), vmem_limit_bytes=VMEM_LIMIT),
        name="moe_combine_ple",
    )(meta["piece_dst"], meta["tot"],
      h2, gw, pos, p_prompt, p_sample, w["w_pg"], w["b_pg"], w["w_pp"], w["g_ple"], g_final, ys)


def _layer_weights(l, batch, g_mix, w_in, g_v, w_s, b_s, conv_w, conv_b, w_a, b_a, w_x, b_x, lam,
                   g_oa, g_ob, w_out, g_ffn, w_r, b_r, g_ple, w_pg, b_pg, w_pp):
    d = w_in.shape[1]
    dh = d // 2
    gd = dh // GMLP_GROUPS
    n_exp = w_r.shape[2]
    row = lambda a: a[l].reshape(1, -1).astype(F32)
    wt = jnp.where(jnp.tril(jnp.ones((CHUNK, CHUNK), bool)), w_s[l], 0.0).astype(BF16)
    hr = HALF * batch
    row_t = jnp.arange(hr, dtype=I32) // batch
    row_b = jnp.arange(hr, dtype=I32) % batch
    expand_t = (row_t[:, None] == jnp.arange(HALF, dtype=I32)[None, :]).astype(BF16)
    same_b = row_b[:, None] == row_b[None, :]

    def kron_block(blk):
        rows = jnp.einsum("it,gts->gis", expand_t, blk, preferred_element_type=F32).astype(BF16)
        full = jnp.einsum("gis,js->gij", rows, expand_t, preferred_element_type=F32)
        return jnp.where(same_b, full, 0.0).astype(BF16)

    kd = jnp.stack([kron_block(wt[:, :HALF, :HALF]), kron_block(wt[:, HALF:, HALF:])])
    k10 = kron_block(wt[:, HALF:, :HALF])
    bs_rows = jnp.repeat(jnp.repeat(b_s[l].T.astype(F32), gd, axis=1), batch, axis=0).reshape(2, hr, dh)
    hd = dh // LRU_HEADS
    head_of = jnp.arange(dh, dtype=I32) // hd
    same_head = head_of[:, None] == head_of[None, :]
    block_diag = lambda w: jnp.where(same_head, jnp.tile(w.astype(F32).reshape(dh, hd), (1, LRU_HEADS)), 0.0)
    wa_f32, wx_f32 = block_diag(w_a[l]), block_diag(w_x[l])
    w_r_pad = jnp.zeros((d, LANES), F32).at[:, :n_exp].set(w_r[l])
    return dict(
        g_mix=row(g_mix), w_in=w_in[l].astype(BF16), w_in_f32=w_in[l].astype(F32), g_v=row(g_v),
        kd=kd, k10=k10, bs_rows=bs_rows,
        w0_row=jnp.repeat(w_s[l][:, 0, 0], gd).reshape(1, dh).astype(F32),
        b0_row=jnp.repeat(b_s[l][:, 0], gd).reshape(1, dh).astype(F32),
        conv_w=conv_w[l].astype(F32), conv_b=row(conv_b),
        wa=wa_f32.astype(BF16), wa_f32=wa_f32, b_a=row(b_a), wx=wx_f32.astype(BF16), wx_f32=wx_f32,
        b_x=row(b_x), lam=row(lam),
        g_oa=row(g_oa), g_ob=row(g_ob), w_out=w_out[l].astype(BF16), w_out_f32=w_out[l].astype(F32),
        g_ffn=row(g_ffn),
        w_r=w_r_pad, b_r=b_r[l].reshape(n_exp, 1).astype(F32),
        g_ple=row(g_ple), w_pg=w_pg[l].astype(BF16), b_pg=row(b_pg), w_pp=w_pp[l].astype(BF16))


def kernel(x_prompt, x_sample, state_conv, state_lru, p_prompt, p_sample, g_mix, w_in, g_v, w_s, b_s, conv_w, conv_b, w_a, b_a, w_x, b_x, lam, g_oa, g_ob, w_out, g_ffn, w_r, b_r, w1, b1, w2, b2, g_ple, w_pg, b_pg, w_pp, g_final):
    batch, seq, d = x_prompt.shape
    dec_batch, dec_seq, _ = x_sample.shape
    depth = w_in.shape[0]
    n_exp = w_r.shape[2]
    dh = d // 2
    ple = p_prompt.shape[-1]
    assert dec_seq == 1 and seq % CHUNK == 0 and batch % SUBLANES == 0
    assert dh == GMLP_GROUPS * LANES and dh % (LRU_HEADS * (LANES // 2)) == 0
    n_prompt = seq * batch
    assert n_prompt % TOKEN_BLOCK == 0
    n_valid = n_prompt + dec_batch
    n_total = _round_up(n_valid, TOKEN_BLOCK)
    s_rows = n_total - n_prompt
    assert s_rows == TOKEN_BLOCK
    nb = n_total // TOKEN_BLOCK
    max_rows = _max_sorted_rows(n_valid, nb, n_exp)
    pad_rows = ((0, 0), (0, s_rows - dec_batch), (0, 0))

    h, h_sample, h_sample_block = x_prompt, jnp.pad(x_sample.reshape(1, dec_batch, d), pad_rows)[0], 0
    p_samp = jnp.pad(p_sample.reshape(depth, dec_batch, ple), pad_rows)
    sconv = jnp.pad(state_conv.transpose(0, 2, 1, 3), ((0, 0),) + pad_rows)
    slru = jnp.pad(state_lru, pad_rows)
    g_fin = g_final.reshape(1, d).astype(F32)
    b1_rows = b1.reshape(depth, n_exp, 1, -1)
    b2_rows = b2.reshape(depth, n_exp, 1, -1)

    v_p, conv_p, lru_p, v_s, conv_s, lru_s = [], [], [], [], [], []
    for l in range(depth):
        w = _layer_weights(l, batch, g_mix, w_in, g_v, w_s, b_s, conv_w, conv_b, w_a, b_a, w_x, b_x, lam,
                           g_oa, g_ob, w_out, g_ffn, w_r, b_r, g_ple, w_pg, b_pg, w_pp)
        y, vp, cp, lp = _mix_prompt(h, n_prompt, batch, w)
        post_out = _post(y, h, w, n_total, n_valid, n_exp, batch)
        h2, n2, gw, pos, cnt, vs, cs, ls = _sample_front(
            h_sample, h_sample_block, post_out, nb - 1, w, sconv[l], slru[l], n_valid, n_exp)
        meta = _route_meta(cnt[:, 0, :].astype(I32), max_rows // EXPERT_TILE)
        xs = _dispatch(n2, pos, meta, n_exp, max_rows)
        ys = _experts(xs, meta, l, w1, b1_rows, w2, b2_rows)
        final = l == depth - 1
        res = _combine(h2, gw, pos, p_prompt, p_samp, l, ys, meta, w, g_fin, n_exp, final)
        if final:
            y_prompt, out_sample = res
        else:
            h = h_sample = res[0]
            h_sample_block = nb - 1
        v_p.append(vp.reshape(CHUNK, batch, dh).transpose(1, 0, 2))
        conv_p.append(cp.reshape(CONV_W - 1, batch, dh).transpose(1, 0, 2))
        lru_p.append(lp)
        v_s.append(vs[:dec_batch].reshape(dec_batch, 1, dh))
        conv_s.append(cs[:, :dec_batch].transpose(1, 0, 2))
        lru_s.append(ls[:dec_batch])

    y_sample = out_sample[:dec_batch].reshape(dec_batch, 1, d)
    return (y_prompt, y_sample, jnp.stack(v_p), jnp.stack(conv_p), jnp.stack(lru_p),
            jnp.stack(v_s), jnp.stack(conv_s), jnp.stack(lru_s))
```

```python
import functools

import jax
import jax.numpy as jnp
from jax import lax
from jax.experimental import pallas as pl
from jax.experimental.pallas import tpu as pltpu

F32 = jnp.float32
BF16 = jnp.bfloat16
I32 = jnp.int32

CHUNK = 128
HALF = CHUNK // 2
GMLP_GROUPS = 4
LRU_HEADS = 8
CONV_W = 4
LRU_C = 8.0
TOP_K = 4
SWIGLU_LIMIT = 7.0
SWIGLU_ALPHA = 1.702
EPS = 1e-6

LANES = 128
SUBLANES = 8
TOKEN_BLOCK = 256
EXPERT_TILE = 512
EXPERT_SUBTILE = 128
FETCH_UNROLL = 4
SEG_ALIGN = SUBLANES
VMEM_LIMIT = 56 * 1024 * 1024


def _round_up(x, m):
    return (x + m - 1) // m * m


def _dot(a, b):
    return jnp.dot(a, b, preferred_element_type=F32)


def _gelu(x):
    return 0.5 * x * (1.0 + jnp.tanh(0.7978845608028654 * (x + 0.044715 * (x * x * x))))


def _sigmoid(x):
    return 1.0 / (1.0 + jnp.exp(-x))


def _softplus(x):
    return jnp.maximum(x, 0.0) + jnp.log1p(jnp.exp(-jnp.abs(x)))


def _rms(x, g):
    ms = jnp.mean(x * x, axis=-1, keepdims=True)
    return x * lax.rsqrt(ms + EPS) * g


def _group_norm_128(x, g):
    outs = []
    for j in range(x.shape[1] // LANES):
        blk = x[:, j * LANES:(j + 1) * LANES]
        ms = jnp.mean(blk * blk, axis=-1, keepdims=True)
        outs.append(blk * lax.rsqrt(ms + EPS))
    return jnp.concatenate(outs, axis=1) * g


def _group_norm_64(x, g):
    half = LANES // 2
    lo_mask = lax.broadcasted_iota(I32, (1, LANES), 1) < half
    outs = []
    for j in range(x.shape[1] // LANES):
        blk = x[:, j * LANES:(j + 1) * LANES]
        sq = blk * blk
        lo = jnp.sum(jnp.where(lo_mask, sq, 0.0), axis=-1, keepdims=True)
        hi = jnp.sum(jnp.where(lo_mask, 0.0, sq), axis=-1, keepdims=True)
        ms = jnp.where(lo_mask, lo, hi) * (1.0 / half)
        outs.append(blk * lax.rsqrt(ms + EPS))
    return jnp.concatenate(outs, axis=1) * g


def _time_major_rows(src_ref, slabs_ref, batch):
    steps = src_ref.shape[1]
    n_slab = slabs_ref.shape[0]
    for b in range(batch):
        blk = src_ref[b]
        for j in range(n_slab):
            slabs_ref[j, pl.ds(b, steps, stride=batch), :] = blk[:, j * LANES:(j + 1) * LANES]
    return jnp.concatenate([slabs_ref[j] for j in range(n_slab)], axis=1)


def _batch_major_store(rows, slabs_ref, dst_ref, batch):
    steps = dst_ref.shape[1]
    n_slab = slabs_ref.shape[0]
    for j in range(n_slab):
        slabs_ref[j] = rows[:, j * LANES:(j + 1) * LANES]
    for b in range(batch):
        dst_ref[b] = jnp.concatenate(
            [slabs_ref[j, pl.ds(b, steps, stride=batch), :] for j in range(n_slab)], axis=1)


def _split_bf16(x):
    hi = x.astype(BF16)
    lo = (x - hi.astype(F32)).astype(BF16)
    return hi, lo


def _dot_bf16(x, w):
    return _dot(x.astype(BF16), w)


def _dot_3pass(x, w):
    x_hi, x_lo = _split_bf16(x)
    w_hi, w_lo = _split_bf16(w)
    return _dot(x_hi, w_hi) + (_dot(x_lo, w_hi) + _dot(x_hi, w_lo))


def _lru_gates(xc, wa_ref, ba_ref, wx_ref, bx_ref, lam_ref, mm):
    r = _sigmoid(mm(xc, wa_ref[...]) + ba_ref[...])
    i = _sigmoid(mm(xc, wx_ref[...]) + bx_ref[...])
    log_a = (-LRU_C * r) * _softplus(-lam_ref[...])
    a = jnp.exp(log_a)
    mult = jnp.sqrt(-jnp.tanh(log_a) * (a * a + 1.0))
    return a, mult, i


def _mix_prompt_kernel(h_ref, gmix_ref, win_ref, gv_ref, kd_ref, k10_ref, bs_ref, cw_ref, cb_ref,
                       wa_ref, ba_ref, wx_ref, bx_ref, lam_ref, goa_ref, gob_ref,
                       y_ref, v_ref, conv_ref, lru_ref,
                       hstate, xpad, vprev, a_s, b_s, hs_s, hrows, z_even, z_odd, *, batch, batch_major):
    c = pl.program_id(0)
    rows = y_ref.shape[0]
    dh = gv_ref.shape[1]
    tail = (CONV_W - 1) * batch

    @pl.when(c == 0)
    def _():
        z_odd[...] = jnp.zeros_like(z_odd)
        vprev[...] = jnp.zeros_like(vprev)

    @pl.when(c <= 1)
    def _():
        hstate[...] = jnp.zeros_like(hstate)
        xpad[0:tail, :] = jnp.zeros((tail, dh), F32)

    def tile_step(z_in, z_out, par):
        h = _time_major_rows(h_ref, hrows, batch) if batch_major else h_ref[...]
        z_out[...] = _dot(_rms(h, gmix_ref[...]).astype(BF16), win_ref[...])

        if par == 0:
            vprev[...] = jnp.zeros_like(vprev)

        vn = _group_norm_128(_gelu(z_in[:, dh:2 * dh]), gv_ref[...])
        vb = vn.astype(BF16)
        s_parts = []
        for g in range(GMLP_GROUPS):
            sl = slice(g * LANES, (g + 1) * LANES)
            s_parts.append(_dot(kd_ref[par, g], vb[:, sl]) + _dot(k10_ref[g], vprev[:, sl]))
        s = jnp.concatenate(s_parts, axis=1) + bs_ref[par]
        vprev[...] = vb
        v_ref[...] = vn
        y_ref[:, 0:dh] = _group_norm_128(_gelu(z_in[:, 0:dh]) * s, goa_ref[...]).astype(BF16)

        xpad[tail:tail + rows, :] = z_in[:, 2 * dh:3 * dh]
        xc = cb_ref[...] + cw_ref[0:1, :] * xpad[0:rows, :]
        for k in range(1, CONV_W):
            xc = xc + cw_ref[k:k + 1, :] * xpad[k * batch:k * batch + rows, :]
        new_tail = xpad[rows:rows + tail, :]
        xpad[0:tail, :] = new_tail

        a, mult, i = _lru_gates(xc, wa_ref, ba_ref, wx_ref, bx_ref, lam_ref, _dot_bf16)
        row = lax.broadcasted_iota(I32, (rows, 1), 0)
        mult = jnp.where((c == 1) & (row < batch), 1.0, mult)
        a_s[...] = a
        b_s[...] = mult * i * xc

        def step(t, h):
            r0 = t * batch
            h = a_s[r0:r0 + batch, :] * h + b_s[r0:r0 + batch, :]
            hs_s[r0:r0 + batch, :] = h
            return h

        h_last = hstate[...]
        for t in range(rows // batch):
            h_last = step(t, h_last)
        hstate[...] = h_last

        y_ref[:, dh:2 * dh] = _group_norm_64(hs_s[...] * _gelu(z_in[:, 3 * dh:4 * dh]), gob_ref[...]).astype(BF16)
        conv_ref[...] = new_tail
        lru_ref[...] = h_last

    @pl.when(c % 2 == 0)
    def _():
        tile_step(z_odd, z_even, 1)

    @pl.when(c % 2 == 1)
    def _():
        tile_step(z_even, z_odd, 0)


def _mix_prompt(h, n_rows, batch, w):
    batch_major = h.ndim == 3
    d = h.shape[-1]
    dh = d // 2
    rows = HALF * batch
    n_steps = n_rows // rows
    tail = (CONV_W - 1) * batch
    const2 = lambda c: (0, 0)
    const3 = lambda c: (0, 0, 0)
    const4 = lambda c: (0, 0, 0, 0)
    proj_tile = lambda c: jnp.minimum(c, n_steps - 1)
    mix_tile = lambda c: jnp.maximum(c - 1, 0)
    h_spec = (pl.BlockSpec((batch, HALF, d), lambda c: (0, proj_tile(c), 0)) if batch_major
              else pl.BlockSpec((rows, d), lambda c: (proj_tile(c), 0)))
    return pl.pallas_call(
        functools.partial(_mix_prompt_kernel, batch=batch, batch_major=batch_major),
        grid=(n_steps + 1,),
        in_specs=[
            h_spec,
            pl.BlockSpec((1, d), const2),
            pl.BlockSpec((d, 2 * d), const2),
            pl.BlockSpec((1, dh), const2),
            pl.BlockSpec((2, GMLP_GROUPS, rows, rows), const4),
            pl.BlockSpec((GMLP_GROUPS, rows, rows), const3),
            pl.BlockSpec((2, rows, dh), const3),
            pl.BlockSpec((CONV_W, dh), const2),
            pl.BlockSpec((1, dh), const2),
            pl.BlockSpec((dh, dh), const2),
            pl.BlockSpec((1, dh), const2),
            pl.BlockSpec((dh, dh), const2),
            pl.BlockSpec((1, dh), const2),
            pl.BlockSpec((1, dh), const2),
            pl.BlockSpec((1, dh), const2),
            pl.BlockSpec((1, dh), const2),
        ],
        out_specs=[
            pl.BlockSpec((rows, d), lambda c: (mix_tile(c), 0)),
            pl.BlockSpec((rows, dh), lambda c: (jnp.maximum(mix_tile(c) - (n_steps - 2), 0), 0)),
            pl.BlockSpec((tail, dh), const2),
            pl.BlockSpec((batch, dh), const2),
        ],
        out_shape=[
            jax.ShapeDtypeStruct((n_rows, d), BF16),
            jax.ShapeDtypeStruct((2 * rows, dh), F32),
            jax.ShapeDtypeStruct((tail, dh), F32),
            jax.ShapeDtypeStruct((batch, dh), F32),
        ],
        scratch_shapes=[
            pltpu.VMEM((batch, dh), F32),
            pltpu.VMEM((tail + rows, dh), F32),
            pltpu.VMEM((rows, dh), BF16),
            pltpu.VMEM((rows, dh), F32),
            pltpu.VMEM((rows, dh), F32),
            pltpu.VMEM((rows, dh), F32),
            pltpu.VMEM((d // LANES, rows, LANES), F32),
            pltpu.VMEM((rows, 2 * d), F32),
            pltpu.VMEM((rows, 2 * d), F32),
        ],
        compiler_params=pltpu.CompilerParams(dimension_semantics=("arbitrary",), vmem_limit_bytes=VMEM_LIMIT),
        name="mix_prompt",
    )(h, w["g_mix"], w["w_in"], w["g_v"], w["kd"], w["k10"], w["bs_rows"], w["conv_w"], w["conv_b"],
      w["wa"], w["b_a"], w["wx"], w["b_x"], w["lam"], w["g_oa"], w["g_ob"])


def _sample_front_kernel(h_ref, h2_any, n2_any, gw_any, pos_any, cnt_any,
                         gmix_ref, win_ref, gv_ref, w0_ref, b0_ref, cw_ref, cb_ref,
                         wa_ref, ba_ref, wx_ref, bx_ref, lam_ref, goa_ref, gob_ref, sconv_ref, slru_ref,
                         wout_ref, gffn_ref, wr_ref, br_ref,
                         h2_ref, n2_ref, gw_ref, pos_ref, cnt_ref, v_ref, conv_ref, lru_ref,
                         *, block, n_valid, n_exp):
    del h2_any, n2_any, gw_any, pos_any, cnt_any
    dh = gv_ref.shape[1]
    h = h_ref[...]
    n = _rms(h, gmix_ref[...])
    proj = lambda lo: _dot_3pass(n, win_ref[:, lo:lo + dh])

    vn = _group_norm_128(_gelu(proj(dh)), gv_ref[...])
    v_ref[...] = vn
    s = vn * w0_ref[...] + b0_ref[...]
    ya = _group_norm_128(_gelu(proj(0)) * s, goa_ref[...])

    xb = proj(2 * dh)
    xc = cb_ref[...] + cw_ref[CONV_W - 1:CONV_W, :] * xb
    for k in range(CONV_W - 1):
        xc = xc + cw_ref[k:k + 1, :] * sconv_ref[k]
    for k in range(CONV_W - 2):
        conv_ref[k] = sconv_ref[k + 1]
    conv_ref[CONV_W - 2] = xb

    a, mult, i = _lru_gates(xc, wa_ref, ba_ref, wx_ref, bx_ref, lam_ref, _dot_3pass)
    h_new = a * slru_ref[...] + mult * i * xc
    lru_ref[...] = h_new
    yb = _group_norm_64(h_new * _gelu(proj(3 * dh)), gob_ref[...])

    h2 = h + _dot_3pass(ya, wout_ref[0:dh, :]) + _dot_3pass(yb, wout_ref[dh:2 * dh, :])
    _route_block(h2, gffn_ref, wr_ref, br_ref, h2_ref, n2_ref, gw_ref, pos_ref, cnt_ref,
                 block=block, n_valid=n_valid, n_exp=n_exp)


def _sample_front(h, h_block, post_out, block, w, sconv, slru, n_valid, n_exp):
    d = h.shape[1]
    dh = d // 2
    rows = TOKEN_BLOCK
    const2 = lambda i: (0, 0)
    const3 = lambda i: (0, 0, 0)
    vec = pl.BlockSpec((1, dh), const2)
    anyspace = pl.BlockSpec(memory_space=pl.ANY)
    blk_row = pl.BlockSpec((rows, d), lambda i: (block, 0))
    blk_tok = pl.BlockSpec((SUBLANES, rows), lambda i: (0, block))
    return pl.pallas_call(
        functools.partial(_sample_front_kernel, block=block, n_valid=n_valid, n_exp=n_exp),
        grid=(1,),
        in_specs=[
            pl.BlockSpec((rows, d), lambda i: (h_block, 0)),
            anyspace, anyspace, anyspace, anyspace, anyspace,
            pl.BlockSpec((1, d), const2),
            pl.BlockSpec((d, 2 * d), const2),
            vec, vec, vec,
            pl.BlockSpec((CONV_W, dh), const2),
            vec,
            pl.BlockSpec((dh, dh), const2),
            vec,
            pl.BlockSpec((dh, dh), const2),
            vec, vec, vec, vec,
            pl.BlockSpec((CONV_W - 1, rows, dh), const3),
            pl.BlockSpec((rows, dh), const2),
            pl.BlockSpec((d, d), const2),
            pl.BlockSpec((1, d), const2),
            pl.BlockSpec((d, LANES), const2),
            pl.BlockSpec((n_exp, 1), const2),
        ],
        out_specs=[
            blk_row, blk_row, blk_tok, blk_tok,
            pl.BlockSpec((1, SUBLANES, n_exp), lambda i: (block, 0, 0)),
            pl.BlockSpec((rows, dh), const2),
            pl.BlockSpec((CONV_W - 1, rows, dh), const3),
            pl.BlockSpec((rows, dh), const2),
        ],
        out_shape=[jax.ShapeDtypeStruct(a.shape, a.dtype) for a in post_out] + [
            jax.ShapeDtypeStruct((rows, dh), F32),
            jax.ShapeDtypeStruct((CONV_W - 1, rows, dh), F32),
            jax.ShapeDtypeStruct((rows, dh), F32),
        ],
        input_output_aliases={1 + k: k for k in range(len(post_out))},
        compiler_params=pltpu.CompilerParams(dimension_semantics=("arbitrary",), vmem_limit_bytes=VMEM_LIMIT),
        name="sample_front",
    )(h, *post_out, w["g_mix"], w["w_in_f32"], w["g_v"], w["w0_row"], w["b0_row"], w["conv_w"], w["conv_b"],
      w["wa_f32"], w["b_a"], w["wx_f32"], w["b_x"], w["lam"], w["g_oa"], w["g_ob"], sconv, slru,
      w["w_out_f32"], w["g_ffn"], w["w_r"], w["b_r"])


def _route_block(h2, gffn_ref, wr_ref, br_ref, h2_ref, n2_ref, gw_ref, pos_ref, cnt_ref,
                 *, block, n_valid, n_exp):
    tb = h2.shape[0]
    h2_ref[...] = h2
    n2 = _rms(h2, gffn_ref[...])
    n2_ref[...] = n2.astype(BF16)

    logits = _dot_3pass(n2, wr_ref[...])
    lt = logits.T[0:n_exp, :] + br_ref[...]

    eio = lax.broadcasted_iota(I32, (n_exp, tb), 0).astype(F32)
    col = block * tb + lax.broadcasted_iota(I32, (1, tb), 1)
    valid = col < n_valid
    vals, hots = [], []
    for _ in range(TOP_K):
        m = jnp.max(lt, axis=0, keepdims=True)
        sel = jnp.min(jnp.where(lt == m, eio, float(n_exp)), axis=0, keepdims=True)
        hot = eio == sel
        lt = jnp.where(hot, -jnp.inf, lt)
        vals.append(m)
        hots.append(hot)
    exps = [jnp.exp(v - vals[0]) for v in vals]
    den = exps[0]
    for e in exps[1:]:
        den = den + e

    occ = jnp.zeros((n_exp, tb), F32)
    for hot in hots:
        occ = occ + jnp.where(hot & valid, 1.0, 0.0)
    occ_b = occ.astype(BF16)
    before = (lax.broadcasted_iota(I32, (tb, tb), 0) < lax.broadcasted_iota(I32, (tb, tb), 1))
    ranks_e = _dot(occ_b, jnp.where(before, 1.0, 0.0).astype(BF16))
    cnt_ref[0] = lax.dot_general(jnp.ones((SUBLANES, tb), BF16), occ_b, (((1,), (1,)), ((), ())),
                                 preferred_element_type=F32)
    cnt_col = jnp.sum(occ, axis=1, keepdims=True)
    units = jnp.floor((cnt_col + (SEG_ALIGN - 1.0)) * (1.0 / SEG_ALIGN))
    earlier = (lax.broadcasted_iota(I32, (n_exp, n_exp), 1) < lax.broadcasted_iota(I32, (n_exp, n_exp), 0))
    seg_off = _dot(jnp.where(earlier, 1.0, 0.0).astype(BF16),
                   jnp.broadcast_to(units, (n_exp, LANES)).astype(BF16))[:, 0:1] * float(SEG_ALIGN)
    rows_e = ranks_e + seg_off

    rio = lax.broadcasted_iota(I32, (SUBLANES, tb), 0)
    gw_out = jnp.zeros((SUBLANES, tb), F32)
    pos_out = jnp.full((SUBLANES, tb), -1, I32)
    for k in range(TOP_K):
        pos_k = jnp.sum(jnp.where(hots[k], rows_e, 0.0), axis=0, keepdims=True).astype(I32)
        gw_out = jnp.where(rio == k, exps[k] / den, gw_out)
        pos_out = jnp.where(rio == k, jnp.where(valid, pos_k, -1), pos_out)
    gw_ref[...] = gw_out
    pos_ref[...] = pos_out


def _post_kernel(y_ref, h_ref, wout_ref, gffn_ref, wr_ref, br_ref,
                 h2_ref, n2_ref, gw_ref, pos_ref, cnt_ref, hrows, *, n_valid, n_exp, batch, batch_major):
    h = _time_major_rows(h_ref, hrows, batch) if batch_major else h_ref[...]
    h2 = h + _dot(y_ref[...], wout_ref[...])
    _route_block(h2, gffn_ref, wr_ref, br_ref, h2_ref, n2_ref, gw_ref, pos_ref, cnt_ref,
                 block=pl.program_id(0), n_valid=n_valid, n_exp=n_exp)


def _post(y, h, w, n_total, n_valid, n_exp, batch):
    batch_major = h.ndim == 3
    d = h.shape[-1]
    tb = TOKEN_BLOCK
    nb = n_total // tb
    const2 = lambda b: (0, 0)
    tok = pl.BlockSpec((SUBLANES, tb), lambda b: (0, b))
    h_spec = (pl.BlockSpec((batch, tb // batch, d), lambda b: (0, b, 0)) if batch_major
              else pl.BlockSpec((tb, d), lambda b: (b, 0)))
    return pl.pallas_call(
        functools.partial(_post_kernel, n_valid=n_valid, n_exp=n_exp, batch=batch, batch_major=batch_major),
        grid=(y.shape[0] // tb,),
        in_specs=[
            pl.BlockSpec((tb, d), lambda b: (b, 0)),
            h_spec,
            pl.BlockSpec((d, d), const2),
            pl.BlockSpec((1, d), const2),
            pl.BlockSpec((d, LANES), const2),
            pl.BlockSpec((n_exp, 1), const2),
        ],
        out_specs=[
            pl.BlockSpec((tb, d), lambda b: (b, 0)),
            pl.BlockSpec((tb, d), lambda b: (b, 0)),
            tok, tok,
            pl.BlockSpec((1, SUBLANES, n_exp), lambda b: (b, 0, 0)),
        ],
        out_shape=[
            jax.ShapeDtypeStruct((n_total, d), F32),
            jax.ShapeDtypeStruct((n_total, d), BF16),
            jax.ShapeDtypeStruct((SUBLANES, n_total), F32),
            jax.ShapeDtypeStruct((SUBLANES, n_total), I32),
            jax.ShapeDtypeStruct((nb, SUBLANES, n_exp), F32),
        ],
        scratch_shapes=[pltpu.VMEM((d // LANES, tb, LANES), F32)],
        compiler_params=pltpu.CompilerParams(dimension_semantics=("arbitrary",), vmem_limit_bytes=VMEM_LIMIT),
        name="post_router",
    )(y, h, w["w_out"], w["g_ffn"], w["w_r"], w["b_r"])


def _max_sorted_rows(n_valid, nb, n_exp):
    worst = TOP_K * n_valid + (SEG_ALIGN - 1) * nb * n_exp + (EXPERT_TILE - SEG_ALIGN) * n_exp
    return _round_up(worst, EXPERT_TILE)


def _local_rows(n_exp):
    return _round_up(TOP_K * TOKEN_BLOCK + (SEG_ALIGN - 1) * n_exp, LANES)


def _route_meta(cnt, max_tiles):
    nb, n_exp = cnt.shape
    p8 = (cnt + SEG_ALIGN - 1) // SEG_ALIGN * SEG_ALIGN
    off = jnp.cumsum(p8, axis=1) - p8
    tot = jnp.sum(p8, axis=1)
    seg = jnp.sum(p8, axis=0)
    reg = (seg + EXPERT_TILE - 1) // EXPERT_TILE * EXPERT_TILE
    reg_start = jnp.cumsum(reg) - reg
    start = reg_start[None, :] + jnp.cumsum(p8, axis=0) - p8
    piece_row = jnp.arange(_local_rows(n_exp) // SEG_ALIGN, dtype=I32) * SEG_ALIGN
    inside = (off[:, None, :] <= piece_row[None, :, None]) & (piece_row[None, :, None] < (off + p8)[:, None, :])
    piece_dst = piece_row[None, :] + jnp.sum(jnp.where(inside, (start - off)[:, None, :], 0), axis=2)
    piece_dst = jnp.where(piece_row[None, :] < tot[:, None], piece_dst, -1)
    fetch = SEG_ALIGN * FETCH_UNROLL
    tiles_end = jnp.cumsum(reg // EXPERT_TILE)
    tile_ids = jnp.arange(max_tiles, dtype=I32)
    tile_expert = jnp.minimum(jnp.sum((tiles_end[None, :] <= tile_ids[:, None]).astype(I32), axis=1), n_exp - 1)
    of_tile = tile_expert[:, None] == jnp.arange(n_exp, dtype=I32)[None, :]
    pick = lambda per_expert: jnp.sum(jnp.where(of_tile, per_expert[None, :], 0), axis=1)
    used = jnp.clip(pick(seg) - (tile_ids - pick(tiles_end - reg // EXPERT_TILE)) * EXPERT_TILE, 0, EXPERT_TILE)
    tile_rows = (used + EXPERT_SUBTILE - 1) // EXPERT_SUBTILE * EXPERT_SUBTILE
    sub_end = (seg + EXPERT_SUBTILE - 1) // EXPERT_SUBTILE * EXPERT_SUBTILE
    return dict(piece_dst=piece_dst.reshape(-1).astype(I32), tot=tot.astype(I32),
                tot_fetch=((tot + fetch - 1) // fetch * fetch).astype(I32),
                tail_start=(reg_start + seg).astype(I32), tail_len=(sub_end - seg).astype(I32),
                tile_expert=tile_expert.astype(I32), n_tiles=tiles_end[-1:].astype(I32),
                tiles_end=tiles_end.astype(I32), tile_rows=tile_rows.astype(I32))


def _wait_pieces(n_rows, src, dst, sem):
    def body(j, carry):
        pltpu.make_async_copy(src, dst, sem).wait()
        return carry
    lax.fori_loop(0, n_rows // SEG_ALIGN, body, 0)


def _wait_rows(n_rows, max_rows, src_rows, dst_rows, sem):
    size = SEG_ALIGN
    while size <= max_rows:
        @pl.when((n_rows & size) != 0)
        def _(size=size):
            pltpu.make_async_copy(src_rows(size), dst_rows(size), sem).wait()
        size *= 2


def _dispatch_kernel(dst_ref, tot_ref, tstart_ref, tlen_ref,
                     n2_ref, pos_ref, xs_hbm, xbuf, zbuf, sems, zsem, *, n_exp, mp):
    b = pl.program_id(0)
    nb = pl.num_programs(0)
    slot = b % 2
    tb = n2_ref.shape[0]
    pieces = mp // SEG_ALIGN
    vmem_rows = lambda n: xbuf.at[0, pl.ds(0, n), :]
    hbm_rows = lambda n: xs_hbm.at[pl.ds(0, n), :]

    @pl.when(b >= 2)
    def _():
        _wait_rows(tot_ref[jnp.maximum(b - 2, 0)], mp, vmem_rows, hbm_rows, sems.at[slot])

    pos = pos_ref[...]
    jio = lax.broadcasted_iota(I32, (mp, tb), 0)
    hit = jio == pos[0:1, :]
    for k in range(1, TOP_K):
        hit = hit | (jio == pos[k:k + 1, :])
    xbuf[slot] = _dot(jnp.where(hit, 1.0, 0.0).astype(BF16), n2_ref[...])

    def issue(j, carry):
        src = xbuf.at[slot, pl.ds(pl.multiple_of(j * SEG_ALIGN, SEG_ALIGN), SEG_ALIGN), :]
        dst = xs_hbm.at[pl.ds(pl.multiple_of(dst_ref[b * pieces + j], SEG_ALIGN), SEG_ALIGN), :]
        pltpu.make_async_copy(src, dst, sems.at[slot]).start()
        return carry
    lax.fori_loop(0, tot_ref[b] // SEG_ALIGN, issue, 0)

    @pl.when(b == nb - 1)
    def _():
        zbuf[...] = jnp.zeros_like(zbuf)
        for e in range(n_exp):
            ts = tstart_ref[e]

            def issue_zero(j, carry, ts=ts):
                dst = xs_hbm.at[pl.ds(pl.multiple_of(ts + j * SEG_ALIGN, SEG_ALIGN), SEG_ALIGN), :]
                pltpu.make_async_copy(zbuf, dst, zsem).start()
                return carry
            lax.fori_loop(0, tlen_ref[e] // SEG_ALIGN, issue_zero, 0)

        @pl.when(nb >= 2)
        def _():
            _wait_rows(tot_ref[jnp.maximum(nb - 2, 0)], mp, vmem_rows, hbm_rows, sems.at[1 - slot])
        _wait_rows(tot_ref[nb - 1], mp, vmem_rows, hbm_rows, sems.at[slot])
        for e in range(n_exp):
            _wait_pieces(tlen_ref[e], zbuf, hbm_rows(SEG_ALIGN), zsem)


def _dispatch(n2, pos, meta, n_exp, max_rows):
    n_total, d = n2.shape
    tb = TOKEN_BLOCK
    nb = n_total // tb
    mp = _local_rows(n_exp)
    tok = pl.BlockSpec((SUBLANES, tb), lambda b, *_: (0, b))
    return pl.pallas_call(
        functools.partial(_dispatch_kernel, n_exp=n_exp, mp=mp),
        grid_spec=pltpu.PrefetchScalarGridSpec(
            num_scalar_prefetch=4,
            grid=(nb,),
            in_specs=[pl.BlockSpec((tb, d), lambda b, *_: (b, 0)), tok],
            out_specs=pl.BlockSpec(memory_space=pl.ANY),
            scratch_shapes=[
                pltpu.VMEM((2, mp, d), F32),
                pltpu.VMEM((SEG_ALIGN, d), F32),
                pltpu.SemaphoreType.DMA((2,)),
                pltpu.SemaphoreType.DMA,
            ],
        ),
        out_shape=jax.ShapeDtypeStruct((max_rows, d), F32),
        compiler_params=pltpu.CompilerParams(dimension_semantics=("arbitrary",), vmem_limit_bytes=VMEM_LIMIT),
        name="moe_dispatch",
    )(meta["piece_dst"], meta["tot"], meta["tail_start"], meta["tail_len"], n2, pos)


def _expert_kernel(te_ref, nt_ref, tend_ref, rows_ref, x_ref, w1_hbm, b1_ref, w2_hbm, b2_ref, y_ref,
                   w1s, w2s, w1b, w2b, sems, n_loaded, *, layer):
    i = pl.program_id(0)
    n_tiles = nt_ref[0]
    dff = w2b.shape[0]

    def weight_copies(expert, slot):
        return (pltpu.make_async_copy(w1_hbm.at[layer, expert], w1s.at[slot], sems.at[0, slot]),
                pltpu.make_async_copy(w2_hbm.at[layer, expert], w2s.at[slot], sems.at[1, slot]))

    @pl.when(i < n_tiles)
    def _():
        expert = te_ref[i]

        @pl.when(i == 0)
        def _():
            n_loaded[0] = 0
            for c in weight_copies(expert, 0):
                c.start()

        @pl.when((i == 0) | (expert != te_ref[jnp.maximum(i - 1, 0)]))
        def _():
            slot = n_loaded[0] % 2
            n_loaded[0] = n_loaded[0] + 1
            for c in weight_copies(expert, slot):
                c.wait()
            nxt = tend_ref[expert]

            @pl.when(nxt < n_tiles)
            def _():
                for c in weight_copies(te_ref[jnp.minimum(nxt, n_tiles - 1)], 1 - slot):
                    c.start()
            w1b[...] = w1s[slot].astype(BF16)
            w2b[...] = w2s[slot].astype(BF16)

        for rows in range(EXPERT_SUBTILE, x_ref.shape[0] + 1, EXPERT_SUBTILE):
            @pl.when(rows_ref[i] == rows)
            def _(rows=rows):
                hdn = _dot(x_ref[0:rows, :].astype(BF16), w1b[...]) + b1_ref[0, 0]
                gate = jnp.minimum(hdn[:, 0:dff], SWIGLU_LIMIT)
                up = jnp.clip(hdn[:, dff:2 * dff], -SWIGLU_LIMIT, SWIGLU_LIMIT)
                act = (up + 1.0) * gate * _sigmoid(SWIGLU_ALPHA * gate)
                y_ref[0:rows, :] = _dot(act.astype(BF16), w2b[...]) + b2_ref[0, 0]


def _experts(xs, meta, layer, w1, b1, w2, b2):
    max_rows, d = xs.shape
    _, n_exp, _, dff2 = w1.shape
    dff = dff2 // 2
    tm = EXPERT_TILE
    max_tiles = max_rows // tm
    last = lambda i, nt: jnp.minimum(i, jnp.maximum(nt[0] - 1, 0))
    row_map = lambda i, te, nt, *_: (last(i, nt), 0)
    exp_map = lambda i, te, nt, *_: (layer, te[last(i, nt)], 0, 0)
    return pl.pallas_call(
        functools.partial(_expert_kernel, layer=layer),
        grid_spec=pltpu.PrefetchScalarGridSpec(
            num_scalar_prefetch=4,
            grid=(max_tiles,),
            in_specs=[
                pl.BlockSpec((tm, d), row_map),
                pl.BlockSpec(memory_space=pl.ANY),
                pl.BlockSpec((1, 1, 1, dff2), exp_map),
                pl.BlockSpec(memory_space=pl.ANY),
                pl.BlockSpec((1, 1, 1, d), exp_map),
            ],
            out_specs=pl.BlockSpec((tm, d), row_map),
            scratch_shapes=[
                pltpu.VMEM((2, d, dff2), F32), pltpu.VMEM((2, dff, d), F32),
                pltpu.VMEM((d, dff2), BF16), pltpu.VMEM((dff, d), BF16),
                pltpu.SemaphoreType.DMA((2, 2)),
                pltpu.SMEM((1,), I32),
            ],
        ),
        out_shape=jax.ShapeDtypeStruct((max_rows, d), F32),
        compiler_params=pltpu.CompilerParams(dimension_semantics=("arbitrary",), vmem_limit_bytes=VMEM_LIMIT),
        name="moe_experts",
    )(meta["tile_expert"], meta["n_tiles"], meta["tiles_end"], meta["tile_rows"], xs, w1, b1, w2, b2)


def _combine_kernel(src_ref, fetch_ref,
                    h2_ref, gw_ref, pos_ref, pp_ref, ps_ref, wpg_ref, bpg_ref, wpp_ref, gple_ref, gfin_ref,
                    ys_hbm, *rest, final, mp, batch):
    if final:
        outp_ref, outs_ref, ybuf, sems, prows, orows = rest
    else:
        hn_ref, ybuf, sems, prows = rest
    b = pl.program_id(0)
    nb = pl.num_programs(0)
    slot = b % 2
    tb = h2_ref.shape[0]
    pieces = mp // SEG_ALIGN

    def fetch(blk, sl):
        def issue(g, carry):
            for u in range(FETCH_UNROLL):
                j = g * FETCH_UNROLL + u
                row = jnp.maximum(src_ref[blk * pieces + j], 0)
                src = ys_hbm.at[pl.ds(pl.multiple_of(row, SEG_ALIGN), SEG_ALIGN), :]
                dst = ybuf.at[sl, pl.ds(pl.multiple_of(j * SEG_ALIGN, SEG_ALIGN), SEG_ALIGN), :]
                pltpu.make_async_copy(src, dst, sems.at[sl]).start()
            return carry
        lax.fori_loop(0, fetch_ref[blk] // (SEG_ALIGN * FETCH_UNROLL), issue, 0)

    @pl.when(b == 0)
    def _():
        ybuf[...] = jnp.zeros_like(ybuf)
        fetch(0, 0)

    @pl.when(b + 1 < nb)
    def _():
        fetch(jnp.minimum(b + 1, nb - 1), 1 - slot)

    _wait_rows(fetch_ref[b], mp, lambda n: ys_hbm.at[pl.ds(0, n), :], lambda n: ybuf.at[0, pl.ds(0, n), :],
               sems.at[slot])

    pos = pos_ref[...]
    gw = gw_ref[...]
    jio = lax.broadcasted_iota(I32, (mp, tb), 0)
    cg = jnp.where(jio == pos[0:1, :], gw[0:1, :], 0.0)
    for k in range(1, TOP_K):
        cg = cg + jnp.where(jio == pos[k:k + 1, :], gw[k:k + 1, :], 0.0)
    moe = lax.dot_general(cg.astype(BF16), ybuf[slot].astype(BF16), (((0,), (0,)), ((), ())),
                          preferred_element_type=F32)
    h3 = h2_ref[...] + moe
    n3 = _rms(h3, gple_ref[...]).astype(BF16)
    gate = _sigmoid(_dot(n3, wpg_ref[...]) + bpg_ref[...])
    p_rows = jnp.where(b == nb - 1, ps_ref[0], _time_major_rows(pp_ref.at[0], prows, batch))
    h4 = h3 + gate * _dot(p_rows.astype(BF16), wpp_ref[...])
    if final:
        out = _rms(h4, gfin_ref[...])

        @pl.when(b < nb - 1)
        def _():
            _batch_major_store(out, orows, outp_ref, batch)

        @pl.when(b == nb - 1)
        def _():
            outs_ref[...] = out
    else:
        hn_ref[...] = h4


def _combine(h2, gw, pos, p_prompt, p_sample, layer, ys, meta, w, g_final, n_exp, final):
    n_total, d = h2.shape
    _, batch, seq, ple = p_prompt.shape
    tb = TOKEN_BLOCK
    nb = n_total // tb
    steps = tb // batch
    mp = _local_rows(n_exp)
    const2 = lambda b, *_: (0, 0)
    tok = pl.BlockSpec((SUBLANES, tb), lambda b, *_: (0, b))
    row = pl.BlockSpec((tb, d), lambda b, *_: (b, 0))
    prompt_blk = lambda b: jnp.minimum(b, nb - 2)
    if final:
        out_specs = [pl.BlockSpec((batch, steps, d), lambda b, *_: (0, prompt_blk(b), 0)),
                     pl.BlockSpec((tb, d), const2)]
        out_shape = [jax.ShapeDtypeStruct((batch, seq, d), F32), jax.ShapeDtypeStruct((tb, d), F32)]
        extra_scratch = [pltpu.VMEM((d // LANES, tb, LANES), F32)]
    else:
        out_specs = [row]
        out_shape = [jax.ShapeDtypeStruct((n_total, d), F32)]
        extra_scratch = []
    return pl.pallas_call(
        functools.partial(_combine_kernel, final=final, mp=mp, batch=batch),
        grid_spec=pltpu.PrefetchScalarGridSpec(
            num_scalar_prefetch=2,
            grid=(nb,),
            in_specs=[
                row, tok, tok,
                pl.BlockSpec((1, batch, steps, ple), lambda b, *_: (layer, 0, prompt_blk(b), 0)),
                pl.BlockSpec((1, tb, ple), lambda b, *_: (layer, 0, 0)),
                pl.BlockSpec((d, d), const2),
                pl.BlockSpec((1, d), const2),
                pl.BlockSpec((ple, d), const2),
                pl.BlockSpec((1, d), const2),
                pl.BlockSpec((1, d), const2),
                pl.BlockSpec(memory_space=pl.ANY),
            ],
            out_specs=out_specs,
            scratch_shapes=[pltpu.VMEM((2, mp, d), F32), pltpu.SemaphoreType.DMA((2,)),
                            pltpu.VMEM((ple // LANES, tb, LANES), F32)] + extra_scratch,
        ),
        out_shape=out_shape,
        compiler_params=pltpu.CompilerParams(dimension_semantics=("arbitrary",), vmem_limit_bytes=VMEM_LIMIT),
        name="moe_combine_ple",
    )(meta["piece_dst"], meta["tot_fetch"],
      h2, gw, pos, p_prompt, p_sample, w["w_pg"], w["b_pg"], w["w_pp"], w["g_ple"], g_final, ys)


def _layer_weights(l, batch, g_mix, w_in, g_v, w_s, b_s, conv_w, conv_b, w_a, b_a, w_x, b_x, lam,
                   g_oa, g_ob, w_out, g_ffn, w_r, b_r, g_ple, w_pg, b_pg, w_pp):
    d = w_in.shape[1]
    dh = d // 2
    gd = dh // GMLP_GROUPS
    n_exp = w_r.shape[2]
    row = lambda a: a[l].reshape(1, -1).astype(F32)
    wt = jnp.where(jnp.tril(jnp.ones((CHUNK, CHUNK), bool)), w_s[l], 0.0).astype(BF16)
    hr = HALF * batch
    row_t = jnp.arange(hr, dtype=I32) // batch
    row_b = jnp.arange(hr, dtype=I32) % batch
    expand_t = (row_t[:, None] == jnp.arange(HALF, dtype=I32)[None, :]).astype(BF16)
    same_b = row_b[:, None] == row_b[None, :]

    def kron_block(blk):
        rows = jnp.einsum("it,gts->gis", expand_t, blk, preferred_element_type=F32).astype(BF16)
        full = jnp.einsum("gis,js->gij", rows, expand_t, preferred_element_type=F32)
        return jnp.where(same_b, full, 0.0).astype(BF16)

    kd = jnp.stack([kron_block(wt[:, :HALF, :HALF]), kron_block(wt[:, HALF:, HALF:])])
    k10 = kron_block(wt[:, HALF:, :HALF])
    bs_rows = jnp.repeat(jnp.repeat(b_s[l].T.astype(F32), gd, axis=1), batch, axis=0).reshape(2, hr, dh)
    hd = dh // LRU_HEADS
    head_of = jnp.arange(dh, dtype=I32) // hd
    same_head = head_of[:, None] == head_of[None, :]
    block_diag = lambda w: jnp.where(same_head, jnp.tile(w.astype(F32).reshape(dh, hd), (1, LRU_HEADS)), 0.0)
    wa_f32, wx_f32 = block_diag(w_a[l]), block_diag(w_x[l])
    w_r_pad = jnp.zeros((d, LANES), F32).at[:, :n_exp].set(w_r[l])
    return dict(
        g_mix=row(g_mix), w_in=w_in[l].astype(BF16), w_in_f32=w_in[l].astype(F32), g_v=row(g_v),
        kd=kd, k10=k10, bs_rows=bs_rows,
        w0_row=jnp.repeat(w_s[l][:, 0, 0], gd).reshape(1, dh).astype(F32),
        b0_row=jnp.repeat(b_s[l][:, 0], gd).reshape(1, dh).astype(F32),
        conv_w=conv_w[l].astype(F32), conv_b=row(conv_b),
        wa=wa_f32.astype(BF16), wa_f32=wa_f32, b_a=row(b_a), wx=wx_f32.astype(BF16), wx_f32=wx_f32,
        b_x=row(b_x), lam=row(lam),
        g_oa=row(g_oa), g_ob=row(g_ob), w_out=w_out[l].astype(BF16), w_out_f32=w_out[l].astype(F32),
        g_ffn=row(g_ffn),
        w_r=w_r_pad, b_r=b_r[l].reshape(n_exp, 1).astype(F32),
        g_ple=row(g_ple), w_pg=w_pg[l].astype(BF16), b_pg=row(b_pg), w_pp=w_pp[l].astype(BF16))


def kernel(x_prompt, x_sample, state_conv, state_lru, p_prompt, p_sample, g_mix, w_in, g_v, w_s, b_s, conv_w, conv_b, w_a, b_a, w_x, b_x, lam, g_oa, g_ob, w_out, g_ffn, w_r, b_r, w1, b1, w2, b2, g_ple, w_pg, b_pg, w_pp, g_final):
    batch, seq, d = x_prompt.shape
    dec_batch, dec_seq, _ = x_sample.shape
    depth = w_in.shape[0]
    n_exp = w_r.shape[2]
    dh = d // 2
    ple = p_prompt.shape[-1]
    assert dec_seq == 1 and seq % CHUNK == 0 and batch % SUBLANES == 0
    assert dh == GMLP_GROUPS * LANES and dh % (LRU_HEADS * (LANES // 2)) == 0
    n_prompt = seq * batch
    assert n_prompt % TOKEN_BLOCK == 0
    n_valid = n_prompt + dec_batch
    n_total = _round_up(n_valid, TOKEN_BLOCK)
    s_rows = n_total - n_prompt
    assert s_rows == TOKEN_BLOCK
    nb = n_total // TOKEN_BLOCK
    max_rows = _max_sorted_rows(n_valid, nb, n_exp)
    pad_rows = ((0, 0), (0, s_rows - dec_batch), (0, 0))

    h, h_sample, h_sample_block = x_prompt, jnp.pad(x_sample.reshape(1, dec_batch, d), pad_rows)[0], 0
    p_samp = jnp.pad(p_sample.reshape(depth, dec_batch, ple), pad_rows)
    sconv = jnp.pad(state_conv.transpose(0, 2, 1, 3), ((0, 0),) + pad_rows)
    slru = jnp.pad(state_lru, pad_rows)
    g_fin = g_final.reshape(1, d).astype(F32)
    b1_rows = b1.reshape(depth, n_exp, 1, -1)
    b2_rows = b2.reshape(depth, n_exp, 1, -1)

    v_p, conv_p, lru_p, v_s, conv_s, lru_s = [], [], [], [], [], []
    for l in range(depth):
        w = _layer_weights(l, batch, g_mix, w_in, g_v, w_s, b_s, conv_w, conv_b, w_a, b_a, w_x, b_x, lam,
                           g_oa, g_ob, w_out, g_ffn, w_r, b_r, g_ple, w_pg, b_pg, w_pp)
        y, vp, cp, lp = _mix_prompt(h, n_prompt, batch, w)
        post_out = _post(y, h, w, n_total, n_valid, n_exp, batch)
        h2, n2, gw, pos, cnt, vs, cs, ls = _sample_front(
            h_sample, h_sample_block, post_out, nb - 1, w, sconv[l], slru[l], n_valid, n_exp)
        meta = _route_meta(cnt[:, 0, :].astype(I32), max_rows // EXPERT_TILE)
        xs = _dispatch(n2, pos, meta, n_exp, max_rows)
        ys = _experts(xs, meta, l, w1, b1_rows, w2, b2_rows)
        final = l == depth - 1
        res = _combine(h2, gw, pos, p_prompt, p_samp, l, ys, meta, w, g_fin, n_exp, final)
        if final:
            y_prompt, out_sample = res
        else:
            h = h_sample = res[0]
            h_sample_block = nb - 1
        v_p.append(vp.reshape(CHUNK, batch, dh).transpose(1, 0, 2))
        conv_p.append(cp.reshape(CONV_W - 1, batch, dh).transpose(1, 0, 2))
        lru_p.append(lp)
        v_s.append(vs[:dec_batch].reshape(dec_batch, 1, dh))
        conv_s.append(cs[:, :dec_batch].transpose(1, 0, 2))
        lru_s.append(ls[:dec_batch])

    y_sample = out_sample[:dec_batch].reshape(dec_batch, 1, d)
    return (y_prompt, y_sample, jnp.stack(v_p), jnp.stack(conv_p), jnp.stack(lru_p),
            jnp.stack(v_s), jnp.stack(conv_s), jnp.stack(lru_s))
```

```python
import functools

import jax
import jax.numpy as jnp
from jax import lax
from jax.experimental import pallas as pl
from jax.experimental.pallas import tpu as pltpu

F32 = jnp.float32
BF16 = jnp.bfloat16
I32 = jnp.int32

CHUNK = 128
HALF = CHUNK // 2
GMLP_GROUPS = 4
LRU_HEADS = 8
CONV_W = 4
LRU_C = 8.0
TOP_K = 4
SWIGLU_LIMIT = 7.0
SWIGLU_ALPHA = 1.702
EPS = 1e-6

LANES = 128
SUBLANES = 8
TOKEN_BLOCK = 256
EXPERT_TILE = 512
EXPERT_SUBTILE = 128
SEG_ALIGN = SUBLANES
VMEM_LIMIT = 56 * 1024 * 1024


def _round_up(x, m):
    return (x + m - 1) // m * m


def _dot(a, b):
    return jnp.dot(a, b, preferred_element_type=F32)


def _gelu(x):
    return 0.5 * x * (1.0 + jnp.tanh(0.7978845608028654 * (x + 0.044715 * (x * x * x))))


def _sigmoid(x):
    return 1.0 / (1.0 + jnp.exp(-x))


def _softplus(x):
    return jnp.maximum(x, 0.0) + jnp.log1p(jnp.exp(-jnp.abs(x)))


def _rms(x, g):
    ms = jnp.mean(x * x, axis=-1, keepdims=True)
    return x * lax.rsqrt(ms + EPS) * g


def _group_norm_128(x, g):
    outs = []
    for j in range(x.shape[1] // LANES):
        blk = x[:, j * LANES:(j + 1) * LANES]
        ms = jnp.mean(blk * blk, axis=-1, keepdims=True)
        outs.append(blk * lax.rsqrt(ms + EPS))
    return jnp.concatenate(outs, axis=1) * g


def _group_norm_64(x, g):
    half = LANES // 2
    lo_mask = lax.broadcasted_iota(I32, (1, LANES), 1) < half
    outs = []
    for j in range(x.shape[1] // LANES):
        blk = x[:, j * LANES:(j + 1) * LANES]
        sq = blk * blk
        lo = jnp.sum(jnp.where(lo_mask, sq, 0.0), axis=-1, keepdims=True)
        hi = jnp.sum(jnp.where(lo_mask, 0.0, sq), axis=-1, keepdims=True)
        ms = jnp.where(lo_mask, lo, hi) * (1.0 / half)
        outs.append(blk * lax.rsqrt(ms + EPS))
    return jnp.concatenate(outs, axis=1) * g


def _time_major_rows(src_ref, slabs_ref, batch):
    steps = src_ref.shape[1]
    n_slab = slabs_ref.shape[0]
    for b in range(batch):
        blk = src_ref[b]
        for j in range(n_slab):
            slabs_ref[j, pl.ds(b, steps, stride=batch), :] = blk[:, j * LANES:(j + 1) * LANES]
    return jnp.concatenate([slabs_ref[j] for j in range(n_slab)], axis=1)


def _batch_major_store(rows, slabs_ref, dst_ref, batch):
    steps = dst_ref.shape[1]
    n_slab = slabs_ref.shape[0]
    for j in range(n_slab):
        slabs_ref[j] = rows[:, j * LANES:(j + 1) * LANES]
    for b in range(batch):
        dst_ref[b] = jnp.concatenate(
            [slabs_ref[j, pl.ds(b, steps, stride=batch), :] for j in range(n_slab)], axis=1)


def _split_bf16(x):
    hi = x.astype(BF16)
    lo = (x - hi.astype(F32)).astype(BF16)
    return hi, lo


def _dot_bf16(x, w):
    return _dot(x.astype(BF16), w)


def _dot_3pass(x, w):
    x_hi, x_lo = _split_bf16(x)
    w_hi, w_lo = _split_bf16(w)
    return _dot(x_hi, w_hi) + (_dot(x_lo, w_hi) + _dot(x_hi, w_lo))


def _lru_gates(xc, wa_ref, ba_ref, wx_ref, bx_ref, lam_ref, mm):
    r = _sigmoid(mm(xc, wa_ref[...]) + ba_ref[...])
    i = _sigmoid(mm(xc, wx_ref[...]) + bx_ref[...])
    log_a = (-LRU_C * r) * _softplus(-lam_ref[...])
    a = jnp.exp(log_a)
    mult = jnp.sqrt(-jnp.tanh(log_a) * (a * a + 1.0))
    return a, mult, i


def _mix_prompt_kernel(h_ref, gmix_ref, win_ref, gv_ref, kd_ref, k10_ref, bs_ref, cw_ref, cb_ref,
                       wa_ref, ba_ref, wx_ref, bx_ref, lam_ref, goa_ref, gob_ref,
                       y_ref, v_ref, conv_ref, lru_ref,
                       hstate, xpad, vprev, a_s, b_s, hs_s, hrows, z_even, z_odd, *, batch, batch_major):
    c = pl.program_id(0)
    rows = y_ref.shape[0]
    dh = gv_ref.shape[1]
    tail = (CONV_W - 1) * batch

    @pl.when(c == 0)
    def _():
        z_odd[...] = jnp.zeros_like(z_odd)
        vprev[...] = jnp.zeros_like(vprev)

    @pl.when(c <= 1)
    def _():
        hstate[...] = jnp.zeros_like(hstate)
        xpad[0:tail, :] = jnp.zeros((tail, dh), F32)

    def tile_step(z_in, z_out, par):
        h = _time_major_rows(h_ref, hrows, batch) if batch_major else h_ref[...]
        z_out[...] = _dot(_rms(h, gmix_ref[...]).astype(BF16), win_ref[...])

        if par == 0:
            vprev[...] = jnp.zeros_like(vprev)

        vn = _group_norm_128(_gelu(z_in[:, dh:2 * dh]), gv_ref[...])
        vb = vn.astype(BF16)
        s_parts = []
        for g in range(GMLP_GROUPS):
            sl = slice(g * LANES, (g + 1) * LANES)
            s_parts.append(_dot(kd_ref[par, g], vb[:, sl]) + _dot(k10_ref[g], vprev[:, sl]))
        s = jnp.concatenate(s_parts, axis=1) + bs_ref[par]
        vprev[...] = vb
        v_ref[...] = vn
        y_ref[:, 0:dh] = _group_norm_128(_gelu(z_in[:, 0:dh]) * s, goa_ref[...]).astype(BF16)

        xpad[tail:tail + rows, :] = z_in[:, 2 * dh:3 * dh]
        xc = cb_ref[...] + cw_ref[0:1, :] * xpad[0:rows, :]
        for k in range(1, CONV_W):
            xc = xc + cw_ref[k:k + 1, :] * xpad[k * batch:k * batch + rows, :]
        new_tail = xpad[rows:rows + tail, :]
        xpad[0:tail, :] = new_tail

        a, mult, i = _lru_gates(xc, wa_ref, ba_ref, wx_ref, bx_ref, lam_ref, _dot_bf16)
        row = lax.broadcasted_iota(I32, (rows, 1), 0)
        mult = jnp.where((c == 1) & (row < batch), 1.0, mult)
        a_s[...] = a
        b_s[...] = mult * i * xc

        def step(t, h):
            r0 = t * batch
            h = a_s[r0:r0 + batch, :] * h + b_s[r0:r0 + batch, :]
            hs_s[r0:r0 + batch, :] = h
            return h

        h_last = hstate[...]
        for t in range(rows // batch):
            h_last = step(t, h_last)
        hstate[...] = h_last

        y_ref[:, dh:2 * dh] = _group_norm_64(hs_s[...] * _gelu(z_in[:, 3 * dh:4 * dh]), gob_ref[...]).astype(BF16)
        conv_ref[...] = new_tail
        lru_ref[...] = h_last

    @pl.when(c % 2 == 0)
    def _():
        tile_step(z_odd, z_even, 1)

    @pl.when(c % 2 == 1)
    def _():
        tile_step(z_even, z_odd, 0)


def _mix_prompt(h, n_rows, batch, w):
    batch_major = h.ndim == 3
    d = h.shape[-1]
    dh = d // 2
    rows = HALF * batch
    n_steps = n_rows // rows
    tail = (CONV_W - 1) * batch
    const2 = lambda c: (0, 0)
    const3 = lambda c: (0, 0, 0)
    const4 = lambda c: (0, 0, 0, 0)
    proj_tile = lambda c: jnp.minimum(c, n_steps - 1)
    mix_tile = lambda c: jnp.maximum(c - 1, 0)
    h_spec = (pl.BlockSpec((batch, HALF, d), lambda c: (0, proj_tile(c), 0)) if batch_major
              else pl.BlockSpec((rows, d), lambda c: (proj_tile(c), 0)))
    return pl.pallas_call(
        functools.partial(_mix_prompt_kernel, batch=batch, batch_major=batch_major),
        grid=(n_steps + 1,),
        in_specs=[
            h_spec,
            pl.BlockSpec((1, d), const2),
            pl.BlockSpec((d, 2 * d), const2),
            pl.BlockSpec((1, dh), const2),
            pl.BlockSpec((2, GMLP_GROUPS, rows, rows), const4),
            pl.BlockSpec((GMLP_GROUPS, rows, rows), const3),
            pl.BlockSpec((2, rows, dh), const3),
            pl.BlockSpec((CONV_W, dh), const2),
            pl.BlockSpec((1, dh), const2),
            pl.BlockSpec((dh, dh), const2),
            pl.BlockSpec((1, dh), const2),
            pl.BlockSpec((dh, dh), const2),
            pl.BlockSpec((1, dh), const2),
            pl.BlockSpec((1, dh), const2),
            pl.BlockSpec((1, dh), const2),
            pl.BlockSpec((1, dh), const2),
        ],
        out_specs=[
            pl.BlockSpec((rows, d), lambda c: (mix_tile(c), 0)),
            pl.BlockSpec((rows, dh), lambda c: (jnp.maximum(mix_tile(c) - (n_steps - 2), 0), 0)),
            pl.BlockSpec((tail, dh), const2),
            pl.BlockSpec((batch, dh), const2),
        ],
        out_shape=[
            jax.ShapeDtypeStruct((n_rows, d), BF16),
            jax.ShapeDtypeStruct((2 * rows, dh), F32),
            jax.ShapeDtypeStruct((tail, dh), F32),
            jax.ShapeDtypeStruct((batch, dh), F32),
        ],
        scratch_shapes=[
            pltpu.VMEM((batch, dh), F32),
            pltpu.VMEM((tail + rows, dh), F32),
            pltpu.VMEM((rows, dh), BF16),
            pltpu.VMEM((rows, dh), F32),
            pltpu.VMEM((rows, dh), F32),
            pltpu.VMEM((rows, dh), F32),
            pltpu.VMEM((d // LANES, rows, LANES), F32),
            pltpu.VMEM((rows, 2 * d), F32),
            pltpu.VMEM((rows, 2 * d), F32),
        ],
        compiler_params=pltpu.CompilerParams(dimension_semantics=("arbitrary",), vmem_limit_bytes=VMEM_LIMIT),
        name="mix_prompt",
    )(h, w["g_mix"], w["w_in"], w["g_v"], w["kd"], w["k10"], w["bs_rows"], w["conv_w"], w["conv_b"],
      w["wa"], w["b_a"], w["wx"], w["b_x"], w["lam"], w["g_oa"], w["g_ob"])


def _sample_front_kernel(h_ref, h2_any, xg_any, gw_any, pos_any, cnt_any,
                         gmix_ref, win_ref, gv_ref, w0_ref, b0_ref, cw_ref, cb_ref,
                         wa_ref, ba_ref, wx_ref, bx_ref, lam_ref, goa_ref, gob_ref, sconv_ref, slru_ref,
                         wout_ref, gffn_ref, wr_ref, br_ref,
                         h2_ref, xg_ref, gw_ref, pos_ref, cnt_ref, v_ref, conv_ref, lru_ref,
                         *, block, n_valid, n_exp):
    del h2_any, xg_any, gw_any, pos_any, cnt_any
    dh = gv_ref.shape[1]
    h = h_ref[...]
    n = _rms(h, gmix_ref[...])
    proj = lambda lo: _dot_3pass(n, win_ref[:, lo:lo + dh])

    vn = _group_norm_128(_gelu(proj(dh)), gv_ref[...])
    v_ref[...] = vn
    s = vn * w0_ref[...] + b0_ref[...]
    ya = _group_norm_128(_gelu(proj(0)) * s, goa_ref[...])

    xb = proj(2 * dh)
    xc = cb_ref[...] + cw_ref[CONV_W - 1:CONV_W, :] * xb
    for k in range(CONV_W - 1):
        xc = xc + cw_ref[k:k + 1, :] * sconv_ref[k]
    for k in range(CONV_W - 2):
        conv_ref[k] = sconv_ref[k + 1]
    conv_ref[CONV_W - 2] = xb

    a, mult, i = _lru_gates(xc, wa_ref, ba_ref, wx_ref, bx_ref, lam_ref, _dot_3pass)
    h_new = a * slru_ref[...] + mult * i * xc
    lru_ref[...] = h_new
    yb = _group_norm_64(h_new * _gelu(proj(3 * dh)), gob_ref[...])

    h2 = h + _dot_3pass(ya, wout_ref[0:dh, :]) + _dot_3pass(yb, wout_ref[dh:2 * dh, :])
    _route_block(h2, gffn_ref, wr_ref, br_ref, h2_ref, xg_ref, gw_ref, pos_ref, cnt_ref,
                 block=block, n_valid=n_valid, n_exp=n_exp)


def _sample_front(h, h_block, post_out, block, w, sconv, slru, n_valid, n_exp):
    d = h.shape[1]
    dh = d // 2
    rows = TOKEN_BLOCK
    const2 = lambda i: (0, 0)
    const3 = lambda i: (0, 0, 0)
    vec = pl.BlockSpec((1, dh), const2)
    anyspace = pl.BlockSpec(memory_space=pl.ANY)
    blk_row = pl.BlockSpec((rows, d), lambda i: (block, 0))
    blk_tok = pl.BlockSpec((SUBLANES, rows), lambda i: (0, block))
    return pl.pallas_call(
        functools.partial(_sample_front_kernel, block=block, n_valid=n_valid, n_exp=n_exp),
        grid=(1,),
        in_specs=[
            pl.BlockSpec((rows, d), lambda i: (h_block, 0)),
            anyspace, anyspace, anyspace, anyspace, anyspace,
            pl.BlockSpec((1, d), const2),
            pl.BlockSpec((d, 2 * d), const2),
            vec, vec, vec,
            pl.BlockSpec((CONV_W, dh), const2),
            vec,
            pl.BlockSpec((dh, dh), const2),
            vec,
            pl.BlockSpec((dh, dh), const2),
            vec, vec, vec, vec,
            pl.BlockSpec((CONV_W - 1, rows, dh), const3),
            pl.BlockSpec((rows, dh), const2),
            pl.BlockSpec((d, d), const2),
            pl.BlockSpec((1, d), const2),
            pl.BlockSpec((d, LANES), const2),
            pl.BlockSpec((n_exp, 1), const2),
        ],
        out_specs=[
            blk_row, pl.BlockSpec((_local_rows(n_exp), d), lambda i: (block, 0)), blk_tok, blk_tok,
            pl.BlockSpec((1, SUBLANES, n_exp), lambda i: (block, 0, 0)),
            pl.BlockSpec((rows, dh), const2),
            pl.BlockSpec((CONV_W - 1, rows, dh), const3),
            pl.BlockSpec((rows, dh), const2),
        ],
        out_shape=[jax.ShapeDtypeStruct(a.shape, a.dtype) for a in post_out] + [
            jax.ShapeDtypeStruct((rows, dh), F32),
            jax.ShapeDtypeStruct((CONV_W - 1, rows, dh), F32),
            jax.ShapeDtypeStruct((rows, dh), F32),
        ],
        input_output_aliases={1 + k: k for k in range(len(post_out))},
        compiler_params=pltpu.CompilerParams(dimension_semantics=("arbitrary",), vmem_limit_bytes=VMEM_LIMIT),
        name="sample_front",
    )(h, *post_out, w["g_mix"], w["w_in_f32"], w["g_v"], w["w0_row"], w["b0_row"], w["conv_w"], w["conv_b"],
      w["wa_f32"], w["b_a"], w["wx_f32"], w["b_x"], w["lam"], w["g_oa"], w["g_ob"], sconv, slru,
      w["w_out_f32"], w["g_ffn"], w["w_r"], w["b_r"])


def _route_block(h2, gffn_ref, wr_ref, br_ref, h2_ref, xg_ref, gw_ref, pos_ref, cnt_ref,
                 *, block, n_valid, n_exp):
    tb = h2.shape[0]
    h2_ref[...] = h2
    n2 = _rms(h2, gffn_ref[...])

    logits = _dot_3pass(n2, wr_ref[...])
    lt = logits.T[0:n_exp, :] + br_ref[...]

    eio = lax.broadcasted_iota(I32, (n_exp, tb), 0).astype(F32)
    col = block * tb + lax.broadcasted_iota(I32, (1, tb), 1)
    valid = col < n_valid
    vals, hots = [], []
    for _ in range(TOP_K):
        m = jnp.max(lt, axis=0, keepdims=True)
        sel = jnp.min(jnp.where(lt == m, eio, float(n_exp)), axis=0, keepdims=True)
        hot = eio == sel
        lt = jnp.where(hot, -jnp.inf, lt)
        vals.append(m)
        hots.append(hot)
    exps = [jnp.exp(v - vals[0]) for v in vals]
    den = exps[0]
    for e in exps[1:]:
        den = den + e

    occ = jnp.zeros((n_exp, tb), F32)
    for hot in hots:
        occ = occ + jnp.where(hot & valid, 1.0, 0.0)
    occ_b = occ.astype(BF16)
    before = (lax.broadcasted_iota(I32, (tb, tb), 0) < lax.broadcasted_iota(I32, (tb, tb), 1))
    ranks_e = _dot(occ_b, jnp.where(before, 1.0, 0.0).astype(BF16))
    cnt_ref[0] = lax.dot_general(jnp.ones((SUBLANES, tb), BF16), occ_b, (((1,), (1,)), ((), ())),
                                 preferred_element_type=F32)
    cnt_col = jnp.sum(occ, axis=1, keepdims=True)
    units = jnp.floor((cnt_col + (SEG_ALIGN - 1.0)) * (1.0 / SEG_ALIGN))
    earlier = (lax.broadcasted_iota(I32, (n_exp, n_exp), 1) < lax.broadcasted_iota(I32, (n_exp, n_exp), 0))
    seg_off = _dot(jnp.where(earlier, 1.0, 0.0).astype(BF16),
                   jnp.broadcast_to(units, (n_exp, LANES)).astype(BF16))[:, 0:1] * float(SEG_ALIGN)
    rows_e = ranks_e + seg_off

    rio = lax.broadcasted_iota(I32, (SUBLANES, tb), 0)
    gw_out = jnp.zeros((SUBLANES, tb), F32)
    pos_out = jnp.full((SUBLANES, tb), -1, I32)
    for k in range(TOP_K):
        pos_k = jnp.sum(jnp.where(hots[k], rows_e, 0.0), axis=0, keepdims=True).astype(I32)
        gw_out = jnp.where(rio == k, exps[k] / den, gw_out)
        pos_out = jnp.where(rio == k, jnp.where(valid, pos_k, -1), pos_out)
    gw_ref[...] = gw_out
    pos_ref[...] = pos_out

    jio = lax.broadcasted_iota(I32, (xg_ref.shape[0], tb), 0)
    hit = jio == pos_out[0:1, :]
    for k in range(1, TOP_K):
        hit = hit | (jio == pos_out[k:k + 1, :])
    xg_ref[...] = _dot(jnp.where(hit, 1.0, 0.0).astype(BF16), n2.astype(BF16))


def _post_kernel(y_ref, h_ref, wout_ref, gffn_ref, wr_ref, br_ref,
                 h2_ref, xg_ref, gw_ref, pos_ref, cnt_ref, hrows, *, n_valid, n_exp, batch, batch_major):
    h = _time_major_rows(h_ref, hrows, batch) if batch_major else h_ref[...]
    h2 = h + _dot(y_ref[...], wout_ref[...])
    _route_block(h2, gffn_ref, wr_ref, br_ref, h2_ref, xg_ref, gw_ref, pos_ref, cnt_ref,
                 block=pl.program_id(0), n_valid=n_valid, n_exp=n_exp)


def _post(y, h, w, n_total, n_valid, n_exp, batch):
    batch_major = h.ndim == 3
    d = h.shape[-1]
    tb = TOKEN_BLOCK
    nb = n_total // tb
    const2 = lambda b: (0, 0)
    tok = pl.BlockSpec((SUBLANES, tb), lambda b: (0, b))
    h_spec = (pl.BlockSpec((batch, tb // batch, d), lambda b: (0, b, 0)) if batch_major
              else pl.BlockSpec((tb, d), lambda b: (b, 0)))
    return pl.pallas_call(
        functools.partial(_post_kernel, n_valid=n_valid, n_exp=n_exp, batch=batch, batch_major=batch_major),
        grid=(y.shape[0] // tb,),
        in_specs=[
            pl.BlockSpec((tb, d), lambda b: (b, 0)),
            h_spec,
            pl.BlockSpec((d, d), const2),
            pl.BlockSpec((1, d), const2),
            pl.BlockSpec((d, LANES), const2),
            pl.BlockSpec((n_exp, 1), const2),
        ],
        out_specs=[
            pl.BlockSpec((tb, d), lambda b: (b, 0)),
            pl.BlockSpec((_local_rows(n_exp), d), lambda b: (b, 0)),
            tok, tok,
            pl.BlockSpec((1, SUBLANES, n_exp), lambda b: (b, 0, 0)),
        ],
        out_shape=[
            jax.ShapeDtypeStruct((n_total, d), F32),
            jax.ShapeDtypeStruct((_grouped_rows(nb, n_exp), d), F32),
            jax.ShapeDtypeStruct((SUBLANES, n_total), F32),
            jax.ShapeDtypeStruct((SUBLANES, n_total), I32),
            jax.ShapeDtypeStruct((nb, SUBLANES, n_exp), F32),
        ],
        scratch_shapes=[pltpu.VMEM((d // LANES, tb, LANES), F32)],
        compiler_params=pltpu.CompilerParams(dimension_semantics=("arbitrary",), vmem_limit_bytes=VMEM_LIMIT),
        name="post_router",
    )(y, h, w["w_out"], w["g_ffn"], w["w_r"], w["b_r"])


def _max_sorted_rows(n_valid, nb, n_exp):
    worst = TOP_K * n_valid + (SEG_ALIGN - 1) * nb * n_exp + (EXPERT_TILE - SEG_ALIGN) * n_exp
    return _round_up(worst, EXPERT_TILE)


def _local_rows(n_exp):
    return _round_up(TOP_K * TOKEN_BLOCK + (SEG_ALIGN - 1) * n_exp, LANES)


def _dump_rows(n_exp):
    return n_exp * EXPERT_SUBTILE


def _grouped_rows(nb, n_exp):
    return nb * _local_rows(n_exp) + _dump_rows(n_exp)


def _select_columns(per_block_expert, expert_of):
    n_exp = per_block_expert.shape[1]
    one_hot = (expert_of[:, None] == jnp.arange(n_exp, dtype=I32)[None, :]).astype(F32)
    picked = jnp.dot(one_hot, per_block_expert.T.astype(F32), precision=lax.Precision.HIGHEST)
    return jnp.round(picked).astype(I32)


def _route_meta(cnt, max_tiles):
    nb, n_exp = cnt.shape
    mp = _local_rows(n_exp)
    p8 = (cnt + SEG_ALIGN - 1) // SEG_ALIGN * SEG_ALIGN
    off = jnp.cumsum(p8, axis=1) - p8
    tot = jnp.sum(p8, axis=1)
    seg = jnp.sum(p8, axis=0)
    reg = (seg + EXPERT_TILE - 1) // EXPERT_TILE * EXPERT_TILE
    reg_start = jnp.cumsum(reg) - reg
    tiles_end = jnp.cumsum(reg // EXPERT_TILE)
    tile_ids = jnp.arange(max_tiles, dtype=I32)
    tile_expert = jnp.minimum(jnp.sum((tiles_end[None, :] <= tile_ids[:, None]).astype(I32), axis=1), n_exp - 1)
    of_tile = tile_expert[:, None] == jnp.arange(n_exp, dtype=I32)[None, :]
    pick = lambda per_expert: jnp.sum(jnp.where(of_tile, per_expert[None, :], 0), axis=1)
    used = jnp.clip(pick(seg) - (tile_ids - pick(tiles_end - reg // EXPERT_TILE)) * EXPERT_TILE, 0, EXPERT_TILE)
    tile_rows = (used + EXPERT_SUBTILE - 1) // EXPERT_SUBTILE * EXPERT_SUBTILE

    pieces_per_tile = EXPERT_TILE // SEG_ALIGN
    piece_expert = jnp.repeat(tile_expert, pieces_per_tile)
    piece_tile = jnp.repeat(tile_ids, pieces_per_tile)
    row = jnp.arange(max_tiles * pieces_per_tile, dtype=I32) * SEG_ALIGN
    is_e = piece_expert[:, None] == jnp.arange(n_exp, dtype=I32)[None, :]
    of_expert = lambda per_expert: jnp.sum(jnp.where(is_e, per_expert[None, :], 0), axis=1)
    rin = row - of_expert(reg_start)
    seg_e = of_expert(seg)
    ends = _select_columns(jnp.cumsum(p8, axis=0), piece_expert)
    blk = jnp.minimum(jnp.sum((ends <= rin[:, None]).astype(I32), axis=1), nb - 1)
    is_b = blk[:, None] == jnp.arange(nb, dtype=I32)[None, :]
    of_block = lambda a: jnp.sum(jnp.where(is_b, a, 0), axis=1)
    seg_first = of_block(ends - _select_columns(p8, piece_expert))
    local = of_block(_select_columns(off, piece_expert)) + rin - seg_first
    pad_k = jnp.clip((rin - seg_e) // SEG_ALIGN, 0, EXPERT_SUBTILE // SEG_ALIGN - 1)
    real = (rin < seg_e) & (piece_tile < tiles_end[-1])
    piece = jnp.where(real, blk * mp + local, -1 - (piece_expert * (EXPERT_SUBTILE // SEG_ALIGN) + pad_k))
    return dict(piece=piece.astype(I32), tot=tot.astype(I32),
                tile_expert=tile_expert.astype(I32), n_tiles=tiles_end[-1:].astype(I32),
                tiles_end=tiles_end.astype(I32), tile_rows=tile_rows.astype(I32))


def _wait_rows(n_rows, min_rows, max_rows, src_rows, dst_rows, sem):
    size = min_rows
    while size <= max_rows:
        @pl.when((n_rows & size) != 0)
        def _(size=size):
            pltpu.make_async_copy(src_rows(size), dst_rows(size), sem).wait()
        size *= 2


def _expert_kernel(te_ref, nt_ref, tend_ref, rows_ref, piece_ref,
                   xg_hbm, w1_hbm, b1_ref, w2_hbm, b2_ref, yg_hbm,
                   xbuf, ybuf, w1s, w2s, w1b, w2b, xsem, ysem, sems, n_loaded, *, layer, zero_row, dump_row):
    i = pl.program_id(0)
    n_tiles = nt_ref[0]
    dff = w2b.shape[0]
    tm = xbuf.shape[1]
    pieces = tm // SEG_ALIGN
    slot = i % 2

    def weight_copies(expert, slot):
        return (pltpu.make_async_copy(w1_hbm.at[layer, expert], w1s.at[slot], sems.at[0, slot]),
                pltpu.make_async_copy(w2_hbm.at[layer, expert], w2s.at[slot], sems.at[1, slot]))

    def gather(tile, sl, exists=True):
        for j in range(pieces):
            p = piece_ref[tile * pieces + j]
            row = pl.multiple_of(jnp.where((p >= 0) & exists, p, zero_row), SEG_ALIGN)
            pltpu.make_async_copy(xg_hbm.at[pl.ds(row, SEG_ALIGN), :],
                                  xbuf.at[sl, pl.ds(j * SEG_ALIGN, SEG_ALIGN), :], xsem.at[sl]).start()

    def scatter(tile, sl, rows):
        for j in range(rows // SEG_ALIGN):
            p = piece_ref[tile * pieces + j]
            row = pl.multiple_of(jnp.where(p >= 0, p, dump_row + (-1 - p) * SEG_ALIGN), SEG_ALIGN)
            pltpu.make_async_copy(ybuf.at[sl, pl.ds(j * SEG_ALIGN, SEG_ALIGN), :],
                                  yg_hbm.at[pl.ds(row, SEG_ALIGN), :], ysem.at[sl]).start()

    def wait_gather(sl):
        pltpu.make_async_copy(xg_hbm.at[pl.ds(0, tm), :], xbuf.at[sl], xsem.at[sl]).wait()

    def wait_scatter(tile, sl):
        _wait_rows(rows_ref[tile], EXPERT_SUBTILE, tm, lambda n: ybuf.at[sl, pl.ds(0, n), :],
                   lambda n: yg_hbm.at[pl.ds(0, n), :], ysem.at[sl])

    @pl.when(i < n_tiles)
    def _():
        expert = te_ref[i]

        @pl.when(i == 0)
        def _():
            n_loaded[0] = 0
            for c in weight_copies(expert, 0):
                c.start()
            gather(0, 0)

        wait_gather(slot)

        @pl.when(i >= 2)
        def _():
            wait_scatter(jnp.maximum(i - 2, 0), slot)

        @pl.when((i == 0) | (expert != te_ref[jnp.maximum(i - 1, 0)]))
        def _():
            slot = n_loaded[0] % 2
            n_loaded[0] = n_loaded[0] + 1
            for c in weight_copies(expert, slot):
                c.wait()
            nxt = tend_ref[expert]

            @pl.when(nxt < n_tiles)
            def _():
                for c in weight_copies(te_ref[jnp.minimum(nxt, n_tiles - 1)], 1 - slot):
                    c.start()
            w1b[...] = w1s[slot].astype(BF16)
            w2b[...] = w2s[slot].astype(BF16)

        nxt_tile = jnp.minimum(i + 1, n_tiles - 1)
        for rows in range(EXPERT_SUBTILE, tm + 1, EXPERT_SUBTILE):
            @pl.when(rows_ref[i] == rows)
            def _(rows=rows):
                gather(nxt_tile, 1 - slot, i + 1 < n_tiles)
                hdn = _dot(xbuf[slot, 0:rows, :].astype(BF16), w1b[...]) + b1_ref[0, 0]
                gate = jnp.minimum(hdn[:, 0:dff], SWIGLU_LIMIT)
                up = jnp.clip(hdn[:, dff:2 * dff], -SWIGLU_LIMIT, SWIGLU_LIMIT)
                act = (up + 1.0) * gate * _sigmoid(SWIGLU_ALPHA * gate)
                ybuf[slot, 0:rows, :] = _dot(act.astype(BF16), w2b[...]) + b2_ref[0, 0]
                scatter(i, slot, rows)

        @pl.when(i == n_tiles - 1)
        def _():
            wait_gather(1 - slot)
            wait_scatter(i, slot)

            @pl.when(i >= 1)
            def _():
                wait_scatter(jnp.maximum(i - 1, 0), 1 - slot)


def _experts(xg, meta, layer, w1, b1, w2, b2, n_exp, max_tiles):
    rows_total, d = xg.shape
    dff2 = w1.shape[-1]
    dff = dff2 // 2
    tm = EXPERT_TILE
    mp = _local_rows(n_exp)
    last = lambda i, nt: jnp.minimum(i, jnp.maximum(nt[0] - 1, 0))
    exp_map = lambda i, te, nt, *_: (layer, te[last(i, nt)], 0, 0)
    return pl.pallas_call(
        functools.partial(_expert_kernel, layer=layer, zero_row=mp - SEG_ALIGN,
                          dump_row=rows_total - _dump_rows(n_exp)),
        grid_spec=pltpu.PrefetchScalarGridSpec(
            num_scalar_prefetch=5,
            grid=(max_tiles,),
            in_specs=[
                pl.BlockSpec(memory_space=pl.ANY),
                pl.BlockSpec(memory_space=pl.ANY),
                pl.BlockSpec((1, 1, 1, dff2), exp_map),
                pl.BlockSpec(memory_space=pl.ANY),
                pl.BlockSpec((1, 1, 1, d), exp_map),
            ],
            out_specs=pl.BlockSpec(memory_space=pl.ANY),
            scratch_shapes=[
                pltpu.VMEM((2, tm, d), F32), pltpu.VMEM((2, tm, d), F32),
                pltpu.VMEM((2, d, dff2), F32), pltpu.VMEM((2, dff, d), F32),
                pltpu.VMEM((d, dff2), BF16), pltpu.VMEM((dff, d), BF16),
                pltpu.SemaphoreType.DMA((2,)), pltpu.SemaphoreType.DMA((2,)),
                pltpu.SemaphoreType.DMA((2, 2)),
                pltpu.SMEM((1,), I32),
            ],
        ),
        out_shape=jax.ShapeDtypeStruct(xg.shape, F32),
        input_output_aliases={5: 0},
        compiler_params=pltpu.CompilerParams(dimension_semantics=("arbitrary",), vmem_limit_bytes=VMEM_LIMIT),
        name="moe_experts",
    )(meta["tile_expert"], meta["n_tiles"], meta["tiles_end"], meta["tile_rows"], meta["piece"],
      xg, w1, b1, w2, b2)


def _combine_kernel(h2_ref, yg_ref, gw_ref, pos_ref, pp_ref, ps_ref, wpg_ref, bpg_ref, wpp_ref, gple_ref, gfin_ref,
                    *rest, final, batch):
    if final:
        outp_ref, outs_ref, prows, orows = rest
    else:
        hn_ref, prows = rest
    b = pl.program_id(0)
    nb = pl.num_programs(0)
    tb = h2_ref.shape[0]
    mp = yg_ref.shape[0]

    pos = pos_ref[...]
    gw = gw_ref[...]
    jio = lax.broadcasted_iota(I32, (mp, tb), 0)
    cg = jnp.where(jio == pos[0:1, :], gw[0:1, :], 0.0)
    for k in range(1, TOP_K):
        cg = cg + jnp.where(jio == pos[k:k + 1, :], gw[k:k + 1, :], 0.0)
    moe = lax.dot_general(cg.astype(BF16), yg_ref[...].astype(BF16), (((0,), (0,)), ((), ())),
                          preferred_element_type=F32)
    h3 = h2_ref[...] + moe
    n3 = _rms(h3, gple_ref[...]).astype(BF16)
    gate = _sigmoid(_dot(n3, wpg_ref[...]) + bpg_ref[...])
    p_rows = jnp.where(b == nb - 1, ps_ref[0], _time_major_rows(pp_ref.at[0], prows, batch))
    h4 = h3 + gate * _dot(p_rows.astype(BF16), wpp_ref[...])
    if final:
        out = _rms(h4, gfin_ref[...])

        @pl.when(b < nb - 1)
        def _():
            _batch_major_store(out, orows, outp_ref, batch)

        @pl.when(b == nb - 1)
        def _():
            outs_ref[...] = out
    else:
        hn_ref[...] = h4


def _combine(h2, yg, gw, pos, p_prompt, p_sample, layer, w, g_final, n_exp, final):
    n_total, d = h2.shape
    _, batch, seq, ple = p_prompt.shape
    tb = TOKEN_BLOCK
    nb = n_total // tb
    steps = tb // batch
    const2 = lambda b: (0, 0)
    tok = pl.BlockSpec((SUBLANES, tb), lambda b: (0, b))
    row = pl.BlockSpec((tb, d), lambda b: (b, 0))
    prompt_blk = lambda b: jnp.minimum(b, nb - 2)
    if final:
        out_specs = [pl.BlockSpec((batch, steps, d), lambda b: (0, prompt_blk(b), 0)),
                     pl.BlockSpec((tb, d), const2)]
        out_shape = [jax.ShapeDtypeStruct((batch, seq, d), F32), jax.ShapeDtypeStruct((tb, d), F32)]
        extra_scratch = [pltpu.VMEM((d // LANES, tb, LANES), F32)]
    else:
        out_specs = [row]
        out_shape = [jax.ShapeDtypeStruct((n_total, d), F32)]
        extra_scratch = []
    return pl.pallas_call(
        functools.partial(_combine_kernel, final=final, batch=batch),
        grid=(nb,),
        in_specs=[
            row,
            pl.BlockSpec((_local_rows(n_exp), d), lambda b: (b, 0)),
            tok, tok,
            pl.BlockSpec((1, batch, steps, ple), lambda b: (layer, 0, prompt_blk(b), 0)),
            pl.BlockSpec((1, tb, ple), lambda b: (layer, 0, 0)),
            pl.BlockSpec((d, d), const2),
            pl.BlockSpec((1, d), const2),
            pl.BlockSpec((ple, d), const2),
            pl.BlockSpec((1, d), const2),
            pl.BlockSpec((1, d), const2),
        ],
        out_specs=out_specs,
        scratch_shapes=[pltpu.VMEM((ple // LANES, tb, LANES), F32)] + extra_scratch,
        out_shape=out_shape,
        compiler_params=pltpu.CompilerParams(dimension_semantics=("arbitrary",), vmem_limit_bytes=VMEM_LIMIT),
        name="moe_combine_ple",
    )(h2, yg, gw, pos, p_prompt, p_sample, w["w_pg"], w["b_pg"], w["w_pp"], w["g_ple"], g_final)


def _layer_weights(l, batch, g_mix, w_in, g_v, w_s, b_s, conv_w, conv_b, w_a, b_a, w_x, b_x, lam,
                   g_oa, g_ob, w_out, g_ffn, w_r, b_r, g_ple, w_pg, b_pg, w_pp):
    d = w_in.shape[1]
    dh = d // 2
    gd = dh // GMLP_GROUPS
    n_exp = w_r.shape[2]
    row = lambda a: a[l].reshape(1, -1).astype(F32)
    wt = jnp.where(jnp.tril(jnp.ones((CHUNK, CHUNK), bool)), w_s[l], 0.0).astype(BF16)
    hr = HALF * batch
    row_t = jnp.arange(hr, dtype=I32) // batch
    row_b = jnp.arange(hr, dtype=I32) % batch
    expand_t = (row_t[:, None] == jnp.arange(HALF, dtype=I32)[None, :]).astype(BF16)
    same_b = row_b[:, None] == row_b[None, :]

    def kron_block(blk):
        rows = jnp.einsum("it,gts->gis", expand_t, blk, preferred_element_type=F32).astype(BF16)
        full = jnp.einsum("gis,js->gij", rows, expand_t, preferred_element_type=F32)
        return jnp.where(same_b, full, 0.0).astype(BF16)

    kd = jnp.stack([kron_block(wt[:, :HALF, :HALF]), kron_block(wt[:, HALF:, HALF:])])
    k10 = kron_block(wt[:, HALF:, :HALF])
    bs_rows = jnp.repeat(jnp.repeat(b_s[l].T.astype(F32), gd, axis=1), batch, axis=0).reshape(2, hr, dh)
    hd = dh // LRU_HEADS
    head_of = jnp.arange(dh, dtype=I32) // hd
    same_head = head_of[:, None] == head_of[None, :]
    block_diag = lambda w: jnp.where(same_head, jnp.tile(w.astype(F32).reshape(dh, hd), (1, LRU_HEADS)), 0.0)
    wa_f32, wx_f32 = block_diag(w_a[l]), block_diag(w_x[l])
    w_r_pad = jnp.zeros((d, LANES), F32).at[:, :n_exp].set(w_r[l])
    return dict(
        g_mix=row(g_mix), w_in=w_in[l].astype(BF16), w_in_f32=w_in[l].astype(F32), g_v=row(g_v),
        kd=kd, k10=k10, bs_rows=bs_rows,
        w0_row=jnp.repeat(w_s[l][:, 0, 0], gd).reshape(1, dh).astype(F32),
        b0_row=jnp.repeat(b_s[l][:, 0], gd).reshape(1, dh).astype(F32),
        conv_w=conv_w[l].astype(F32), conv_b=row(conv_b),
        wa=wa_f32.astype(BF16), wa_f32=wa_f32, b_a=row(b_a), wx=wx_f32.astype(BF16), wx_f32=wx_f32,
        b_x=row(b_x), lam=row(lam),
        g_oa=row(g_oa), g_ob=row(g_ob), w_out=w_out[l].astype(BF16), w_out_f32=w_out[l].astype(F32),
        g_ffn=row(g_ffn),
        w_r=w_r_pad, b_r=b_r[l].reshape(n_exp, 1).astype(F32),
        g_ple=row(g_ple), w_pg=w_pg[l].astype(BF16), b_pg=row(b_pg), w_pp=w_pp[l].astype(BF16))


def kernel(x_prompt, x_sample, state_conv, state_lru, p_prompt, p_sample, g_mix, w_in, g_v, w_s, b_s, conv_w, conv_b, w_a, b_a, w_x, b_x, lam, g_oa, g_ob, w_out, g_ffn, w_r, b_r, w1, b1, w2, b2, g_ple, w_pg, b_pg, w_pp, g_final):
    batch, seq, d = x_prompt.shape
    dec_batch, dec_seq, _ = x_sample.shape
    depth = w_in.shape[0]
    n_exp = w_r.shape[2]
    dh = d // 2
    ple = p_prompt.shape[-1]
    assert dec_seq == 1 and seq % CHUNK == 0 and batch % SUBLANES == 0
    assert dh == GMLP_GROUPS * LANES and dh % (LRU_HEADS * (LANES // 2)) == 0
    n_prompt = seq * batch
    assert n_prompt % TOKEN_BLOCK == 0
    n_valid = n_prompt + dec_batch
    n_total = _round_up(n_valid, TOKEN_BLOCK)
    s_rows = n_total - n_prompt
    assert s_rows == TOKEN_BLOCK
    nb = n_total // TOKEN_BLOCK
    max_tiles = _max_sorted_rows(n_valid, nb, n_exp) // EXPERT_TILE
    pad_rows = ((0, 0), (0, s_rows - dec_batch), (0, 0))

    h, h_sample, h_sample_block = x_prompt, jnp.pad(x_sample.reshape(1, dec_batch, d), pad_rows)[0], 0
    p_samp = jnp.pad(p_sample.reshape(depth, dec_batch, ple), pad_rows)
    sconv = jnp.pad(state_conv.transpose(0, 2, 1, 3), ((0, 0),) + pad_rows)
    slru = jnp.pad(state_lru, pad_rows)
    g_fin = g_final.reshape(1, d).astype(F32)
    b1_rows = b1.reshape(depth, n_exp, 1, -1)
    b2_rows = b2.reshape(depth, n_exp, 1, -1)

    v_p, conv_p, lru_p, v_s, conv_s, lru_s = [], [], [], [], [], []
    for l in range(depth):
        w = _layer_weights(l, batch, g_mix, w_in, g_v, w_s, b_s, conv_w, conv_b, w_a, b_a, w_x, b_x, lam,
                           g_oa, g_ob, w_out, g_ffn, w_r, b_r, g_ple, w_pg, b_pg, w_pp)
        y, vp, cp, lp = _mix_prompt(h, n_prompt, batch, w)
        post_out = _post(y, h, w, n_total, n_valid, n_exp, batch)
        h2, xg, gw, pos, cnt, vs, cs, ls = _sample_front(
            h_sample, h_sample_block, post_out, nb - 1, w, sconv[l], slru[l], n_valid, n_exp)
        meta = _route_meta(cnt[:, 0, :].astype(I32), max_tiles)
        yg = _experts(xg, meta, l, w1, b1_rows, w2, b2_rows, n_exp, max_tiles)
        final = l == depth - 1
        res = _combine(h2, yg, gw, pos, p_prompt, p_samp, l, w, g_fin, n_exp, final)
        if final:
            y_prompt, out_sample = res
        else:
            h = h_sample = res[0]
            h_sample_block = nb - 1
        v_p.append(vp.reshape(CHUNK, batch, dh).transpose(1, 0, 2))
        conv_p.append(cp.reshape(CONV_W - 1, batch, dh).transpose(1, 0, 2))
        lru_p.append(lp)
        v_s.append(vs[:dec_batch].reshape(dec_batch, 1, dh))
        conv_s.append(cs[:, :dec_batch].transpose(1, 0, 2))
        lru_s.append(ls[:dec_batch])

    y_sample = out_sample[:dec_batch].reshape(dec_batch, 1, d)
    return (y_prompt, y_sample, jnp.stack(v_p), jnp.stack(conv_p), jnp.stack(lru_p),
            jnp.stack(v_s), jnp.stack(conv_s), jnp.stack(lru_s))
```

```python
import functools

import jax
import jax.numpy as jnp
from jax import lax
from jax.experimental import pallas as pl
from jax.experimental.pallas import tpu as pltpu

F32 = jnp.float32
BF16 = jnp.bfloat16
I32 = jnp.int32

CHUNK = 128
HALF = CHUNK // 2
GMLP_GROUPS = 4
LRU_HEADS = 8
CONV_W = 4
LRU_C = 8.0
TOP_K = 4
SWIGLU_LIMIT = 7.0
SWIGLU_ALPHA = 1.702
EPS = 1e-6

LANES = 128
SUBLANES = 8
TOKEN_BLOCK = 256
EXPERT_TILE = 512
EXPERT_SUBTILE = 128
SEG_ALIGN = SUBLANES
VMEM_LIMIT = 56 * 1024 * 1024


def _round_up(x, m):
    return (x + m - 1) // m * m


def _dot(a, b):
    return jnp.dot(a, b, preferred_element_type=F32)


def _gelu(x):
    return 0.5 * x * (1.0 + jnp.tanh(0.7978845608028654 * (x + 0.044715 * (x * x * x))))


def _sigmoid(x):
    return 1.0 / (1.0 + jnp.exp(-x))


def _softplus(x):
    return jnp.maximum(x, 0.0) + jnp.log1p(jnp.exp(-jnp.abs(x)))


def _rms(x, g):
    ms = jnp.mean(x * x, axis=-1, keepdims=True)
    return x * lax.rsqrt(ms + EPS) * g


def _group_norm_128(x, g):
    outs = []
    for j in range(x.shape[1] // LANES):
        blk = x[:, j * LANES:(j + 1) * LANES]
        ms = jnp.mean(blk * blk, axis=-1, keepdims=True)
        outs.append(blk * lax.rsqrt(ms + EPS))
    return jnp.concatenate(outs, axis=1) * g


def _group_norm_64(x, g):
    half = LANES // 2
    lo_mask = lax.broadcasted_iota(I32, (1, LANES), 1) < half
    outs = []
    for j in range(x.shape[1] // LANES):
        blk = x[:, j * LANES:(j + 1) * LANES]
        sq = blk * blk
        lo = jnp.sum(jnp.where(lo_mask, sq, 0.0), axis=-1, keepdims=True)
        hi = jnp.sum(jnp.where(lo_mask, 0.0, sq), axis=-1, keepdims=True)
        ms = jnp.where(lo_mask, lo, hi) * (1.0 / half)
        outs.append(blk * lax.rsqrt(ms + EPS))
    return jnp.concatenate(outs, axis=1) * g


def _time_major_rows(src_ref, slabs_ref, batch):
    steps = src_ref.shape[1]
    n_slab = slabs_ref.shape[0]
    for b in range(batch):
        blk = src_ref[b]
        for j in range(n_slab):
            slabs_ref[j, pl.ds(b, steps, stride=batch), :] = blk[:, j * LANES:(j + 1) * LANES]
    return jnp.concatenate([slabs_ref[j] for j in range(n_slab)], axis=1)


def _batch_major_store(rows, slabs_ref, dst_ref, batch):
    steps = dst_ref.shape[1]
    n_slab = slabs_ref.shape[0]
    for j in range(n_slab):
        slabs_ref[j] = rows[:, j * LANES:(j + 1) * LANES]
    for b in range(batch):
        dst_ref[b] = jnp.concatenate(
            [slabs_ref[j, pl.ds(b, steps, stride=batch), :] for j in range(n_slab)], axis=1)


def _split_bf16(x):
    hi = x.astype(BF16)
    lo = (x - hi.astype(F32)).astype(BF16)
    return hi, lo


def _dot_bf16(x, w):
    return _dot(x.astype(BF16), w)


def _dot_3pass(x, w):
    x_hi, x_lo = _split_bf16(x)
    w_hi, w_lo = _split_bf16(w)
    return _dot(x_hi, w_hi) + (_dot(x_lo, w_hi) + _dot(x_hi, w_lo))


def _lru_gates(xc, wa_ref, ba_ref, wx_ref, bx_ref, lam_ref, mm):
    r = _sigmoid(mm(xc, wa_ref[...]) + ba_ref[...])
    i = _sigmoid(mm(xc, wx_ref[...]) + bx_ref[...])
    log_a = (-LRU_C * r) * _softplus(-lam_ref[...])
    a = jnp.exp(log_a)
    mult = jnp.sqrt(-jnp.tanh(log_a) * (a * a + 1.0))
    return a, mult, i


def _mix_prompt_kernel(h_ref, gmix_ref, win_ref, gv_ref, kd_ref, k10_ref, bs_ref, cw_ref, cb_ref,
                       wa_ref, ba_ref, wx_ref, bx_ref, lam_ref, goa_ref, gob_ref,
                       y_ref, v_ref, conv_ref, lru_ref,
                       hstate, xpad, vprev, a_s, b_s, hs_s, hrows, z_even, z_odd, *, batch, batch_major):
    c = pl.program_id(0)
    rows = y_ref.shape[0]
    dh = gv_ref.shape[1]
    tail = (CONV_W - 1) * batch

    @pl.when(c == 0)
    def _():
        z_odd[...] = jnp.zeros_like(z_odd)
        vprev[...] = jnp.zeros_like(vprev)

    @pl.when(c <= 1)
    def _():
        hstate[...] = jnp.zeros_like(hstate)
        xpad[0:tail, :] = jnp.zeros((tail, dh), F32)

    def tile_step(z_in, z_out, par):
        h = _time_major_rows(h_ref, hrows, batch) if batch_major else h_ref[...]
        z_out[...] = _dot(_rms(h, gmix_ref[...]).astype(BF16), win_ref[...])

        if par == 0:
            vprev[...] = jnp.zeros_like(vprev)

        vn = _group_norm_128(_gelu(z_in[:, dh:2 * dh]), gv_ref[...])
        vb = vn.astype(BF16)
        s_parts = []
        for g in range(GMLP_GROUPS):
            sl = slice(g * LANES, (g + 1) * LANES)
            s_parts.append(_dot(kd_ref[par, g], vb[:, sl]) + _dot(k10_ref[g], vprev[:, sl]))
        s = jnp.concatenate(s_parts, axis=1) + bs_ref[par]
        vprev[...] = vb
        v_ref[...] = vn
        y_ref[:, 0:dh] = _group_norm_128(_gelu(z_in[:, 0:dh]) * s, goa_ref[...]).astype(BF16)

        xpad[tail:tail + rows, :] = z_in[:, 2 * dh:3 * dh]
        xc = cb_ref[...] + cw_ref[0:1, :] * xpad[0:rows, :]
        for k in range(1, CONV_W):
            xc = xc + cw_ref[k:k + 1, :] * xpad[k * batch:k * batch + rows, :]
        new_tail = xpad[rows:rows + tail, :]
        xpad[0:tail, :] = new_tail

        a, mult, i = _lru_gates(xc, wa_ref, ba_ref, wx_ref, bx_ref, lam_ref, _dot_bf16)
        row = lax.broadcasted_iota(I32, (rows, 1), 0)
        mult = jnp.where((c == 1) & (row < batch), 1.0, mult)
        a_s[...] = a
        b_s[...] = mult * i * xc

        def step(t, h):
            r0 = t * batch
            h = a_s[r0:r0 + batch, :] * h + b_s[r0:r0 + batch, :]
            hs_s[r0:r0 + batch, :] = h
            return h

        h_last = hstate[...]
        for t in range(rows // batch):
            h_last = step(t, h_last)
        hstate[...] = h_last

        y_ref[:, dh:2 * dh] = _group_norm_64(hs_s[...] * _gelu(z_in[:, 3 * dh:4 * dh]), gob_ref[...]).astype(BF16)
        conv_ref[...] = new_tail
        lru_ref[...] = h_last

    @pl.when(c % 2 == 0)
    def _():
        tile_step(z_odd, z_even, 1)

    @pl.when(c % 2 == 1)
    def _():
        tile_step(z_even, z_odd, 0)


def _mix_prompt(h, n_rows, batch, w):
    batch_major = h.ndim == 3
    d = h.shape[-1]
    dh = d // 2
    rows = HALF * batch
    n_steps = n_rows // rows
    tail = (CONV_W - 1) * batch
    const2 = lambda c: (0, 0)
    const3 = lambda c: (0, 0, 0)
    const4 = lambda c: (0, 0, 0, 0)
    proj_tile = lambda c: jnp.minimum(c, n_steps - 1)
    mix_tile = lambda c: jnp.maximum(c - 1, 0)
    h_spec = (pl.BlockSpec((batch, HALF, d), lambda c: (0, proj_tile(c), 0)) if batch_major
              else pl.BlockSpec((rows, d), lambda c: (proj_tile(c), 0)))
    return pl.pallas_call(
        functools.partial(_mix_prompt_kernel, batch=batch, batch_major=batch_major),
        grid=(n_steps + 1,),
        in_specs=[
            h_spec,
            pl.BlockSpec((1, d), const2),
            pl.BlockSpec((d, 2 * d), const2),
            pl.BlockSpec((1, dh), const2),
            pl.BlockSpec((2, GMLP_GROUPS, rows, rows), const4),
            pl.BlockSpec((GMLP_GROUPS, rows, rows), const3),
            pl.BlockSpec((2, rows, dh), const3),
            pl.BlockSpec((CONV_W, dh), const2),
            pl.BlockSpec((1, dh), const2),
            pl.BlockSpec((dh, dh), const2),
            pl.BlockSpec((1, dh), const2),
            pl.BlockSpec((dh, dh), const2),
            pl.BlockSpec((1, dh), const2),
            pl.BlockSpec((1, dh), const2),
            pl.BlockSpec((1, dh), const2),
            pl.BlockSpec((1, dh), const2),
        ],
        out_specs=[
            pl.BlockSpec((rows, d), lambda c: (mix_tile(c), 0)),
            pl.BlockSpec((rows, dh), lambda c: (jnp.maximum(mix_tile(c) - (n_steps - 2), 0), 0)),
            pl.BlockSpec((tail, dh), const2),
            pl.BlockSpec((batch, dh), const2),
        ],
        out_shape=[
            jax.ShapeDtypeStruct((n_rows, d), BF16),
            jax.ShapeDtypeStruct((2 * rows, dh), F32),
            jax.ShapeDtypeStruct((tail, dh), F32),
            jax.ShapeDtypeStruct((batch, dh), F32),
        ],
        scratch_shapes=[
            pltpu.VMEM((batch, dh), F32),
            pltpu.VMEM((tail + rows, dh), F32),
            pltpu.VMEM((rows, dh), BF16),
            pltpu.VMEM((rows, dh), F32),
            pltpu.VMEM((rows, dh), F32),
            pltpu.VMEM((rows, dh), F32),
            pltpu.VMEM((d // LANES, rows, LANES), F32),
            pltpu.VMEM((rows, 2 * d), F32),
            pltpu.VMEM((rows, 2 * d), F32),
        ],
        compiler_params=pltpu.CompilerParams(dimension_semantics=("arbitrary",), vmem_limit_bytes=VMEM_LIMIT),
        name="mix_prompt",
    )(h, w["g_mix"], w["w_in"], w["g_v"], w["kd"], w["k10"], w["bs_rows"], w["conv_w"], w["conv_b"],
      w["wa"], w["b_a"], w["wx"], w["b_x"], w["lam"], w["g_oa"], w["g_ob"])


def _sample_front_kernel(h_ref, h2_any, xg_any, gw_any, pos_any, cnt_any,
                         gmix_ref, win_ref, gv_ref, w0_ref, b0_ref, cw_ref, cb_ref,
                         wa_ref, ba_ref, wx_ref, bx_ref, lam_ref, goa_ref, gob_ref, sconv_ref, slru_ref,
                         wout_ref, gffn_ref, wr_ref, br_ref,
                         h2_ref, xg_ref, gw_ref, pos_ref, cnt_ref, v_ref, conv_ref, lru_ref,
                         *, block, n_valid, n_exp):
    del h2_any, xg_any, gw_any, pos_any, cnt_any
    dh = gv_ref.shape[1]
    h = h_ref[...]
    n = _rms(h, gmix_ref[...])
    proj = lambda lo: _dot_3pass(n, win_ref[:, lo:lo + dh])

    vn = _group_norm_128(_gelu(proj(dh)), gv_ref[...])
    v_ref[...] = vn
    s = vn * w0_ref[...] + b0_ref[...]
    ya = _group_norm_128(_gelu(proj(0)) * s, goa_ref[...])

    xb = proj(2 * dh)
    xc = cb_ref[...] + cw_ref[CONV_W - 1:CONV_W, :] * xb
    for k in range(CONV_W - 1):
        xc = xc + cw_ref[k:k + 1, :] * sconv_ref[k]
    for k in range(CONV_W - 2):
        conv_ref[k] = sconv_ref[k + 1]
    conv_ref[CONV_W - 2] = xb

    a, mult, i = _lru_gates(xc, wa_ref, ba_ref, wx_ref, bx_ref, lam_ref, _dot_3pass)
    h_new = a * slru_ref[...] + mult * i * xc
    lru_ref[...] = h_new
    yb = _group_norm_64(h_new * _gelu(proj(3 * dh)), gob_ref[...])

    h2 = h + _dot_3pass(ya, wout_ref[0:dh, :]) + _dot_3pass(yb, wout_ref[dh:2 * dh, :])
    _route_block(h2, gffn_ref, wr_ref, br_ref, h2_ref, xg_ref, gw_ref, pos_ref, cnt_ref,
                 block=block, n_valid=n_valid, n_exp=n_exp)


def _sample_front(h, h_block, post_out, block, w, sconv, slru, n_valid, n_exp):
    d = h.shape[1]
    dh = d // 2
    rows = TOKEN_BLOCK
    const2 = lambda i: (0, 0)
    const3 = lambda i: (0, 0, 0)
    vec = pl.BlockSpec((1, dh), const2)
    anyspace = pl.BlockSpec(memory_space=pl.ANY)
    blk_row = pl.BlockSpec((rows, d), lambda i: (block, 0))
    blk_tok = pl.BlockSpec((SUBLANES, rows), lambda i: (0, block))
    return pl.pallas_call(
        functools.partial(_sample_front_kernel, block=block, n_valid=n_valid, n_exp=n_exp),
        grid=(1,),
        in_specs=[
            pl.BlockSpec((rows, d), lambda i: (h_block, 0)),
            anyspace, anyspace, anyspace, anyspace, anyspace,
            pl.BlockSpec((1, d), const2),
            pl.BlockSpec((d, 2 * d), const2),
            vec, vec, vec,
            pl.BlockSpec((CONV_W, dh), const2),
            vec,
            pl.BlockSpec((dh, dh), const2),
            vec,
            pl.BlockSpec((dh, dh), const2),
            vec, vec, vec, vec,
            pl.BlockSpec((CONV_W - 1, rows, dh), const3),
            pl.BlockSpec((rows, dh), const2),
            pl.BlockSpec((d, d), const2),
            pl.BlockSpec((1, d), const2),
            pl.BlockSpec((d, LANES), const2),
            pl.BlockSpec((n_exp, 1), const2),
        ],
        out_specs=[
            blk_row, pl.BlockSpec((_local_rows(n_exp), d), lambda i: (block, 0)), blk_tok, blk_tok,
            pl.BlockSpec((1, SUBLANES, n_exp), lambda i: (block, 0, 0)),
            pl.BlockSpec((rows, dh), const2),
            pl.BlockSpec((CONV_W - 1, rows, dh), const3),
            pl.BlockSpec((rows, dh), const2),
        ],
        out_shape=[jax.ShapeDtypeStruct(a.shape, a.dtype) for a in post_out] + [
            jax.ShapeDtypeStruct((rows, dh), F32),
            jax.ShapeDtypeStruct((CONV_W - 1, rows, dh), F32),
            jax.ShapeDtypeStruct((rows, dh), F32),
        ],
        input_output_aliases={1 + k: k for k in range(len(post_out))},
        compiler_params=pltpu.CompilerParams(dimension_semantics=("arbitrary",), vmem_limit_bytes=VMEM_LIMIT),
        name="sample_front",
    )(h, *post_out, w["g_mix"], w["w_in_f32"], w["g_v"], w["w0_row"], w["b0_row"], w["conv_w"], w["conv_b"],
      w["wa_f32"], w["b_a"], w["wx_f32"], w["b_x"], w["lam"], w["g_oa"], w["g_ob"], sconv, slru,
      w["w_out_f32"], w["g_ffn"], w["w_r"], w["b_r"])


def _route_block(h2, gffn_ref, wr_ref, br_ref, h2_ref, xg_ref, gw_ref, pos_ref, cnt_ref,
                 *, block, n_valid, n_exp):
    tb = h2.shape[0]
    h2_ref[...] = h2
    n2 = _rms(h2, gffn_ref[...])

    logits = _dot_3pass(n2, wr_ref[...])
    lt = logits.T[0:n_exp, :] + br_ref[...]

    eio = lax.broadcasted_iota(I32, (n_exp, tb), 0).astype(F32)
    col = block * tb + lax.broadcasted_iota(I32, (1, tb), 1)
    valid = col < n_valid
    vals, hots = [], []
    for _ in range(TOP_K):
        m = jnp.max(lt, axis=0, keepdims=True)
        sel = jnp.min(jnp.where(lt == m, eio, float(n_exp)), axis=0, keepdims=True)
        hot = eio == sel
        lt = jnp.where(hot, -jnp.inf, lt)
        vals.append(m)
        hots.append(hot)
    exps = [jnp.exp(v - vals[0]) for v in vals]
    den = exps[0]
    for e in exps[1:]:
        den = den + e

    occ = jnp.zeros((n_exp, tb), F32)
    for hot in hots:
        occ = occ + jnp.where(hot & valid, 1.0, 0.0)
    occ_b = occ.astype(BF16)
    before = (lax.broadcasted_iota(I32, (tb, tb), 0) < lax.broadcasted_iota(I32, (tb, tb), 1))
    ranks_e = _dot(occ_b, jnp.where(before, 1.0, 0.0).astype(BF16))
    cnt_ref[0] = lax.dot_general(jnp.ones((SUBLANES, tb), BF16), occ_b, (((1,), (1,)), ((), ())),
                                 preferred_element_type=F32)
    cnt_col = jnp.sum(occ, axis=1, keepdims=True)
    units = jnp.floor((cnt_col + (SEG_ALIGN - 1.0)) * (1.0 / SEG_ALIGN))
    earlier = (lax.broadcasted_iota(I32, (n_exp, n_exp), 1) < lax.broadcasted_iota(I32, (n_exp, n_exp), 0))
    seg_off = _dot(jnp.where(earlier, 1.0, 0.0).astype(BF16),
                   jnp.broadcast_to(units, (n_exp, LANES)).astype(BF16))[:, 0:1] * float(SEG_ALIGN)
    rows_e = ranks_e + seg_off

    rio = lax.broadcasted_iota(I32, (SUBLANES, tb), 0)
    gw_out = jnp.zeros((SUBLANES, tb), F32)
    pos_out = jnp.full((SUBLANES, tb), -1, I32)
    for k in range(TOP_K):
        pos_k = jnp.sum(jnp.where(hots[k], rows_e, 0.0), axis=0, keepdims=True).astype(I32)
        gw_out = jnp.where(rio == k, exps[k] / den, gw_out)
        pos_out = jnp.where(rio == k, jnp.where(valid, pos_k, -1), pos_out)
    gw_ref[...] = gw_out
    pos_ref[...] = pos_out

    jio = lax.broadcasted_iota(I32, (xg_ref.shape[0], tb), 0)
    hit = jio == pos_out[0:1, :]
    for k in range(1, TOP_K):
        hit = hit | (jio == pos_out[k:k + 1, :])
    xg_ref[...] = _dot(jnp.where(hit, 1.0, 0.0).astype(BF16), n2.astype(BF16)).astype(BF16)


def _post_kernel(y_ref, h_ref, wout_ref, gffn_ref, wr_ref, br_ref,
                 h2_ref, xg_ref, gw_ref, pos_ref, cnt_ref, hrows, *, n_valid, n_exp, batch, batch_major):
    h = _time_major_rows(h_ref, hrows, batch) if batch_major else h_ref[...]
    h2 = h + _dot(y_ref[...], wout_ref[...])
    _route_block(h2, gffn_ref, wr_ref, br_ref, h2_ref, xg_ref, gw_ref, pos_ref, cnt_ref,
                 block=pl.program_id(0), n_valid=n_valid, n_exp=n_exp)


def _post(y, h, w, n_total, n_valid, n_exp, batch):
    batch_major = h.ndim == 3
    d = h.shape[-1]
    tb = TOKEN_BLOCK
    nb = n_total // tb
    const2 = lambda b: (0, 0)
    tok = pl.BlockSpec((SUBLANES, tb), lambda b: (0, b))
    h_spec = (pl.BlockSpec((batch, tb // batch, d), lambda b: (0, b, 0)) if batch_major
              else pl.BlockSpec((tb, d), lambda b: (b, 0)))
    return pl.pallas_call(
        functools.partial(_post_kernel, n_valid=n_valid, n_exp=n_exp, batch=batch, batch_major=batch_major),
        grid=(y.shape[0] // tb,),
        in_specs=[
            pl.BlockSpec((tb, d), lambda b: (b, 0)),
            h_spec,
            pl.BlockSpec((d, d), const2),
            pl.BlockSpec((1, d), const2),
            pl.BlockSpec((d, LANES), const2),
            pl.BlockSpec((n_exp, 1), const2),
        ],
        out_specs=[
            pl.BlockSpec((tb, d), lambda b: (b, 0)),
            pl.BlockSpec((_local_rows(n_exp), d), lambda b: (b, 0)),
            tok, tok,
            pl.BlockSpec((1, SUBLANES, n_exp), lambda b: (b, 0, 0)),
        ],
        out_shape=[
            jax.ShapeDtypeStruct((n_total, d), F32),
            jax.ShapeDtypeStruct((_grouped_rows(nb, n_exp), d), BF16),
            jax.ShapeDtypeStruct((SUBLANES, n_total), F32),
            jax.ShapeDtypeStruct((SUBLANES, n_total), I32),
            jax.ShapeDtypeStruct((nb, SUBLANES, n_exp), F32),
        ],
        scratch_shapes=[pltpu.VMEM((d // LANES, tb, LANES), F32)],
        compiler_params=pltpu.CompilerParams(dimension_semantics=("arbitrary",), vmem_limit_bytes=VMEM_LIMIT),
        name="post_router",
    )(y, h, w["w_out"], w["g_ffn"], w["w_r"], w["b_r"])


def _max_sorted_rows(n_valid, nb, n_exp):
    worst = TOP_K * n_valid + (SEG_ALIGN - 1) * nb * n_exp + (EXPERT_TILE - SEG_ALIGN) * n_exp
    return _round_up(worst, EXPERT_TILE)


def _local_rows(n_exp):
    return _round_up(TOP_K * TOKEN_BLOCK + (SEG_ALIGN - 1) * n_exp, LANES)


def _dump_rows(n_exp):
    return n_exp * EXPERT_SUBTILE


def _grouped_rows(nb, n_exp):
    return nb * _local_rows(n_exp) + _dump_rows(n_exp)


def _select_columns(per_block_expert, expert_of):
    n_exp = per_block_expert.shape[1]
    one_hot = (expert_of[:, None] == jnp.arange(n_exp, dtype=I32)[None, :]).astype(F32)
    picked = jnp.dot(one_hot, per_block_expert.T.astype(F32), precision=lax.Precision.HIGHEST)
    return jnp.round(picked).astype(I32)


def _route_meta(cnt, max_tiles):
    nb, n_exp = cnt.shape
    mp = _local_rows(n_exp)
    p8 = (cnt + SEG_ALIGN - 1) // SEG_ALIGN * SEG_ALIGN
    off = jnp.cumsum(p8, axis=1) - p8
    tot = jnp.sum(p8, axis=1)
    seg = jnp.sum(p8, axis=0)
    reg = (seg + EXPERT_TILE - 1) // EXPERT_TILE * EXPERT_TILE
    reg_start = jnp.cumsum(reg) - reg
    tiles_end = jnp.cumsum(reg // EXPERT_TILE)
    tile_ids = jnp.arange(max_tiles, dtype=I32)
    tile_expert = jnp.minimum(jnp.sum((tiles_end[None, :] <= tile_ids[:, None]).astype(I32), axis=1), n_exp - 1)
    of_tile = tile_expert[:, None] == jnp.arange(n_exp, dtype=I32)[None, :]
    pick = lambda per_expert: jnp.sum(jnp.where(of_tile, per_expert[None, :], 0), axis=1)
    used = jnp.clip(pick(seg) - (tile_ids - pick(tiles_end - reg // EXPERT_TILE)) * EXPERT_TILE, 0, EXPERT_TILE)
    tile_rows = (used + EXPERT_SUBTILE - 1) // EXPERT_SUBTILE * EXPERT_SUBTILE

    pieces_per_tile = EXPERT_TILE // SEG_ALIGN
    piece_expert = jnp.repeat(tile_expert, pieces_per_tile)
    piece_tile = jnp.repeat(tile_ids, pieces_per_tile)
    row = jnp.arange(max_tiles * pieces_per_tile, dtype=I32) * SEG_ALIGN
    is_e = piece_expert[:, None] == jnp.arange(n_exp, dtype=I32)[None, :]
    of_expert = lambda per_expert: jnp.sum(jnp.where(is_e, per_expert[None, :], 0), axis=1)
    rin = row - of_expert(reg_start)
    seg_e = of_expert(seg)
    ends = _select_columns(jnp.cumsum(p8, axis=0), piece_expert)
    blk = jnp.minimum(jnp.sum((ends <= rin[:, None]).astype(I32), axis=1), nb - 1)
    is_b = blk[:, None] == jnp.arange(nb, dtype=I32)[None, :]
    of_block = lambda a: jnp.sum(jnp.where(is_b, a, 0), axis=1)
    seg_first = of_block(ends - _select_columns(p8, piece_expert))
    local = of_block(_select_columns(off, piece_expert)) + rin - seg_first
    pad_k = jnp.clip((rin - seg_e) // SEG_ALIGN, 0, EXPERT_SUBTILE // SEG_ALIGN - 1)
    real = (rin < seg_e) & (piece_tile < tiles_end[-1])
    piece = jnp.where(real, blk * mp + local, -1 - (piece_expert * (EXPERT_SUBTILE // SEG_ALIGN) + pad_k))
    return dict(piece=piece.astype(I32), tot=tot.astype(I32),
                tile_expert=tile_expert.astype(I32), n_tiles=tiles_end[-1:].astype(I32),
                tiles_end=tiles_end.astype(I32), tile_rows=tile_rows.astype(I32))


def _wait_rows(n_rows, min_rows, max_rows, src_rows, dst_rows, sem):
    size = min_rows
    while size <= max_rows:
        @pl.when((n_rows & size) != 0)
        def _(size=size):
            pltpu.make_async_copy(src_rows(size), dst_rows(size), sem).wait()
        size *= 2


def _expert_kernel(te_ref, nt_ref, tend_ref, rows_ref, piece_ref,
                   xg_hbm, w1_hbm, b1_ref, w2_hbm, b2_ref, yg_hbm,
                   xbuf, ybuf, w1s, w2s, w1b, w2b, xsem, ysem, sems, n_loaded, *, layer, zero_row, dump_row):
    i = pl.program_id(0)
    n_tiles = nt_ref[0]
    dff = w2b.shape[0]
    tm = xbuf.shape[1]
    pieces = tm // SEG_ALIGN
    slot = i % 2

    def weight_copies(expert, slot):
        return (pltpu.make_async_copy(w1_hbm.at[layer, expert], w1s.at[slot], sems.at[0, slot]),
                pltpu.make_async_copy(w2_hbm.at[layer, expert], w2s.at[slot], sems.at[1, slot]))

    def gather(tile, sl, exists=True):
        for j in range(pieces):
            p = piece_ref[tile * pieces + j]
            row = pl.multiple_of(jnp.where((p >= 0) & exists, p, zero_row), SEG_ALIGN)
            pltpu.make_async_copy(xg_hbm.at[pl.ds(row, SEG_ALIGN), :],
                                  xbuf.at[sl, pl.ds(j * SEG_ALIGN, SEG_ALIGN), :], xsem.at[sl]).start()

    def scatter(tile, sl, rows):
        for j in range(rows // SEG_ALIGN):
            p = piece_ref[tile * pieces + j]
            row = pl.multiple_of(jnp.where(p >= 0, p, dump_row + (-1 - p) * SEG_ALIGN), SEG_ALIGN)
            pltpu.make_async_copy(ybuf.at[sl, pl.ds(j * SEG_ALIGN, SEG_ALIGN), :],
                                  yg_hbm.at[pl.ds(row, SEG_ALIGN), :], ysem.at[sl]).start()

    def wait_gather(sl):
        pltpu.make_async_copy(xg_hbm.at[pl.ds(0, tm), :], xbuf.at[sl], xsem.at[sl]).wait()

    def wait_scatter(tile, sl):
        _wait_rows(rows_ref[tile], EXPERT_SUBTILE, tm, lambda n: ybuf.at[sl, pl.ds(0, n), :],
                   lambda n: yg_hbm.at[pl.ds(0, n), :], ysem.at[sl])

    @pl.when(i < n_tiles)
    def _():
        expert = te_ref[i]

        @pl.when(i == 0)
        def _():
            n_loaded[0] = 0
            for c in weight_copies(expert, 0):
                c.start()
            gather(0, 0)

        wait_gather(slot)

        @pl.when(i >= 2)
        def _():
            wait_scatter(jnp.maximum(i - 2, 0), slot)

        @pl.when((i == 0) | (expert != te_ref[jnp.maximum(i - 1, 0)]))
        def _():
            slot = n_loaded[0] % 2
            n_loaded[0] = n_loaded[0] + 1
            for c in weight_copies(expert, slot):
                c.wait()
            nxt = tend_ref[expert]

            @pl.when(nxt < n_tiles)
            def _():
                for c in weight_copies(te_ref[jnp.minimum(nxt, n_tiles - 1)], 1 - slot):
                    c.start()
            w1b[...] = w1s[slot].astype(BF16)
            w2b[...] = w2s[slot].astype(BF16)

        nxt_tile = jnp.minimum(i + 1, n_tiles - 1)
        for rows in range(EXPERT_SUBTILE, tm + 1, EXPERT_SUBTILE):
            @pl.when(rows_ref[i] == rows)
            def _(rows=rows):
                gather(nxt_tile, 1 - slot, i + 1 < n_tiles)
                hdn = _dot(xbuf[slot, 0:rows, :], w1b[...]) + b1_ref[0, 0]
                gate = jnp.minimum(hdn[:, 0:dff], SWIGLU_LIMIT)
                up = jnp.clip(hdn[:, dff:2 * dff], -SWIGLU_LIMIT, SWIGLU_LIMIT)
                act = (up + 1.0) * gate * _sigmoid(SWIGLU_ALPHA * gate)
                ybuf[slot, 0:rows, :] = (_dot(act.astype(BF16), w2b[...]) + b2_ref[0, 0]).astype(BF16)
                scatter(i, slot, rows)

        @pl.when(i == n_tiles - 1)
        def _():
            wait_gather(1 - slot)
            wait_scatter(i, slot)

            @pl.when(i >= 1)
            def _():
                wait_scatter(jnp.maximum(i - 1, 0), 1 - slot)


def _experts(xg, meta, layer, w1, b1, w2, b2, n_exp, max_tiles):
    rows_total, d = xg.shape
    dff2 = w1.shape[-1]
    dff = dff2 // 2
    tm = EXPERT_TILE
    mp = _local_rows(n_exp)
    last = lambda i, nt: jnp.minimum(i, jnp.maximum(nt[0] - 1, 0))
    exp_map = lambda i, te, nt, *_: (layer, te[last(i, nt)], 0, 0)
    return pl.pallas_call(
        functools.partial(_expert_kernel, layer=layer, zero_row=mp - SEG_ALIGN,
                          dump_row=rows_total - _dump_rows(n_exp)),
        grid_spec=pltpu.PrefetchScalarGridSpec(
            num_scalar_prefetch=5,
            grid=(max_tiles,),
            in_specs=[
                pl.BlockSpec(memory_space=pl.ANY),
                pl.BlockSpec(memory_space=pl.ANY),
                pl.BlockSpec((1, 1, 1, dff2), exp_map),
                pl.BlockSpec(memory_space=pl.ANY),
                pl.BlockSpec((1, 1, 1, d), exp_map),
            ],
            out_specs=pl.BlockSpec(memory_space=pl.ANY),
            scratch_shapes=[
                pltpu.VMEM((2, tm, d), BF16), pltpu.VMEM((2, tm, d), BF16),
                pltpu.VMEM((2, d, dff2), F32), pltpu.VMEM((2, dff, d), F32),
                pltpu.VMEM((d, dff2), BF16), pltpu.VMEM((dff, d), BF16),
                pltpu.SemaphoreType.DMA((2,)), pltpu.SemaphoreType.DMA((2,)),
                pltpu.SemaphoreType.DMA((2, 2)),
                pltpu.SMEM((1,), I32),
            ],
        ),
        out_shape=jax.ShapeDtypeStruct(xg.shape, xg.dtype),
        input_output_aliases={5: 0},
        compiler_params=pltpu.CompilerParams(dimension_semantics=("arbitrary",), vmem_limit_bytes=VMEM_LIMIT),
        name="moe_experts",
    )(meta["tile_expert"], meta["n_tiles"], meta["tiles_end"], meta["tile_rows"], meta["piece"],
      xg, w1, b1, w2, b2)


def _combine_kernel(h2_ref, yg_ref, gw_ref, pos_ref, pp_ref, ps_ref, wpg_ref, bpg_ref, wpp_ref, gple_ref, gfin_ref,
                    *rest, final, batch):
    if final:
        outp_ref, outs_ref, prows, orows = rest
    else:
        hn_ref, prows = rest
    b = pl.program_id(0)
    nb = pl.num_programs(0)
    tb = h2_ref.shape[0]
    mp = yg_ref.shape[0]

    pos = pos_ref[...]
    gw = gw_ref[...]
    jio = lax.broadcasted_iota(I32, (mp, tb), 0)
    cg = jnp.where(jio == pos[0:1, :], gw[0:1, :], 0.0)
    for k in range(1, TOP_K):
        cg = cg + jnp.where(jio == pos[k:k + 1, :], gw[k:k + 1, :], 0.0)
    moe = lax.dot_general(cg.astype(BF16), yg_ref[...], (((0,), (0,)), ((), ())),
                          preferred_element_type=F32)
    h3 = h2_ref[...] + moe
    n3 = _rms(h3, gple_ref[...]).astype(BF16)
    gate = _sigmoid(_dot(n3, wpg_ref[...]) + bpg_ref[...])
    p_rows = jnp.where(b == nb - 1, ps_ref[0], _time_major_rows(pp_ref.at[0], prows, batch))
    h4 = h3 + gate * _dot(p_rows.astype(BF16), wpp_ref[...])
    if final:
        out = _rms(h4, gfin_ref[...])

        @pl.when(b < nb - 1)
        def _():
            _batch_major_store(out, orows, outp_ref, batch)

        @pl.when(b == nb - 1)
        def _():
            outs_ref[...] = out
    else:
        hn_ref[...] = h4


def _combine(h2, yg, gw, pos, p_prompt, p_sample, layer, w, g_final, n_exp, final):
    n_total, d = h2.shape
    _, batch, seq, ple = p_prompt.shape
    tb = TOKEN_BLOCK
    nb = n_total // tb
    steps = tb // batch
    const2 = lambda b: (0, 0)
    tok = pl.BlockSpec((SUBLANES, tb), lambda b: (0, b))
    row = pl.BlockSpec((tb, d), lambda b: (b, 0))
    prompt_blk = lambda b: jnp.minimum(b, nb - 2)
    if final:
        out_specs = [pl.BlockSpec((batch, steps, d), lambda b: (0, prompt_blk(b), 0)),
                     pl.BlockSpec((tb, d), const2)]
        out_shape = [jax.ShapeDtypeStruct((batch, seq, d), F32), jax.ShapeDtypeStruct((tb, d), F32)]
        extra_scratch = [pltpu.VMEM((d // LANES, tb, LANES), F32)]
    else:
        out_specs = [row]
        out_shape = [jax.ShapeDtypeStruct((n_total, d), F32)]
        extra_scratch = []
    return pl.pallas_call(
        functools.partial(_combine_kernel, final=final, batch=batch),
        grid=(nb,),
        in_specs=[
            row,
            pl.BlockSpec((_local_rows(n_exp), d), lambda b: (b, 0)),
            tok, tok,
            pl.BlockSpec((1, batch, steps, ple), lambda b: (layer, 0, prompt_blk(b), 0)),
            pl.BlockSpec((1, tb, ple), lambda b: (layer, 0, 0)),
            pl.BlockSpec((d, d), const2),
            pl.BlockSpec((1, d), const2),
            pl.BlockSpec((ple, d), const2),
            pl.BlockSpec((1, d), const2),
            pl.BlockSpec((1, d), const2),
        ],
        out_specs=out_specs,
        scratch_shapes=[pltpu.VMEM((ple // LANES, tb, LANES), F32)] + extra_scratch,
        out_shape=out_shape,
        compiler_params=pltpu.CompilerParams(dimension_semantics=("arbitrary",), vmem_limit_bytes=VMEM_LIMIT),
        name="moe_combine_ple",
    )(h2, yg, gw, pos, p_prompt, p_sample, w["w_pg"], w["b_pg"], w["w_pp"], w["g_ple"], g_final)


def _layer_weights(l, batch, g_mix, w_in, g_v, w_s, b_s, conv_w, conv_b, w_a, b_a, w_x, b_x, lam,
                   g_oa, g_ob, w_out, g_ffn, w_r, b_r, g_ple, w_pg, b_pg, w_pp):
    d = w_in.shape[1]
    dh = d // 2
    gd = dh // GMLP_GROUPS
    n_exp = w_r.shape[2]
    row = lambda a: a[l].reshape(1, -1).astype(F32)
    wt = jnp.where(jnp.tril(jnp.ones((CHUNK, CHUNK), bool)), w_s[l], 0.0).astype(BF16)
    hr = HALF * batch
    row_t = jnp.arange(hr, dtype=I32) // batch
    row_b = jnp.arange(hr, dtype=I32) % batch
    expand_t = (row_t[:, None] == jnp.arange(HALF, dtype=I32)[None, :]).astype(BF16)
    same_b = row_b[:, None] == row_b[None, :]

    def kron_block(blk):
        rows = jnp.einsum("it,gts->gis", expand_t, blk, preferred_element_type=F32).astype(BF16)
        full = jnp.einsum("gis,js->gij", rows, expand_t, preferred_element_type=F32)
        return jnp.where(same_b, full, 0.0).astype(BF16)

    kd = jnp.stack([kron_block(wt[:, :HALF, :HALF]), kron_block(wt[:, HALF:, HALF:])])
    k10 = kron_block(wt[:, HALF:, :HALF])
    bs_rows = jnp.repeat(jnp.repeat(b_s[l].T.astype(F32), gd, axis=1), batch, axis=0).reshape(2, hr, dh)
    hd = dh // LRU_HEADS
    head_of = jnp.arange(dh, dtype=I32) // hd
    same_head = head_of[:, None] == head_of[None, :]
    block_diag = lambda w: jnp.where(same_head, jnp.tile(w.astype(F32).reshape(dh, hd), (1, LRU_HEADS)), 0.0)
    wa_f32, wx_f32 = block_diag(w_a[l]), block_diag(w_x[l])
    w_r_pad = jnp.zeros((d, LANES), F32).at[:, :n_exp].set(w_r[l])
    return dict(
        g_mix=row(g_mix), w_in=w_in[l].astype(BF16), w_in_f32=w_in[l].astype(F32), g_v=row(g_v),
        kd=kd, k10=k10, bs_rows=bs_rows,
        w0_row=jnp.repeat(w_s[l][:, 0, 0], gd).reshape(1, dh).astype(F32),
        b0_row=jnp.repeat(b_s[l][:, 0], gd).reshape(1, dh).astype(F32),
        conv_w=conv_w[l].astype(F32), conv_b=row(conv_b),
        wa=wa_f32.astype(BF16), wa_f32=wa_f32, b_a=row(b_a), wx=wx_f32.astype(BF16), wx_f32=wx_f32,
        b_x=row(b_x), lam=row(lam),
        g_oa=row(g_oa), g_ob=row(g_ob), w_out=w_out[l].astype(BF16), w_out_f32=w_out[l].astype(F32),
        g_ffn=row(g_ffn),
        w_r=w_r_pad, b_r=b_r[l].reshape(n_exp, 1).astype(F32),
        g_ple=row(g_ple), w_pg=w_pg[l].astype(BF16), b_pg=row(b_pg), w_pp=w_pp[l].astype(BF16))


def kernel(x_prompt, x_sample, state_conv, state_lru, p_prompt, p_sample, g_mix, w_in, g_v, w_s, b_s, conv_w, conv_b, w_a, b_a, w_x, b_x, lam, g_oa, g_ob, w_out, g_ffn, w_r, b_r, w1, b1, w2, b2, g_ple, w_pg, b_pg, w_pp, g_final):
    batch, seq, d = x_prompt.shape
    dec_batch, dec_seq, _ = x_sample.shape
    depth = w_in.shape[0]
    n_exp = w_r.shape[2]
    dh = d // 2
    ple = p_prompt.shape[-1]
    assert dec_seq == 1 and seq % CHUNK == 0 and batch % SUBLANES == 0
    assert dh == GMLP_GROUPS * LANES and dh % (LRU_HEADS * (LANES // 2)) == 0
    n_prompt = seq * batch
    assert n_prompt % TOKEN_BLOCK == 0
    n_valid = n_prompt + dec_batch
    n_total = _round_up(n_valid, TOKEN_BLOCK)
    s_rows = n_total - n_prompt
    assert s_rows == TOKEN_BLOCK
    nb = n_total // TOKEN_BLOCK
    max_tiles = _max_sorted_rows(n_valid, nb, n_exp) // EXPERT_TILE
    pad_rows = ((0, 0), (0, s_rows - dec_batch), (0, 0))

    h, h_sample, h_sample_block = x_prompt, jnp.pad(x_sample.reshape(1, dec_batch, d), pad_rows)[0], 0
    p_samp = jnp.pad(p_sample.reshape(depth, dec_batch, ple), pad_rows)
    sconv = jnp.pad(state_conv.transpose(0, 2, 1, 3), ((0, 0),) + pad_rows)
    slru = jnp.pad(state_lru, pad_rows)
    g_fin = g_final.reshape(1, d).astype(F32)
    b1_rows = b1.reshape(depth, n_exp, 1, -1)
    b2_rows = b2.reshape(depth, n_exp, 1, -1)

    v_p, conv_p, lru_p, v_s, conv_s, lru_s = [], [], [], [], [], []
    for l in range(depth):
        w = _layer_weights(l, batch, g_mix, w_in, g_v, w_s, b_s, conv_w, conv_b, w_a, b_a, w_x, b_x, lam,
                           g_oa, g_ob, w_out, g_ffn, w_r, b_r, g_ple, w_pg, b_pg, w_pp)
        y, vp, cp, lp = _mix_prompt(h, n_prompt, batch, w)
        post_out = _post(y, h, w, n_total, n_valid, n_exp, batch)
        h2, xg, gw, pos, cnt, vs, cs, ls = _sample_front(
            h_sample, h_sample_block, post_out, nb - 1, w, sconv[l], slru[l], n_valid, n_exp)
        meta = _route_meta(cnt[:, 0, :].astype(I32), max_tiles)
        yg = _experts(xg, meta, l, w1, b1_rows, w2, b2_rows, n_exp, max_tiles)
        final = l == depth - 1
        res = _combine(h2, yg, gw, pos, p_prompt, p_samp, l, w, g_fin, n_exp, final)
        if final:
            y_prompt, out_sample = res
        else:
            h = h_sample = res[0]
            h_sample_block = nb - 1
        v_p.append(vp.reshape(CHUNK, batch, dh).transpose(1, 0, 2))
        conv_p.append(cp.reshape(CONV_W - 1, batch, dh).transpose(1, 0, 2))
        lru_p.append(lp)
        v_s.append(vs[:dec_batch].reshape(dec_batch, 1, dh))
        conv_s.append(cs[:, :dec_batch].transpose(1, 0, 2))
        lru_s.append(ls[:dec_batch])

    y_sample = out_sample[:dec_batch].reshape(dec_batch, 1, d)
    return (y_prompt, y_sample, jnp.stack(v_p), jnp.stack(conv_p), jnp.stack(lru_p),
            jnp.stack(v_s), jnp.stack(conv_s), jnp.stack(lru_s))
```

```python
import functools

import jax
import jax.numpy as jnp
from jax import lax
from jax.experimental import pallas as pl
from jax.experimental.pallas import tpu as pltpu

F32 = jnp.float32
BF16 = jnp.bfloat16
I32 = jnp.int32

CHUNK = 128
HALF = CHUNK // 2
GMLP_GROUPS = 4
LRU_HEADS = 8
CONV_W = 4
LRU_C = 8.0
TOP_K = 4
SWIGLU_LIMIT = 7.0
SWIGLU_ALPHA = 1.702
EPS = 1e-6

LANES = 128
SUBLANES = 8
TOKEN_BLOCK = 256
EXPERT_TILE = 512
EXPERT_SUBTILE = 128
SEG_ALIGN = SUBLANES
VMEM_LIMIT = 56 * 1024 * 1024


def _round_up(x, m):
    return (x + m - 1) // m * m


def _dot(a, b):
    return jnp.dot(a, b, preferred_element_type=F32)


def _gelu(x):
    return 0.5 * x * (1.0 + jnp.tanh(0.7978845608028654 * (x + 0.044715 * (x * x * x))))


def _sigmoid(x):
    return 1.0 / (1.0 + jnp.exp(-x))


def _softplus(x):
    return jnp.maximum(x, 0.0) + jnp.log1p(jnp.exp(-jnp.abs(x)))


def _rms(x, g):
    ms = jnp.mean(x * x, axis=-1, keepdims=True)
    return x * lax.rsqrt(ms + EPS) * g


def _group_norm_128(x, g):
    outs = []
    for j in range(x.shape[1] // LANES):
        blk = x[:, j * LANES:(j + 1) * LANES]
        ms = jnp.mean(blk * blk, axis=-1, keepdims=True)
        outs.append(blk * lax.rsqrt(ms + EPS))
    return jnp.concatenate(outs, axis=1) * g


def _group_norm_64(x, g):
    half = LANES // 2
    lo_mask = lax.broadcasted_iota(I32, (1, LANES), 1) < half
    outs = []
    for j in range(x.shape[1] // LANES):
        blk = x[:, j * LANES:(j + 1) * LANES]
        sq = blk * blk
        lo = jnp.sum(jnp.where(lo_mask, sq, 0.0), axis=-1, keepdims=True)
        hi = jnp.sum(jnp.where(lo_mask, 0.0, sq), axis=-1, keepdims=True)
        ms = jnp.where(lo_mask, lo, hi) * (1.0 / half)
        outs.append(blk * lax.rsqrt(ms + EPS))
    return jnp.concatenate(outs, axis=1) * g


def _time_major_rows(src_ref, slabs_ref, batch):
    steps = src_ref.shape[1]
    n_slab = slabs_ref.shape[0]
    for b in range(batch):
        blk = src_ref[b]
        for j in range(n_slab):
            slabs_ref[j, pl.ds(b, steps, stride=batch), :] = blk[:, j * LANES:(j + 1) * LANES]
    return jnp.concatenate([slabs_ref[j] for j in range(n_slab)], axis=1)


def _batch_major_store(rows, slabs_ref, dst_ref, batch):
    steps = dst_ref.shape[1]
    n_slab = slabs_ref.shape[0]
    for j in range(n_slab):
        slabs_ref[j] = rows[:, j * LANES:(j + 1) * LANES]
    for b in range(batch):
        dst_ref[b] = jnp.concatenate(
            [slabs_ref[j, pl.ds(b, steps, stride=batch), :] for j in range(n_slab)], axis=1)


def _split_bf16(x):
    hi = x.astype(BF16)
    lo = (x - hi.astype(F32)).astype(BF16)
    return hi, lo


def _dot_bf16(x, w):
    return _dot(x.astype(BF16), w)


def _dot_3pass(x, w):
    x_hi, x_lo = _split_bf16(x)
    w_hi, w_lo = _split_bf16(w)
    return _dot(x_hi, w_hi) + (_dot(x_lo, w_hi) + _dot(x_hi, w_lo))


def _lru_gates(xc, wa_ref, ba_ref, wx_ref, bx_ref, lam_ref, mm):
    r = _sigmoid(mm(xc, wa_ref[...]) + ba_ref[...])
    i = _sigmoid(mm(xc, wx_ref[...]) + bx_ref[...])
    log_a = (-LRU_C * r) * _softplus(-lam_ref[...])
    a = jnp.exp(log_a)
    mult = jnp.sqrt(-jnp.tanh(log_a) * (a * a + 1.0))
    return a, mult, i


def _mix_prompt_kernel(h_ref, gmix_ref, win_ref, gv_ref, kd_ref, k10_ref, bs_ref, cw_ref, cb_ref,
                       wa_ref, ba_ref, wx_ref, bx_ref, lam_ref, goa_ref, gob_ref,
                       y_ref, v_ref, conv_ref, lru_ref,
                       hstate, xpad, vprev, a_s, b_s, hs_s, hrows, z_even, z_odd, *, batch, batch_major):
    c = pl.program_id(0)
    rows = y_ref.shape[0]
    dh = gv_ref.shape[1]
    tail = (CONV_W - 1) * batch

    @pl.when(c == 0)
    def _():
        z_odd[...] = jnp.zeros_like(z_odd)
        vprev[...] = jnp.zeros_like(vprev)

    @pl.when(c <= 1)
    def _():
        hstate[...] = jnp.zeros_like(hstate)
        xpad[0:tail, :] = jnp.zeros((tail, dh), F32)

    def tile_step(z_in, z_out, par):
        h = _time_major_rows(h_ref, hrows, batch) if batch_major else h_ref[...]
        z_out[...] = _dot(_rms(h, gmix_ref[...]).astype(BF16), win_ref[...])

        if par == 0:
            vprev[...] = jnp.zeros_like(vprev)

        vn = _group_norm_128(_gelu(z_in[:, dh:2 * dh]), gv_ref[...])
        vb = vn.astype(BF16)
        s_parts = []
        for g in range(GMLP_GROUPS):
            sl = slice(g * LANES, (g + 1) * LANES)
            s_parts.append(_dot(kd_ref[par, g], vb[:, sl]) + _dot(k10_ref[g], vprev[:, sl]))
        s = jnp.concatenate(s_parts, axis=1) + bs_ref[par]
        vprev[...] = vb
        v_ref[...] = vn
        y_ref[:, 0:dh] = _group_norm_128(_gelu(z_in[:, 0:dh]) * s, goa_ref[...]).astype(BF16)

        xpad[tail:tail + rows, :] = z_in[:, 2 * dh:3 * dh]
        xc = cb_ref[...] + cw_ref[0:1, :] * xpad[0:rows, :]
        for k in range(1, CONV_W):
            xc = xc + cw_ref[k:k + 1, :] * xpad[k * batch:k * batch + rows, :]
        new_tail = xpad[rows:rows + tail, :]
        xpad[0:tail, :] = new_tail

        a, mult, i = _lru_gates(xc, wa_ref, ba_ref, wx_ref, bx_ref, lam_ref, _dot_bf16)
        row = lax.broadcasted_iota(I32, (rows, 1), 0)
        mult = jnp.where((c == 1) & (row < batch), 1.0, mult)
        a_s[...] = a
        b_s[...] = mult * i * xc

        def step(t, h):
            r0 = t * batch
            h = a_s[r0:r0 + batch, :] * h + b_s[r0:r0 + batch, :]
            hs_s[r0:r0 + batch, :] = h
            return h

        h_last = hstate[...]
        for t in range(rows // batch):
            h_last = step(t, h_last)
        hstate[...] = h_last

        y_ref[:, dh:2 * dh] = _group_norm_64(hs_s[...] * _gelu(z_in[:, 3 * dh:4 * dh]), gob_ref[...]).astype(BF16)
        conv_ref[...] = new_tail
        lru_ref[...] = h_last

    @pl.when(c % 2 == 0)
    def _():
        tile_step(z_odd, z_even, 1)

    @pl.when(c % 2 == 1)
    def _():
        tile_step(z_even, z_odd, 0)


def _mix_prompt(h, n_rows, batch, w, layer):
    batch_major = h.ndim == 3
    d = h.shape[-1]
    dh = d // 2
    rows = HALF * batch
    n_steps = n_rows // rows
    tail = (CONV_W - 1) * batch
    const2 = lambda c: (0, 0)
    lw = functools.partial(_layer_spec, layer)
    proj_tile = lambda c: jnp.minimum(c, n_steps - 1)
    mix_tile = lambda c: jnp.maximum(c - 1, 0)
    h_spec = (pl.BlockSpec((batch, HALF, d), lambda c: (0, proj_tile(c), 0)) if batch_major
              else pl.BlockSpec((rows, d), lambda c: (proj_tile(c), 0)))
    return pl.pallas_call(
        functools.partial(_mix_prompt_kernel, batch=batch, batch_major=batch_major),
        grid=(n_steps + 1,),
        in_specs=[
            h_spec,
            lw(1, d), lw(d, 2 * d), lw(1, dh),
            lw(2, GMLP_GROUPS, rows, rows), lw(GMLP_GROUPS, rows, rows), lw(2, rows, dh),
            lw(CONV_W, dh), lw(1, dh),
            lw(dh, dh), lw(1, dh), lw(dh, dh), lw(1, dh),
            lw(1, dh), lw(1, dh), lw(1, dh),
        ],
        out_specs=[
            pl.BlockSpec((rows, d), lambda c: (mix_tile(c), 0)),
            pl.BlockSpec((rows, dh), lambda c: (jnp.maximum(mix_tile(c) - (n_steps - 2), 0), 0)),
            pl.BlockSpec((tail, dh), const2),
            pl.BlockSpec((batch, dh), const2),
        ],
        out_shape=[
            jax.ShapeDtypeStruct((n_rows, d), BF16),
            jax.ShapeDtypeStruct((2 * rows, dh), F32),
            jax.ShapeDtypeStruct((tail, dh), F32),
            jax.ShapeDtypeStruct((batch, dh), F32),
        ],
        scratch_shapes=[
            pltpu.VMEM((batch, dh), F32),
            pltpu.VMEM((tail + rows, dh), F32),
            pltpu.VMEM((rows, dh), BF16),
            pltpu.VMEM((rows, dh), F32),
            pltpu.VMEM((rows, dh), F32),
            pltpu.VMEM((rows, dh), F32),
            pltpu.VMEM((d // LANES, rows, LANES), F32),
            pltpu.VMEM((rows, 2 * d), F32),
            pltpu.VMEM((rows, 2 * d), F32),
        ],
        compiler_params=pltpu.CompilerParams(dimension_semantics=("arbitrary",), vmem_limit_bytes=VMEM_LIMIT),
        name="mix_prompt",
    )(h, w["g_mix"], w["w_in"], w["g_v"], w["kd"], w["k10"], w["bs_rows"], w["conv_w"], w["conv_b"],
      w["wa"], w["b_a"], w["wx"], w["b_x"], w["lam"], w["g_oa"], w["g_ob"])


def _sample_front_kernel(h_ref, h2_any, xg_any, gw_any, pos_any, cnt_any,
                         gmix_ref, win_ref, gv_ref, w0_ref, b0_ref, cw_ref, cb_ref,
                         wa_ref, ba_ref, wx_ref, bx_ref, lam_ref, goa_ref, gob_ref, sconv_ref, slru_ref,
                         wout_ref, gffn_ref, wr_ref, br_ref,
                         h2_ref, xg_ref, gw_ref, pos_ref, cnt_ref, v_ref, conv_ref, lru_ref,
                         *, block, n_valid, n_exp):
    del h2_any, xg_any, gw_any, pos_any, cnt_any
    dh = gv_ref.shape[1]
    h = h_ref[...]
    n = _rms(h, gmix_ref[...])
    proj = lambda lo: _dot_3pass(n, win_ref[:, lo:lo + dh])

    vn = _group_norm_128(_gelu(proj(dh)), gv_ref[...])
    v_ref[...] = vn
    s = vn * w0_ref[...] + b0_ref[...]
    ya = _group_norm_128(_gelu(proj(0)) * s, goa_ref[...])

    xb = proj(2 * dh)
    xc = cb_ref[...] + cw_ref[CONV_W - 1:CONV_W, :] * xb
    for k in range(CONV_W - 1):
        xc = xc + cw_ref[k:k + 1, :] * sconv_ref[k]
    for k in range(CONV_W - 2):
        conv_ref[k] = sconv_ref[k + 1]
    conv_ref[CONV_W - 2] = xb

    a, mult, i = _lru_gates(xc, wa_ref, ba_ref, wx_ref, bx_ref, lam_ref, _dot_3pass)
    h_new = a * slru_ref[...] + mult * i * xc
    lru_ref[...] = h_new
    yb = _group_norm_64(h_new * _gelu(proj(3 * dh)), gob_ref[...])

    h2 = h + _dot_3pass(ya, wout_ref[0:dh, :]) + _dot_3pass(yb, wout_ref[dh:2 * dh, :])
    _route_block(h2, gffn_ref, wr_ref, br_ref, h2_ref, xg_ref, gw_ref, pos_ref, cnt_ref,
                 block=block, n_valid=n_valid, n_exp=n_exp)


def _sample_front(h, h_block, post_out, block, w, layer, sconv, slru, n_valid, n_exp):
    d = h.shape[1]
    dh = d // 2
    rows = TOKEN_BLOCK
    const2 = lambda i: (0, 0)
    const3 = lambda i: (0, 0, 0)
    lw = functools.partial(_layer_spec, layer)
    vec = lw(1, dh)
    anyspace = pl.BlockSpec(memory_space=pl.ANY)
    blk_row = pl.BlockSpec((rows, d), lambda i: (block, 0))
    blk_tok = pl.BlockSpec((SUBLANES, rows), lambda i: (0, block))
    return pl.pallas_call(
        functools.partial(_sample_front_kernel, block=block, n_valid=n_valid, n_exp=n_exp),
        grid=(1,),
        in_specs=[
            pl.BlockSpec((rows, d), lambda i: (h_block, 0)),
            anyspace, anyspace, anyspace, anyspace, anyspace,
            lw(1, d), lw(d, 2 * d),
            vec, vec, vec,
            lw(CONV_W, dh), vec,
            lw(dh, dh), vec, lw(dh, dh), vec,
            vec, vec, vec,
            lw(CONV_W - 1, rows, dh), lw(rows, dh),
            lw(d, d), lw(1, d), lw(d, LANES), lw(n_exp, 1),
        ],
        out_specs=[
            blk_row, pl.BlockSpec((_local_rows(n_exp), d), lambda i: (block, 0)), blk_tok, blk_tok,
            pl.BlockSpec((1, SUBLANES, n_exp), lambda i: (block, 0, 0)),
            pl.BlockSpec((rows, dh), const2),
            pl.BlockSpec((CONV_W - 1, rows, dh), const3),
            pl.BlockSpec((rows, dh), const2),
        ],
        out_shape=[jax.ShapeDtypeStruct(a.shape, a.dtype) for a in post_out] + [
            jax.ShapeDtypeStruct((rows, dh), F32),
            jax.ShapeDtypeStruct((CONV_W - 1, rows, dh), F32),
            jax.ShapeDtypeStruct((rows, dh), F32),
        ],
        input_output_aliases={1 + k: k for k in range(len(post_out))},
        compiler_params=pltpu.CompilerParams(dimension_semantics=("arbitrary",), vmem_limit_bytes=VMEM_LIMIT),
        name="sample_front",
    )(h, *post_out, w["g_mix"], w["w_in_f32"], w["g_v"], w["w0_row"], w["b0_row"], w["conv_w"], w["conv_b"],
      w["wa_f32"], w["b_a"], w["wx_f32"], w["b_x"], w["lam"], w["g_oa"], w["g_ob"], sconv, slru,
      w["w_out_f32"], w["g_ffn"], w["w_r"], w["b_r"])


def _route_block(h2, gffn_ref, wr_ref, br_ref, h2_ref, xg_ref, gw_ref, pos_ref, cnt_ref,
                 *, block, n_valid, n_exp):
    tb = h2.shape[0]
    h2_ref[...] = h2
    n2 = _rms(h2, gffn_ref[...])

    logits = _dot_3pass(n2, wr_ref[...])
    lt = logits.T[0:n_exp, :] + br_ref[...]

    eio = lax.broadcasted_iota(I32, (n_exp, tb), 0).astype(F32)
    col = block * tb + lax.broadcasted_iota(I32, (1, tb), 1)
    valid = col < n_valid
    vals, hots = [], []
    for _ in range(TOP_K):
        m = jnp.max(lt, axis=0, keepdims=True)
        sel = jnp.min(jnp.where(lt == m, eio, float(n_exp)), axis=0, keepdims=True)
        hot = eio == sel
        lt = jnp.where(hot, -jnp.inf, lt)
        vals.append(m)
        hots.append(hot)
    exps = [jnp.exp(v - vals[0]) for v in vals]
    den = exps[0]
    for e in exps[1:]:
        den = den + e

    occ = jnp.zeros((n_exp, tb), F32)
    for hot in hots:
        occ = occ + jnp.where(hot & valid, 1.0, 0.0)
    occ_b = occ.astype(BF16)
    before = (lax.broadcasted_iota(I32, (tb, tb), 0) < lax.broadcasted_iota(I32, (tb, tb), 1))
    ranks_e = _dot(occ_b, jnp.where(before, 1.0, 0.0).astype(BF16))
    cnt_ref[0] = lax.dot_general(jnp.ones((SUBLANES, tb), BF16), occ_b, (((1,), (1,)), ((), ())),
                                 preferred_element_type=F32)
    cnt_col = jnp.sum(occ, axis=1, keepdims=True)
    units = jnp.floor((cnt_col + (SEG_ALIGN - 1.0)) * (1.0 / SEG_ALIGN))
    earlier = (lax.broadcasted_iota(I32, (n_exp, n_exp), 1) < lax.broadcasted_iota(I32, (n_exp, n_exp), 0))
    seg_off = _dot(jnp.where(earlier, 1.0, 0.0).astype(BF16),
                   jnp.broadcast_to(units, (n_exp, LANES)).astype(BF16))[:, 0:1] * float(SEG_ALIGN)
    rows_e = ranks_e + seg_off

    rio = lax.broadcasted_iota(I32, (SUBLANES, tb), 0)
    gw_out = jnp.zeros((SUBLANES, tb), F32)
    pos_out = jnp.full((SUBLANES, tb), -1, I32)
    for k in range(TOP_K):
        pos_k = jnp.sum(jnp.where(hots[k], rows_e, 0.0), axis=0, keepdims=True).astype(I32)
        gw_out = jnp.where(rio == k, exps[k] / den, gw_out)
        pos_out = jnp.where(rio == k, jnp.where(valid, pos_k, -1), pos_out)
    gw_ref[...] = gw_out
    pos_ref[...] = pos_out

    jio = lax.broadcasted_iota(I32, (xg_ref.shape[0], tb), 0)
    hit = jio == pos_out[0:1, :]
    for k in range(1, TOP_K):
        hit = hit | (jio == pos_out[k:k + 1, :])
    xg_ref[...] = _dot(jnp.where(hit, 1.0, 0.0).astype(BF16), n2.astype(BF16)).astype(BF16)


def _post_kernel(y_ref, h_ref, wout_ref, gffn_ref, wr_ref, br_ref,
                 h2_ref, xg_ref, gw_ref, pos_ref, cnt_ref, hrows, *, n_valid, n_exp, batch, batch_major):
    h = _time_major_rows(h_ref, hrows, batch) if batch_major else h_ref[...]
    h2 = h + _dot(y_ref[...], wout_ref[...])
    _route_block(h2, gffn_ref, wr_ref, br_ref, h2_ref, xg_ref, gw_ref, pos_ref, cnt_ref,
                 block=pl.program_id(0), n_valid=n_valid, n_exp=n_exp)


def _post(y, h, w, layer, n_total, n_valid, n_exp, batch):
    batch_major = h.ndim == 3
    d = h.shape[-1]
    tb = TOKEN_BLOCK
    nb = n_total // tb
    const2 = lambda b: (0, 0)
    tok = pl.BlockSpec((SUBLANES, tb), lambda b: (0, b))
    h_spec = (pl.BlockSpec((batch, tb // batch, d), lambda b: (0, b, 0)) if batch_major
              else pl.BlockSpec((tb, d), lambda b: (b, 0)))
    return pl.pallas_call(
        functools.partial(_post_kernel, n_valid=n_valid, n_exp=n_exp, batch=batch, batch_major=batch_major),
        grid=(y.shape[0] // tb,),
        in_specs=[
            pl.BlockSpec((tb, d), lambda b: (b, 0)),
            h_spec,
            _layer_spec(layer, d, d), _layer_spec(layer, 1, d), _layer_spec(layer, d, LANES),
            _layer_spec(layer, n_exp, 1),
        ],
        out_specs=[
            pl.BlockSpec((tb, d), lambda b: (b, 0)),
            pl.BlockSpec((_local_rows(n_exp), d), lambda b: (b, 0)),
            tok, tok,
            pl.BlockSpec((1, SUBLANES, n_exp), lambda b: (b, 0, 0)),
        ],
        out_shape=[
            jax.ShapeDtypeStruct((n_total, d), F32),
            jax.ShapeDtypeStruct((_grouped_rows(nb, n_exp), d), BF16),
            jax.ShapeDtypeStruct((SUBLANES, n_total), F32),
            jax.ShapeDtypeStruct((SUBLANES, n_total), I32),
            jax.ShapeDtypeStruct((nb, SUBLANES, n_exp), F32),
        ],
        scratch_shapes=[pltpu.VMEM((d // LANES, tb, LANES), F32)],
        compiler_params=pltpu.CompilerParams(dimension_semantics=("arbitrary",), vmem_limit_bytes=VMEM_LIMIT),
        name="post_router",
    )(y, h, w["w_out"], w["g_ffn"], w["w_r"], w["b_r"])


def _max_sorted_rows(n_valid, nb, n_exp):
    worst = TOP_K * n_valid + (SEG_ALIGN - 1) * nb * n_exp + (EXPERT_TILE - SEG_ALIGN) * n_exp
    return _round_up(worst, EXPERT_TILE)


def _local_rows(n_exp):
    return _round_up(TOP_K * TOKEN_BLOCK + (SEG_ALIGN - 1) * n_exp, LANES)


def _dump_rows(n_exp):
    return n_exp * EXPERT_SUBTILE


def _grouped_rows(nb, n_exp):
    return nb * _local_rows(n_exp) + _dump_rows(n_exp)


def _select_columns(per_block_expert, expert_of):
    n_exp = per_block_expert.shape[1]
    one_hot = (expert_of[:, None] == jnp.arange(n_exp, dtype=I32)[None, :]).astype(F32)
    picked = jnp.dot(one_hot, per_block_expert.T.astype(F32), precision=lax.Precision.HIGHEST)
    return jnp.round(picked).astype(I32)


def _route_meta(cnt, max_tiles):
    nb, n_exp = cnt.shape
    mp = _local_rows(n_exp)
    p8 = (cnt + SEG_ALIGN - 1) // SEG_ALIGN * SEG_ALIGN
    off = jnp.cumsum(p8, axis=1) - p8
    tot = jnp.sum(p8, axis=1)
    seg = jnp.sum(p8, axis=0)
    reg = (seg + EXPERT_TILE - 1) // EXPERT_TILE * EXPERT_TILE
    reg_start = jnp.cumsum(reg) - reg
    tiles_end = jnp.cumsum(reg // EXPERT_TILE)
    tile_ids = jnp.arange(max_tiles, dtype=I32)
    tile_expert = jnp.minimum(jnp.sum((tiles_end[None, :] <= tile_ids[:, None]).astype(I32), axis=1), n_exp - 1)
    of_tile = tile_expert[:, None] == jnp.arange(n_exp, dtype=I32)[None, :]
    pick = lambda per_expert: jnp.sum(jnp.where(of_tile, per_expert[None, :], 0), axis=1)
    used = jnp.clip(pick(seg) - (tile_ids - pick(tiles_end - reg // EXPERT_TILE)) * EXPERT_TILE, 0, EXPERT_TILE)
    tile_rows = (used + EXPERT_SUBTILE - 1) // EXPERT_SUBTILE * EXPERT_SUBTILE

    pieces_per_tile = EXPERT_TILE // SEG_ALIGN
    piece_expert = jnp.repeat(tile_expert, pieces_per_tile)
    piece_tile = jnp.repeat(tile_ids, pieces_per_tile)
    row = jnp.arange(max_tiles * pieces_per_tile, dtype=I32) * SEG_ALIGN
    is_e = piece_expert[:, None] == jnp.arange(n_exp, dtype=I32)[None, :]
    of_expert = lambda per_expert: jnp.sum(jnp.where(is_e, per_expert[None, :], 0), axis=1)
    rin = row - of_expert(reg_start)
    seg_e = of_expert(seg)
    ends = _select_columns(jnp.cumsum(p8, axis=0), piece_expert)
    blk = jnp.minimum(jnp.sum((ends <= rin[:, None]).astype(I32), axis=1), nb - 1)
    is_b = blk[:, None] == jnp.arange(nb, dtype=I32)[None, :]
    of_block = lambda a: jnp.sum(jnp.where(is_b, a, 0), axis=1)
    seg_first = of_block(ends - _select_columns(p8, piece_expert))
    local = of_block(_select_columns(off, piece_expert)) + rin - seg_first
    pad_k = jnp.clip((rin - seg_e) // SEG_ALIGN, 0, EXPERT_SUBTILE // SEG_ALIGN - 1)
    real = (rin < seg_e) & (piece_tile < tiles_end[-1])
    piece = jnp.where(real, blk * mp + local, -1 - (piece_expert * (EXPERT_SUBTILE // SEG_ALIGN) + pad_k))
    return dict(piece=piece.astype(I32), tot=tot.astype(I32),
                tile_expert=tile_expert.astype(I32), n_tiles=tiles_end[-1:].astype(I32),
                tiles_end=tiles_end.astype(I32), tile_rows=tile_rows.astype(I32))


def _wait_rows(n_rows, min_rows, max_rows, src_rows, dst_rows, sem):
    size = min_rows
    while size <= max_rows:
        @pl.when((n_rows & size) != 0)
        def _(size=size):
            pltpu.make_async_copy(src_rows(size), dst_rows(size), sem).wait()
        size *= 2


def _expert_kernel(te_ref, nt_ref, tend_ref, rows_ref, piece_ref,
                   xg_hbm, w1_hbm, b1_ref, w2_hbm, b2_ref, yg_hbm,
                   xbuf, ybuf, w1s, w2s, w1b, w2b, xsem, ysem, sems, n_loaded, *, layer, zero_row, dump_row):
    i = pl.program_id(0)
    n_tiles = nt_ref[0]
    dff = w2b.shape[0]
    tm = xbuf.shape[1]
    pieces = tm // SEG_ALIGN
    slot = i % 2

    def weight_copies(expert, slot):
        return (pltpu.make_async_copy(w1_hbm.at[layer, expert], w1s.at[slot], sems.at[0, slot]),
                pltpu.make_async_copy(w2_hbm.at[layer, expert], w2s.at[slot], sems.at[1, slot]))

    def gather(tile, sl, exists=True):
        for j in range(pieces):
            p = piece_ref[tile * pieces + j]
            row = pl.multiple_of(jnp.where((p >= 0) & exists, p, zero_row), SEG_ALIGN)
            pltpu.make_async_copy(xg_hbm.at[pl.ds(row, SEG_ALIGN), :],
                                  xbuf.at[sl, pl.ds(j * SEG_ALIGN, SEG_ALIGN), :], xsem.at[sl]).start()

    def scatter(tile, sl, rows):
        for j in range(rows // SEG_ALIGN):
            p = piece_ref[tile * pieces + j]
            row = pl.multiple_of(jnp.where(p >= 0, p, dump_row + (-1 - p) * SEG_ALIGN), SEG_ALIGN)
            pltpu.make_async_copy(ybuf.at[sl, pl.ds(j * SEG_ALIGN, SEG_ALIGN), :],
                                  yg_hbm.at[pl.ds(row, SEG_ALIGN), :], ysem.at[sl]).start()

    def wait_gather(sl):
        pltpu.make_async_copy(xg_hbm.at[pl.ds(0, tm), :], xbuf.at[sl], xsem.at[sl]).wait()

    def wait_scatter(tile, sl):
        _wait_rows(rows_ref[tile], EXPERT_SUBTILE, tm, lambda n: ybuf.at[sl, pl.ds(0, n), :],
                   lambda n: yg_hbm.at[pl.ds(0, n), :], ysem.at[sl])

    @pl.when(i < n_tiles)
    def _():
        expert = te_ref[i]

        @pl.when(i == 0)
        def _():
            n_loaded[0] = 0
            for c in weight_copies(expert, 0):
                c.start()
            gather(0, 0)

        wait_gather(slot)

        @pl.when(i >= 2)
        def _():
            wait_scatter(jnp.maximum(i - 2, 0), slot)

        @pl.when((i == 0) | (expert != te_ref[jnp.maximum(i - 1, 0)]))
        def _():
            slot = n_loaded[0] % 2
            n_loaded[0] = n_loaded[0] + 1
            for c in weight_copies(expert, slot):
                c.wait()
            nxt = tend_ref[expert]

            @pl.when(nxt < n_tiles)
            def _():
                for c in weight_copies(te_ref[jnp.minimum(nxt, n_tiles - 1)], 1 - slot):
                    c.start()
            w1b[...] = w1s[slot].astype(BF16)
            w2b[...] = w2s[slot].astype(BF16)

        nxt_tile = jnp.minimum(i + 1, n_tiles - 1)
        for rows in range(EXPERT_SUBTILE, tm + 1, EXPERT_SUBTILE):
            @pl.when(rows_ref[i] == rows)
            def _(rows=rows):
                gather(nxt_tile, 1 - slot, i + 1 < n_tiles)
                hdn = _dot(xbuf[slot, 0:rows, :], w1b[...]) + b1_ref[0, 0]
                gate = jnp.minimum(hdn[:, 0:dff], SWIGLU_LIMIT)
                up = jnp.clip(hdn[:, dff:2 * dff], -SWIGLU_LIMIT, SWIGLU_LIMIT)
                act = (up + 1.0) * gate * _sigmoid(SWIGLU_ALPHA * gate)
                ybuf[slot, 0:rows, :] = (_dot(act.astype(BF16), w2b[...]) + b2_ref[0, 0]).astype(BF16)
                scatter(i, slot, rows)

        @pl.when(i == n_tiles - 1)
        def _():
            wait_gather(1 - slot)
            wait_scatter(i, slot)

            @pl.when(i >= 1)
            def _():
                wait_scatter(jnp.maximum(i - 1, 0), 1 - slot)


def _experts(xg, meta, layer, w1, b1, w2, b2, n_exp, max_tiles):
    rows_total, d = xg.shape
    dff2 = w1.shape[-1]
    dff = dff2 // 2
    tm = EXPERT_TILE
    mp = _local_rows(n_exp)
    last = lambda i, nt: jnp.minimum(i, jnp.maximum(nt[0] - 1, 0))
    exp_map = lambda i, te, nt, *_: (layer, te[last(i, nt)], 0, 0)
    return pl.pallas_call(
        functools.partial(_expert_kernel, layer=layer, zero_row=mp - SEG_ALIGN,
                          dump_row=rows_total - _dump_rows(n_exp)),
        grid_spec=pltpu.PrefetchScalarGridSpec(
            num_scalar_prefetch=5,
            grid=(max_tiles,),
            in_specs=[
                pl.BlockSpec(memory_space=pl.ANY),
                pl.BlockSpec(memory_space=pl.ANY),
                pl.BlockSpec((1, 1, 1, dff2), exp_map),
                pl.BlockSpec(memory_space=pl.ANY),
                pl.BlockSpec((1, 1, 1, d), exp_map),
            ],
            out_specs=pl.BlockSpec(memory_space=pl.ANY),
            scratch_shapes=[
                pltpu.VMEM((2, tm, d), BF16), pltpu.VMEM((2, tm, d), BF16),
                pltpu.VMEM((2, d, dff2), F32), pltpu.VMEM((2, dff, d), F32),
                pltpu.VMEM((d, dff2), BF16), pltpu.VMEM((dff, d), BF16),
                pltpu.SemaphoreType.DMA((2,)), pltpu.SemaphoreType.DMA((2,)),
                pltpu.SemaphoreType.DMA((2, 2)),
                pltpu.SMEM((1,), I32),
            ],
        ),
        out_shape=jax.ShapeDtypeStruct(xg.shape, xg.dtype),
        input_output_aliases={5: 0},
        compiler_params=pltpu.CompilerParams(dimension_semantics=("arbitrary",), vmem_limit_bytes=VMEM_LIMIT),
        name="moe_experts",
    )(meta["tile_expert"], meta["n_tiles"], meta["tiles_end"], meta["tile_rows"], meta["piece"],
      xg, w1, b1, w2, b2)


def _combine_kernel(h2_ref, yg_ref, gw_ref, pos_ref, pp_ref, ps_ref, wpg_ref, bpg_ref, wpp_ref, gple_ref, gfin_ref,
                    *rest, final, batch):
    if final:
        outp_ref, outs_ref, prows, orows = rest
    else:
        hn_ref, prows = rest
    b = pl.program_id(0)
    nb = pl.num_programs(0)
    tb = h2_ref.shape[0]
    mp = yg_ref.shape[0]

    pos = pos_ref[...]
    gw = gw_ref[...]
    jio = lax.broadcasted_iota(I32, (mp, tb), 0)
    cg = jnp.where(jio == pos[0:1, :], gw[0:1, :], 0.0)
    for k in range(1, TOP_K):
        cg = cg + jnp.where(jio == pos[k:k + 1, :], gw[k:k + 1, :], 0.0)
    moe = lax.dot_general(cg.astype(BF16), yg_ref[...], (((0,), (0,)), ((), ())),
                          preferred_element_type=F32)
    h3 = h2_ref[...] + moe
    n3 = _rms(h3, gple_ref[...]).astype(BF16)
    gate = _sigmoid(_dot(n3, wpg_ref[...]) + bpg_ref[...])
    p_rows = jnp.where(b == nb - 1, ps_ref[0], _time_major_rows(pp_ref.at[0], prows, batch))
    h4 = h3 + gate * _dot(p_rows.astype(BF16), wpp_ref[...])
    if final:
        out = _rms(h4, gfin_ref[...])

        @pl.when(b < nb - 1)
        def _():
            _batch_major_store(out, orows, outp_ref, batch)

        @pl.when(b == nb - 1)
        def _():
            outs_ref[...] = out
    else:
        hn_ref[...] = h4


def _combine(h2, yg, gw, pos, p_prompt, p_sample, layer, w, g_final, n_exp, final):
    n_total, d = h2.shape
    _, batch, seq, ple = p_prompt.shape
    tb = TOKEN_BLOCK
    nb = n_total // tb
    steps = tb // batch
    const2 = lambda b: (0, 0)
    tok = pl.BlockSpec((SUBLANES, tb), lambda b: (0, b))
    row = pl.BlockSpec((tb, d), lambda b: (b, 0))
    prompt_blk = lambda b: jnp.minimum(b, nb - 2)
    if final:
        out_specs = [pl.BlockSpec((batch, steps, d), lambda b: (0, prompt_blk(b), 0)),
                     pl.BlockSpec((tb, d), const2)]
        out_shape = [jax.ShapeDtypeStruct((batch, seq, d), F32), jax.ShapeDtypeStruct((tb, d), F32)]
        extra_scratch = [pltpu.VMEM((d // LANES, tb, LANES), F32)]
    else:
        out_specs = [row]
        out_shape = [jax.ShapeDtypeStruct((n_total, d), F32)]
        extra_scratch = []
    return pl.pallas_call(
        functools.partial(_combine_kernel, final=final, batch=batch),
        grid=(nb,),
        in_specs=[
            row,
            pl.BlockSpec((_local_rows(n_exp), d), lambda b: (b, 0)),
            tok, tok,
            pl.BlockSpec((1, batch, steps, ple), lambda b: (layer, 0, prompt_blk(b), 0)),
            pl.BlockSpec((1, tb, ple), lambda b: (layer, 0, 0)),
            _layer_spec(layer, d, d), _layer_spec(layer, 1, d), _layer_spec(layer, ple, d),
            _layer_spec(layer, 1, d),
            pl.BlockSpec((1, d), const2),
        ],
        out_specs=out_specs,
        scratch_shapes=[pltpu.VMEM((ple // LANES, tb, LANES), F32)] + extra_scratch,
        out_shape=out_shape,
        compiler_params=pltpu.CompilerParams(dimension_semantics=("arbitrary",), vmem_limit_bytes=VMEM_LIMIT),
        name="moe_combine_ple",
    )(h2, yg, gw, pos, p_prompt, p_sample, w["w_pg"], w["b_pg"], w["w_pp"], w["g_ple"], g_final)


def _prepare_weights(batch, g_mix, w_in, g_v, w_s, b_s, conv_w, conv_b, w_a, b_a, w_x, b_x, lam,
                     g_oa, g_ob, w_out, g_ffn, w_r, b_r, g_ple, w_pg, b_pg, w_pp):
    depth, d = w_in.shape[0], w_in.shape[1]
    dh = d // 2
    gd = dh // GMLP_GROUPS
    n_exp = w_r.shape[2]
    row = lambda a: a.reshape(depth, 1, -1).astype(F32)
    wt = jnp.where(jnp.tril(jnp.ones((CHUNK, CHUNK), bool)), w_s, 0.0).astype(BF16)
    hr = HALF * batch
    row_t = jnp.arange(hr, dtype=I32) // batch
    row_b = jnp.arange(hr, dtype=I32) % batch
    expand_t = (row_t[:, None] == jnp.arange(HALF, dtype=I32)[None, :]).astype(BF16)
    same_b = row_b[:, None] == row_b[None, :]

    def kron_block(blk):
        rows = jnp.einsum("it,lgts->lgis", expand_t, blk, preferred_element_type=F32).astype(BF16)
        full = jnp.einsum("lgis,js->lgij", rows, expand_t, preferred_element_type=F32)
        return jnp.where(same_b, full, 0.0).astype(BF16)

    kd = jnp.stack([kron_block(wt[:, :, :HALF, :HALF]), kron_block(wt[:, :, HALF:, HALF:])], axis=1)
    k10 = kron_block(wt[:, :, HALF:, :HALF])
    bs_rows = jnp.repeat(jnp.repeat(b_s.transpose(0, 2, 1).astype(F32), gd, axis=2), batch, axis=1).reshape(
        depth, 2, hr, dh)
    hd = dh // LRU_HEADS
    head_of = jnp.arange(dh, dtype=I32) // hd
    same_head = head_of[:, None] == head_of[None, :]
    block_diag = lambda w: jnp.where(same_head, jnp.tile(w.astype(F32).reshape(depth, dh, hd), (1, 1, LRU_HEADS)),
                                     0.0)
    wa_f32, wx_f32 = block_diag(w_a), block_diag(w_x)
    return dict(
        g_mix=row(g_mix), w_in=w_in.astype(BF16), w_in_f32=w_in.astype(F32), g_v=row(g_v),
        kd=kd, k10=k10, bs_rows=bs_rows,
        w0_row=row(jnp.repeat(w_s[:, :, 0, 0], gd, axis=1)), b0_row=row(jnp.repeat(b_s[:, :, 0], gd, axis=1)),
        conv_w=conv_w.astype(F32), conv_b=row(conv_b),
        wa=wa_f32.astype(BF16), wa_f32=wa_f32, b_a=row(b_a), wx=wx_f32.astype(BF16), wx_f32=wx_f32,
        b_x=row(b_x), lam=row(lam),
        g_oa=row(g_oa), g_ob=row(g_ob), w_out=w_out.astype(BF16), w_out_f32=w_out.astype(F32),
        g_ffn=row(g_ffn),
        w_r=jnp.pad(w_r.astype(F32), ((0, 0), (0, 0), (0, LANES - n_exp))),
        b_r=b_r.reshape(depth, n_exp, 1).astype(F32),
        g_ple=row(g_ple), w_pg=w_pg.astype(BF16), b_pg=row(b_pg), w_pp=w_pp.astype(BF16))


def _layer_spec(layer, *shape):
    return pl.BlockSpec((None,) + shape, lambda *_: (layer,) + (0,) * len(shape))


def kernel(x_prompt, x_sample, state_conv, state_lru, p_prompt, p_sample, g_mix, w_in, g_v, w_s, b_s, conv_w, conv_b, w_a, b_a, w_x, b_x, lam, g_oa, g_ob, w_out, g_ffn, w_r, b_r, w1, b1, w2, b2, g_ple, w_pg, b_pg, w_pp, g_final):
    batch, seq, d = x_prompt.shape
    dec_batch, dec_seq, _ = x_sample.shape
    depth = w_in.shape[0]
    n_exp = w_r.shape[2]
    dh = d // 2
    ple = p_prompt.shape[-1]
    assert dec_seq == 1 and seq % CHUNK == 0 and batch % SUBLANES == 0
    assert dh == GMLP_GROUPS * LANES and dh % (LRU_HEADS * (LANES // 2)) == 0
    n_prompt = seq * batch
    assert n_prompt % TOKEN_BLOCK == 0
    n_valid = n_prompt + dec_batch
    n_total = _round_up(n_valid, TOKEN_BLOCK)
    s_rows = n_total - n_prompt
    assert s_rows == TOKEN_BLOCK
    nb = n_total // TOKEN_BLOCK
    max_tiles = _max_sorted_rows(n_valid, nb, n_exp) // EXPERT_TILE
    pad_rows = ((0, 0), (0, s_rows - dec_batch), (0, 0))

    h, h_sample, h_sample_block = x_prompt, jnp.pad(x_sample.reshape(1, dec_batch, d), pad_rows)[0], 0
    p_samp = jnp.pad(p_sample.reshape(depth, dec_batch, ple), pad_rows)
    sconv = jnp.pad(state_conv.transpose(0, 2, 1, 3), ((0, 0),) + pad_rows)
    slru = jnp.pad(state_lru, pad_rows)
    g_fin = g_final.reshape(1, d).astype(F32)
    b1_rows = b1.reshape(depth, n_exp, 1, -1)
    b2_rows = b2.reshape(depth, n_exp, 1, -1)

    w = _prepare_weights(batch, g_mix, w_in, g_v, w_s, b_s, conv_w, conv_b, w_a, b_a, w_x, b_x, lam,
                         g_oa, g_ob, w_out, g_ffn, w_r, b_r, g_ple, w_pg, b_pg, w_pp)
    v_p, conv_p, lru_p, v_s, conv_s, lru_s = [], [], [], [], [], []
    for l in range(depth):
        y, vp, cp, lp = _mix_prompt(h, n_prompt, batch, w, l)
        post_out = _post(y, h, w, l, n_total, n_valid, n_exp, batch)
        h2, xg, gw, pos, cnt, vs, cs, ls = _sample_front(
            h_sample, h_sample_block, post_out, nb - 1, w, l, sconv, slru, n_valid, n_exp)
        meta = _route_meta(cnt[:, 0, :].astype(I32), max_tiles)
        yg = _experts(xg, meta, l, w1, b1_rows, w2, b2_rows, n_exp, max_tiles)
        final = l == depth - 1
        res = _combine(h2, yg, gw, pos, p_prompt, p_samp, l, w, g_fin, n_exp, final)
        if final:
            y_prompt, out_sample = res
        else:
            h = h_sample = res[0]
            h_sample_block = nb - 1
        for acc, val in zip((v_p, conv_p, lru_p, v_s, conv_s, lru_s), (vp, cp, lp, vs, cs, ls)):
            acc.append(val)

    y_sample = out_sample[:dec_batch].reshape(dec_batch, 1, d)
    v_prompt = jnp.stack(v_p).reshape(depth, CHUNK, batch, dh).transpose(0, 2, 1, 3)
    conv_prompt = jnp.stack(conv_p).reshape(depth, CONV_W - 1, batch, dh).transpose(0, 2, 1, 3)
    v_sample = jnp.stack(v_s)[:, :dec_batch].reshape(depth, dec_batch, 1, dh)
    conv_sample = jnp.stack(conv_s)[:, :, :dec_batch].transpose(0, 2, 1, 3)
    return (y_prompt, y_sample, v_prompt, conv_prompt, jnp.stack(lru_p),
            v_sample, conv_sample, jnp.stack(lru_s)[:, :dec_batch])
```

```python
import functools

import jax
import jax.numpy as jnp
from jax import lax
from jax.experimental import pallas as pl
from jax.experimental.pallas import tpu as pltpu

F32 = jnp.float32
BF16 = jnp.bfloat16
I32 = jnp.int32

CHUNK = 128
HALF = CHUNK // 2
GMLP_GROUPS = 4
LRU_HEADS = 8
CONV_W = 4
LRU_C = 8.0
TOP_K = 4
SWIGLU_LIMIT = 7.0
SWIGLU_ALPHA = 1.702
EPS = 1e-6

LANES = 128
SUBLANES = 8
TOKEN_BLOCK = 256
EXPERT_TILE = 512
EXPERT_SUBTILE = 128
SEG_ALIGN = SUBLANES
VMEM_LIMIT = 56 * 1024 * 1024


def _round_up(x, m):
    return (x + m - 1) // m * m


def _dot(a, b):
    return jnp.dot(a, b, preferred_element_type=F32)


def _gelu(x):
    return 0.5 * x * (1.0 + jnp.tanh(0.7978845608028654 * (x + 0.044715 * (x * x * x))))


def _sigmoid(x):
    return 1.0 / (1.0 + jnp.exp(-x))


def _softplus(x):
    return jnp.maximum(x, 0.0) + jnp.log1p(jnp.exp(-jnp.abs(x)))


def _rms(x, g):
    ms = jnp.mean(x * x, axis=-1, keepdims=True)
    return x * lax.rsqrt(ms + EPS) * g


def _group_norm_128(x, g):
    outs = []
    for j in range(x.shape[1] // LANES):
        blk = x[:, j * LANES:(j + 1) * LANES]
        ms = jnp.mean(blk * blk, axis=-1, keepdims=True)
        outs.append(blk * lax.rsqrt(ms + EPS))
    return jnp.concatenate(outs, axis=1) * g


def _group_norm_64(x, g):
    half = LANES // 2
    lo_mask = lax.broadcasted_iota(I32, (1, LANES), 1) < half
    outs = []
    for j in range(x.shape[1] // LANES):
        blk = x[:, j * LANES:(j + 1) * LANES]
        sq = blk * blk
        lo = jnp.sum(jnp.where(lo_mask, sq, 0.0), axis=-1, keepdims=True)
        hi = jnp.sum(jnp.where(lo_mask, 0.0, sq), axis=-1, keepdims=True)
        ms = jnp.where(lo_mask, lo, hi) * (1.0 / half)
        outs.append(blk * lax.rsqrt(ms + EPS))
    return jnp.concatenate(outs, axis=1) * g


def _time_major_rows(src_ref, slabs_ref, batch):
    steps = src_ref.shape[1]
    n_slab = slabs_ref.shape[0]
    for b in range(batch):
        blk = src_ref[b]
        for j in range(n_slab):
            slabs_ref[j, pl.ds(b, steps, stride=batch), :] = blk[:, j * LANES:(j + 1) * LANES]
    return jnp.concatenate([slabs_ref[j] for j in range(n_slab)], axis=1)


def _batch_major_store(rows, slabs_ref, dst_ref, batch):
    steps = dst_ref.shape[1]
    n_slab = slabs_ref.shape[0]
    for j in range(n_slab):
        slabs_ref[j] = rows[:, j * LANES:(j + 1) * LANES]
    for b in range(batch):
        dst_ref[b] = jnp.concatenate(
            [slabs_ref[j, pl.ds(b, steps, stride=batch), :] for j in range(n_slab)], axis=1)


def _split_bf16(x):
    hi = x.astype(BF16)
    lo = (x - hi.astype(F32)).astype(BF16)
    return hi, lo


def _dot_bf16(x, w):
    return _dot(x.astype(BF16), w)


def _dot_3pass(x, w):
    x_hi, x_lo = _split_bf16(x)
    w_hi, w_lo = _split_bf16(w)
    return _dot(x_hi, w_hi) + (_dot(x_lo, w_hi) + _dot(x_hi, w_lo))


def _lru_gates(xc, wa_ref, ba_ref, wx_ref, bx_ref, lam_ref, mm):
    r = _sigmoid(mm(xc, wa_ref[...]) + ba_ref[...])
    i = _sigmoid(mm(xc, wx_ref[...]) + bx_ref[...])
    log_a = (-LRU_C * r) * _softplus(-lam_ref[...])
    a = jnp.exp(log_a)
    mult = jnp.sqrt(-jnp.tanh(log_a) * (a * a + 1.0))
    return a, mult, i


def _mix_prompt_kernel(h_ref, gmix_ref, win_ref, gv_ref, kd_ref, k10_ref, bs_ref, cw_ref, cb_ref,
                       wa_ref, ba_ref, wx_ref, bx_ref, lam_ref, goa_ref, gob_ref,
                       y_ref, v_ref, conv_ref, lru_ref,
                       hstate, xpad, vprev, a_s, b_s, hs_s, hrows, z_even, z_odd, *, batch, batch_major):
    c = pl.program_id(0)
    rows = y_ref.shape[0]
    dh = gv_ref.shape[1]
    tail = (CONV_W - 1) * batch

    @pl.when(c == 0)
    def _():
        z_odd[...] = jnp.zeros_like(z_odd)
        vprev[...] = jnp.zeros_like(vprev)

    @pl.when(c <= 1)
    def _():
        hstate[...] = jnp.zeros_like(hstate)
        xpad[0:tail, :] = jnp.zeros((tail, dh), F32)

    def tile_step(z_in, z_out, par):
        if par == 0:
            vprev[...] = jnp.zeros_like(vprev)

        vn = _group_norm_128(_gelu(z_in[:, dh:2 * dh]), gv_ref[...])
        vb = vn.astype(BF16)
        s_parts = []
        for g in range(GMLP_GROUPS):
            sl = slice(g * LANES, (g + 1) * LANES)
            s_parts.append(_dot(kd_ref[par, g], vb[:, sl]) + _dot(k10_ref[g], vprev[:, sl]))
        s = jnp.concatenate(s_parts, axis=1) + bs_ref[par]
        vprev[...] = vb
        v_ref[...] = vn
        y_ref[:, 0:dh] = _group_norm_128(_gelu(z_in[:, 0:dh]) * s, goa_ref[...]).astype(BF16)

        xpad[tail:tail + rows, :] = z_in[:, 2 * dh:3 * dh]
        xc = cb_ref[...] + cw_ref[0:1, :] * xpad[0:rows, :]
        for k in range(1, CONV_W):
            xc = xc + cw_ref[k:k + 1, :] * xpad[k * batch:k * batch + rows, :]
        new_tail = xpad[rows:rows + tail, :]
        xpad[0:tail, :] = new_tail

        a, mult, i = _lru_gates(xc, wa_ref, ba_ref, wx_ref, bx_ref, lam_ref, _dot_bf16)
        row = lax.broadcasted_iota(I32, (rows, 1), 0)
        mult = jnp.where((c == 1) & (row < batch), 1.0, mult)
        a_s[...] = a
        b_s[...] = mult * i * xc

        def step(t, h):
            r0 = t * batch
            h = a_s[r0:r0 + batch, :] * h + b_s[r0:r0 + batch, :]
            hs_s[r0:r0 + batch, :] = h
            return h

        h_last = hstate[...]
        for t in range(rows // batch):
            h_last = step(t, h_last)
        hstate[...] = h_last

        y_ref[:, dh:2 * dh] = _group_norm_64(hs_s[...] * _gelu(z_in[:, 3 * dh:4 * dh]), gob_ref[...]).astype(BF16)
        conv_ref[...] = new_tail
        lru_ref[...] = h_last

        h = _time_major_rows(h_ref, hrows, batch) if batch_major else h_ref[...]
        z_out[...] = _dot(_rms(h, gmix_ref[...]).astype(BF16), win_ref[...])

    @pl.when(c % 2 == 0)
    def _():
        tile_step(z_odd, z_even, 1)

    @pl.when(c % 2 == 1)
    def _():
        tile_step(z_even, z_odd, 0)


def _mix_prompt(h, n_rows, batch, w, layer):
    batch_major = h.ndim == 3
    d = h.shape[-1]
    dh = d // 2
    rows = HALF * batch
    n_steps = n_rows // rows
    tail = (CONV_W - 1) * batch
    const2 = lambda c: (0, 0)
    lw = functools.partial(_layer_spec, layer)
    proj_tile = lambda c: jnp.minimum(c, n_steps - 1)
    mix_tile = lambda c: jnp.maximum(c - 1, 0)
    h_spec = (pl.BlockSpec((batch, HALF, d), lambda c: (0, proj_tile(c), 0)) if batch_major
              else pl.BlockSpec((rows, d), lambda c: (proj_tile(c), 0)))
    return pl.pallas_call(
        functools.partial(_mix_prompt_kernel, batch=batch, batch_major=batch_major),
        grid=(n_steps + 1,),
        in_specs=[
            h_spec,
            lw(1, d), lw(d, 2 * d), lw(1, dh),
            lw(2, GMLP_GROUPS, rows, rows), lw(GMLP_GROUPS, rows, rows), lw(2, rows, dh),
            lw(CONV_W, dh), lw(1, dh),
            lw(dh, dh), lw(1, dh), lw(dh, dh), lw(1, dh),
            lw(1, dh), lw(1, dh), lw(1, dh),
        ],
        out_specs=[
            pl.BlockSpec((rows, d), lambda c: (mix_tile(c), 0)),
            pl.BlockSpec((rows, dh), lambda c: (jnp.maximum(mix_tile(c) - (n_steps - 2), 0), 0)),
            pl.BlockSpec((tail, dh), const2),
            pl.BlockSpec((batch, dh), const2),
        ],
        out_shape=[
            jax.ShapeDtypeStruct((n_rows, d), BF16),
            jax.ShapeDtypeStruct((2 * rows, dh), F32),
            jax.ShapeDtypeStruct((tail, dh), F32),
            jax.ShapeDtypeStruct((batch, dh), F32),
        ],
        scratch_shapes=[
            pltpu.VMEM((batch, dh), F32),
            pltpu.VMEM((tail + rows, dh), F32),
            pltpu.VMEM((rows, dh), BF16),
            pltpu.VMEM((rows, dh), F32),
            pltpu.VMEM((rows, dh), F32),
            pltpu.VMEM((rows, dh), F32),
            pltpu.VMEM((d // LANES, rows, LANES), F32),
            pltpu.VMEM((rows, 2 * d), F32),
            pltpu.VMEM((rows, 2 * d), F32),
        ],
        compiler_params=pltpu.CompilerParams(dimension_semantics=("arbitrary",), vmem_limit_bytes=VMEM_LIMIT),
        name="mix_prompt",
    )(h, w["g_mix"], w["w_in"], w["g_v"], w["kd"], w["k10"], w["bs_rows"], w["conv_w"], w["conv_b"],
      w["wa"], w["b_a"], w["wx"], w["b_x"], w["lam"], w["g_oa"], w["g_ob"])


def _sample_front_kernel(h_ref, h2_any, xg_any, gw_any, pos_any, cnt_any,
                         gmix_ref, win_ref, gv_ref, w0_ref, b0_ref, cw_ref, cb_ref,
                         wa_ref, ba_ref, wx_ref, bx_ref, lam_ref, goa_ref, gob_ref, sconv_ref, slru_ref,
                         wout_ref, gffn_ref, wr_ref, br_ref,
                         h2_ref, xg_ref, gw_ref, pos_ref, cnt_ref, v_ref, conv_ref, lru_ref,
                         *, block, n_valid, n_exp):
    del h2_any, xg_any, gw_any, pos_any, cnt_any
    dh = gv_ref.shape[1]
    h = h_ref[...]
    n = _rms(h, gmix_ref[...])
    proj = lambda lo: _dot_3pass(n, win_ref[:, lo:lo + dh])

    vn = _group_norm_128(_gelu(proj(dh)), gv_ref[...])
    v_ref[...] = vn
    s = vn * w0_ref[...] + b0_ref[...]
    ya = _group_norm_128(_gelu(proj(0)) * s, goa_ref[...])

    xb = proj(2 * dh)
    xc = cb_ref[...] + cw_ref[CONV_W - 1:CONV_W, :] * xb
    for k in range(CONV_W - 1):
        xc = xc + cw_ref[k:k + 1, :] * sconv_ref[k]
    for k in range(CONV_W - 2):
        conv_ref[k] = sconv_ref[k + 1]
    conv_ref[CONV_W - 2] = xb

    a, mult, i = _lru_gates(xc, wa_ref, ba_ref, wx_ref, bx_ref, lam_ref, _dot_3pass)
    h_new = a * slru_ref[...] + mult * i * xc
    lru_ref[...] = h_new
    yb = _group_norm_64(h_new * _gelu(proj(3 * dh)), gob_ref[...])

    h2 = h + _dot_3pass(ya, wout_ref[0:dh, :]) + _dot_3pass(yb, wout_ref[dh:2 * dh, :])
    _route_block(h2, gffn_ref, wr_ref, br_ref, h2_ref, xg_ref, gw_ref, pos_ref, cnt_ref,
                 block=block, n_valid=n_valid, n_exp=n_exp)


def _sample_front(h, h_block, post_out, block, w, layer, sconv, slru, n_valid, n_exp):
    d = h.shape[1]
    dh = d // 2
    rows = TOKEN_BLOCK
    const2 = lambda i: (0, 0)
    const3 = lambda i: (0, 0, 0)
    lw = functools.partial(_layer_spec, layer)
    vec = lw(1, dh)
    anyspace = pl.BlockSpec(memory_space=pl.ANY)
    blk_row = pl.BlockSpec((rows, d), lambda i: (block, 0))
    blk_tok = pl.BlockSpec((SUBLANES, rows), lambda i: (0, block))
    return pl.pallas_call(
        functools.partial(_sample_front_kernel, block=block, n_valid=n_valid, n_exp=n_exp),
        grid=(1,),
        in_specs=[
            pl.BlockSpec((rows, d), lambda i: (h_block, 0)),
            anyspace, anyspace, anyspace, anyspace, anyspace,
            lw(1, d), lw(d, 2 * d),
            vec, vec, vec,
            lw(CONV_W, dh), vec,
            lw(dh, dh), vec, lw(dh, dh), vec,
            vec, vec, vec,
            lw(CONV_W - 1, rows, dh), lw(rows, dh),
            lw(d, d), lw(1, d), lw(d, LANES), lw(n_exp, 1),
        ],
        out_specs=[
            blk_row, pl.BlockSpec((_local_rows(n_exp), d), lambda i: (block, 0)), blk_tok, blk_tok,
            pl.BlockSpec((1, SUBLANES, n_exp), lambda i: (block, 0, 0)),
            pl.BlockSpec((rows, dh), const2),
            pl.BlockSpec((CONV_W - 1, rows, dh), const3),
            pl.BlockSpec((rows, dh), const2),
        ],
        out_shape=[jax.ShapeDtypeStruct(a.shape, a.dtype) for a in post_out] + [
            jax.ShapeDtypeStruct((rows, dh), F32),
            jax.ShapeDtypeStruct((CONV_W - 1, rows, dh), F32),
            jax.ShapeDtypeStruct((rows, dh), F32),
        ],
        input_output_aliases={1 + k: k for k in range(len(post_out))},
        compiler_params=pltpu.CompilerParams(dimension_semantics=("arbitrary",), vmem_limit_bytes=VMEM_LIMIT),
        name="sample_front",
    )(h, *post_out, w["g_mix"], w["w_in_f32"], w["g_v"], w["w0_row"], w["b0_row"], w["conv_w"], w["conv_b"],
      w["wa_f32"], w["b_a"], w["wx_f32"], w["b_x"], w["lam"], w["g_oa"], w["g_ob"], sconv, slru,
      w["w_out_f32"], w["g_ffn"], w["w_r"], w["b_r"])


def _route_block(h2, gffn_ref, wr_ref, br_ref, h2_ref, xg_ref, gw_ref, pos_ref, cnt_ref,
                 *, block, n_valid, n_exp):
    tb = h2.shape[0]
    h2_ref[...] = h2
    n2 = _rms(h2, gffn_ref[...])

    n_hi, n_lo = _split_bf16(n2)
    w_hi, w_lo = _split_bf16(wr_ref[...])
    both = _dot(n_hi, jnp.concatenate([w_hi, w_lo], axis=1))
    logits = both[:, 0:LANES] + (_dot(n_lo, w_hi) + both[:, LANES:2 * LANES])
    lt = logits.T[0:n_exp, :] + br_ref[...]

    eio = lax.broadcasted_iota(I32, (n_exp, tb), 0).astype(F32)
    col = block * tb + lax.broadcasted_iota(I32, (1, tb), 1)
    valid = col < n_valid
    vals, hots = [], []
    for _ in range(TOP_K):
        m = jnp.max(lt, axis=0, keepdims=True)
        sel = jnp.min(jnp.where(lt == m, eio, float(n_exp)), axis=0, keepdims=True)
        hot = eio == sel
        lt = jnp.where(hot, -jnp.inf, lt)
        vals.append(m)
        hots.append(hot)
    exps = [jnp.exp(v - vals[0]) for v in vals]
    den = exps[0]
    for e in exps[1:]:
        den = den + e

    occ = jnp.zeros((n_exp, tb), F32)
    for hot in hots:
        occ = occ + jnp.where(hot & valid, 1.0, 0.0)
    occ_b = occ.astype(BF16)
    before = (lax.broadcasted_iota(I32, (tb, tb), 0) < lax.broadcasted_iota(I32, (tb, tb), 1))
    ranks_e = _dot(occ_b, jnp.where(before, 1.0, 0.0).astype(BF16))
    cnt_ref[0] = lax.dot_general(jnp.ones((SUBLANES, tb), BF16), occ_b, (((1,), (1,)), ((), ())),
                                 preferred_element_type=F32)
    cnt_col = jnp.sum(occ, axis=1, keepdims=True)
    units = jnp.floor((cnt_col + (SEG_ALIGN - 1.0)) * (1.0 / SEG_ALIGN))
    earlier = (lax.broadcasted_iota(I32, (n_exp, n_exp), 1) < lax.broadcasted_iota(I32, (n_exp, n_exp), 0))
    seg_off = _dot(jnp.where(earlier, 1.0, 0.0).astype(BF16),
                   jnp.broadcast_to(units, (n_exp, LANES)).astype(BF16))[:, 0:1] * float(SEG_ALIGN)
    rows_e = ranks_e + seg_off

    rio = lax.broadcasted_iota(I32, (SUBLANES, tb), 0)
    gw_out = jnp.zeros((SUBLANES, tb), F32)
    pos_out = jnp.full((SUBLANES, tb), -1, I32)
    for k in range(TOP_K):
        pos_k = jnp.sum(jnp.where(hots[k], rows_e, 0.0), axis=0, keepdims=True).astype(I32)
        gw_out = jnp.where(rio == k, exps[k] / den, gw_out)
        pos_out = jnp.where(rio == k, jnp.where(valid, pos_k, -1), pos_out)
    gw_ref[...] = gw_out
    pos_ref[...] = pos_out

    jio = lax.broadcasted_iota(I32, (xg_ref.shape[0], tb), 0)
    hit = jio == pos_out[0:1, :]
    for k in range(1, TOP_K):
        hit = hit | (jio == pos_out[k:k + 1, :])
    xg_ref[...] = _dot(jnp.where(hit, 1.0, 0.0).astype(BF16), n_hi).astype(BF16)


def _post_kernel(y_ref, h_ref, wout_ref, gffn_ref, wr_ref, br_ref,
                 h2_ref, xg_ref, gw_ref, pos_ref, cnt_ref, hrows, *, n_valid, n_exp, batch, batch_major):
    h = _time_major_rows(h_ref, hrows, batch) if batch_major else h_ref[...]
    h2 = h + _dot(y_ref[...], wout_ref[...])
    _route_block(h2, gffn_ref, wr_ref, br_ref, h2_ref, xg_ref, gw_ref, pos_ref, cnt_ref,
                 block=pl.program_id(0), n_valid=n_valid, n_exp=n_exp)


def _post(y, h, w, layer, n_total, n_valid, n_exp, batch):
    batch_major = h.ndim == 3
    d = h.shape[-1]
    tb = TOKEN_BLOCK
    nb = n_total // tb
    const2 = lambda b: (0, 0)
    tok = pl.BlockSpec((SUBLANES, tb), lambda b: (0, b))
    h_spec = (pl.BlockSpec((batch, tb // batch, d), lambda b: (0, b, 0)) if batch_major
              else pl.BlockSpec((tb, d), lambda b: (b, 0)))
    return pl.pallas_call(
        functools.partial(_post_kernel, n_valid=n_valid, n_exp=n_exp, batch=batch, batch_major=batch_major),
        grid=(y.shape[0] // tb,),
        in_specs=[
            pl.BlockSpec((tb, d), lambda b: (b, 0)),
            h_spec,
            _layer_spec(layer, d, d), _layer_spec(layer, 1, d), _layer_spec(layer, d, LANES),
            _layer_spec(layer, n_exp, 1),
        ],
        out_specs=[
            pl.BlockSpec((tb, d), lambda b: (b, 0)),
            pl.BlockSpec((_local_rows(n_exp), d), lambda b: (b, 0)),
            tok, tok,
            pl.BlockSpec((1, SUBLANES, n_exp), lambda b: (b, 0, 0)),
        ],
        out_shape=[
            jax.ShapeDtypeStruct((n_total, d), F32),
            jax.ShapeDtypeStruct((_grouped_rows(nb, n_exp), d), BF16),
            jax.ShapeDtypeStruct((SUBLANES, n_total), F32),
            jax.ShapeDtypeStruct((SUBLANES, n_total), I32),
            jax.ShapeDtypeStruct((nb, SUBLANES, n_exp), F32),
        ],
        scratch_shapes=[pltpu.VMEM((d // LANES, tb, LANES), F32)],
        compiler_params=pltpu.CompilerParams(dimension_semantics=("arbitrary",), vmem_limit_bytes=VMEM_LIMIT),
        name="post_router",
    )(y, h, w["w_out"], w["g_ffn"], w["w_r"], w["b_r"])


def _max_sorted_rows(n_valid, nb, n_exp):
    worst = TOP_K * n_valid + (SEG_ALIGN - 1) * nb * n_exp + (EXPERT_TILE - SEG_ALIGN) * n_exp
    return _round_up(worst, EXPERT_TILE)


def _local_rows(n_exp):
    return _round_up(TOP_K * TOKEN_BLOCK + (SEG_ALIGN - 1) * n_exp, LANES)


def _dump_rows(n_exp):
    return n_exp * EXPERT_SUBTILE


def _grouped_rows(nb, n_exp):
    return nb * _local_rows(n_exp) + _dump_rows(n_exp)


def _select_columns(per_block_expert, expert_of):
    n_exp = per_block_expert.shape[1]
    one_hot = (expert_of[:, None] == jnp.arange(n_exp, dtype=I32)[None, :]).astype(F32)
    picked = jnp.dot(one_hot, per_block_expert.T.astype(F32), precision=lax.Precision.HIGHEST)
    return jnp.round(picked).astype(I32)


def _route_meta(cnt, max_tiles):
    nb, n_exp = cnt.shape
    mp = _local_rows(n_exp)
    p8 = (cnt + SEG_ALIGN - 1) // SEG_ALIGN * SEG_ALIGN
    off = jnp.cumsum(p8, axis=1) - p8
    tot = jnp.sum(p8, axis=1)
    seg = jnp.sum(p8, axis=0)
    reg = (seg + EXPERT_TILE - 1) // EXPERT_TILE * EXPERT_TILE
    reg_start = jnp.cumsum(reg) - reg
    tiles_end = jnp.cumsum(reg // EXPERT_TILE)
    tile_ids = jnp.arange(max_tiles, dtype=I32)
    tile_expert = jnp.minimum(jnp.sum((tiles_end[None, :] <= tile_ids[:, None]).astype(I32), axis=1), n_exp - 1)
    of_tile = tile_expert[:, None] == jnp.arange(n_exp, dtype=I32)[None, :]
    pick = lambda per_expert: jnp.sum(jnp.where(of_tile, per_expert[None, :], 0), axis=1)
    used = jnp.clip(pick(seg) - (tile_ids - pick(tiles_end - reg // EXPERT_TILE)) * EXPERT_TILE, 0, EXPERT_TILE)
    tile_rows = (used + EXPERT_SUBTILE - 1) // EXPERT_SUBTILE * EXPERT_SUBTILE

    pieces_per_tile = EXPERT_TILE // SEG_ALIGN
    piece_expert = jnp.repeat(tile_expert, pieces_per_tile)
    piece_tile = jnp.repeat(tile_ids, pieces_per_tile)
    row = jnp.arange(max_tiles * pieces_per_tile, dtype=I32) * SEG_ALIGN
    is_e = piece_expert[:, None] == jnp.arange(n_exp, dtype=I32)[None, :]
    of_expert = lambda per_expert: jnp.sum(jnp.where(is_e, per_expert[None, :], 0), axis=1)
    rin = row - of_expert(reg_start)
    seg_e = of_expert(seg)
    ends = _select_columns(jnp.cumsum(p8, axis=0), piece_expert)
    blk = jnp.minimum(jnp.sum((ends <= rin[:, None]).astype(I32), axis=1), nb - 1)
    is_b = blk[:, None] == jnp.arange(nb, dtype=I32)[None, :]
    of_block = lambda a: jnp.sum(jnp.where(is_b, a, 0), axis=1)
    seg_first = of_block(ends - _select_columns(p8, piece_expert))
    local = of_block(_select_columns(off, piece_expert)) + rin - seg_first
    pad_k = jnp.clip((rin - seg_e) // SEG_ALIGN, 0, EXPERT_SUBTILE // SEG_ALIGN - 1)
    real = (rin < seg_e) & (piece_tile < tiles_end[-1])
    piece = jnp.where(real, blk * mp + local, -1 - (piece_expert * (EXPERT_SUBTILE // SEG_ALIGN) + pad_k))
    return dict(piece=piece.astype(I32), tot=tot.astype(I32),
                tile_expert=tile_expert.astype(I32), n_tiles=tiles_end[-1:].astype(I32),
                tiles_end=tiles_end.astype(I32), tile_rows=tile_rows.astype(I32))


def _wait_rows(n_rows, min_rows, max_rows, src_rows, dst_rows, sem):
    size = min_rows
    while size <= max_rows:
        @pl.when((n_rows & size) != 0)
        def _(size=size):
            pltpu.make_async_copy(src_rows(size), dst_rows(size), sem).wait()
        size *= 2


def _expert_kernel(te_ref, nt_ref, tend_ref, rows_ref, piece_ref,
                   xg_hbm, w1_hbm, b1_ref, w2_hbm, b2_ref, yg_hbm,
                   xbuf, ybuf, w1s, w2s, w1b, w2b, xsem, ysem, sems, n_loaded, *, layer, zero_row, dump_row):
    n_tiles = nt_ref[0]
    dff = w2b.shape[0]
    tm = xbuf.shape[1]
    pieces = tm // SEG_ALIGN

    def weight_copies(expert, slot):
        return (pltpu.make_async_copy(w1_hbm.at[layer, expert], w1s.at[slot], sems.at[0, slot]),
                pltpu.make_async_copy(w2_hbm.at[layer, expert], w2s.at[slot], sems.at[1, slot]))

    def gather(tile, sl, exists=True):
        for j in range(pieces):
            p = piece_ref[tile * pieces + j]
            row = pl.multiple_of(jnp.where((p >= 0) & exists, p, zero_row), SEG_ALIGN)
            pltpu.make_async_copy(xg_hbm.at[pl.ds(row, SEG_ALIGN), :],
                                  xbuf.at[sl, pl.ds(j * SEG_ALIGN, SEG_ALIGN), :], xsem.at[sl]).start()

    def scatter(tile, sl, rows):
        for j in range(rows // SEG_ALIGN):
            p = piece_ref[tile * pieces + j]
            row = pl.multiple_of(jnp.where(p >= 0, p, dump_row + (-1 - p) * SEG_ALIGN), SEG_ALIGN)
            pltpu.make_async_copy(ybuf.at[sl, pl.ds(j * SEG_ALIGN, SEG_ALIGN), :],
                                  yg_hbm.at[pl.ds(row, SEG_ALIGN), :], ysem.at[sl]).start()

    def wait_gather(sl):
        pltpu.make_async_copy(xg_hbm.at[pl.ds(0, tm), :], xbuf.at[sl], xsem.at[sl]).wait()

    def wait_scatter(tile, sl):
        _wait_rows(rows_ref[tile], EXPERT_SUBTILE, tm, lambda n: ybuf.at[sl, pl.ds(0, n), :],
                   lambda n: yg_hbm.at[pl.ds(0, n), :], ysem.at[sl])

    def tile_step(i, carry):
        slot = i % 2
        expert = te_ref[i]

        @pl.when(i == 0)
        def _():
            n_loaded[0] = 0
            for c in weight_copies(expert, 0):
                c.start()
            gather(0, 0)

        wait_gather(slot)

        @pl.when(i >= 2)
        def _():
            wait_scatter(jnp.maximum(i - 2, 0), slot)

        @pl.when((i == 0) | (expert != te_ref[jnp.maximum(i - 1, 0)]))
        def _():
            slot = n_loaded[0] % 2
            n_loaded[0] = n_loaded[0] + 1
            for c in weight_copies(expert, slot):
                c.wait()
            nxt = tend_ref[expert]

            @pl.when(nxt < n_tiles)
            def _():
                for c in weight_copies(te_ref[jnp.minimum(nxt, n_tiles - 1)], 1 - slot):
                    c.start()
            w1b[...] = w1s[slot].astype(BF16)
            w2b[...] = w2s[slot].astype(BF16)

        nxt_tile = jnp.minimum(i + 1, n_tiles - 1)
        for rows in range(EXPERT_SUBTILE, tm + 1, EXPERT_SUBTILE):
            @pl.when(rows_ref[i] == rows)
            def _(rows=rows):
                gather(nxt_tile, 1 - slot, i + 1 < n_tiles)
                hdn = _dot(xbuf[slot, 0:rows, :], w1b[...]) + b1_ref[expert]
                gate = jnp.minimum(hdn[:, 0:dff], SWIGLU_LIMIT)
                up = jnp.clip(hdn[:, dff:2 * dff], -SWIGLU_LIMIT, SWIGLU_LIMIT)
                act = (up + 1.0) * gate * _sigmoid(SWIGLU_ALPHA * gate)
                ybuf[slot, 0:rows, :] = (_dot(act.astype(BF16), w2b[...]) + b2_ref[expert]).astype(BF16)
                scatter(i, slot, rows)

        @pl.when(i == n_tiles - 1)
        def _():
            wait_gather(1 - slot)
            wait_scatter(i, slot)

            @pl.when(i >= 1)
            def _():
                wait_scatter(jnp.maximum(i - 1, 0), 1 - slot)
        return carry

    lax.fori_loop(0, n_tiles, tile_step, 0)


def _experts(xg, meta, layer, w1, b1, w2, b2, n_exp):
    rows_total, d = xg.shape
    dff2 = w1.shape[-1]
    dff = dff2 // 2
    tm = EXPERT_TILE
    mp = _local_rows(n_exp)
    return pl.pallas_call(
        functools.partial(_expert_kernel, layer=layer, zero_row=mp - SEG_ALIGN,
                          dump_row=rows_total - _dump_rows(n_exp)),
        grid_spec=pltpu.PrefetchScalarGridSpec(
            num_scalar_prefetch=5,
            grid=(1,),
            in_specs=[
                pl.BlockSpec(memory_space=pl.ANY),
                pl.BlockSpec(memory_space=pl.ANY),
                _layer_spec(layer, n_exp, 1, dff2),
                pl.BlockSpec(memory_space=pl.ANY),
                _layer_spec(layer, n_exp, 1, d),
            ],
            out_specs=pl.BlockSpec(memory_space=pl.ANY),
            scratch_shapes=[
                pltpu.VMEM((2, tm, d), BF16), pltpu.VMEM((2, tm, d), BF16),
                pltpu.VMEM((2, d, dff2), F32), pltpu.VMEM((2, dff, d), F32),
                pltpu.VMEM((d, dff2), BF16), pltpu.VMEM((dff, d), BF16),
                pltpu.SemaphoreType.DMA((2,)), pltpu.SemaphoreType.DMA((2,)),
                pltpu.SemaphoreType.DMA((2, 2)),
                pltpu.SMEM((1,), I32),
            ],
        ),
        out_shape=jax.ShapeDtypeStruct(xg.shape, xg.dtype),
        input_output_aliases={5: 0},
        compiler_params=pltpu.CompilerParams(dimension_semantics=("arbitrary",), vmem_limit_bytes=VMEM_LIMIT),
        name="moe_experts",
    )(meta["tile_expert"], meta["n_tiles"], meta["tiles_end"], meta["tile_rows"], meta["piece"],
      xg, w1, b1, w2, b2)


def _combine_kernel(h2_ref, yg_ref, gw_ref, pos_ref, pp_ref, ps_ref, wpg_ref, bpg_ref, wpp_ref, gple_ref, gfin_ref,
                    *rest, final, batch):
    if final:
        outp_ref, outs_ref, prows, orows = rest
    else:
        hn_ref, prows = rest
    b = pl.program_id(0)
    nb = pl.num_programs(0)
    tb = h2_ref.shape[0]
    mp = yg_ref.shape[0]

    pos = pos_ref[...]
    gw = gw_ref[...]
    jio = lax.broadcasted_iota(I32, (mp, tb), 0)
    cg = jnp.where(jio == pos[0:1, :], gw[0:1, :], 0.0)
    for k in range(1, TOP_K):
        cg = cg + jnp.where(jio == pos[k:k + 1, :], gw[k:k + 1, :], 0.0)
    moe = lax.dot_general(cg.astype(BF16), yg_ref[...], (((0,), (0,)), ((), ())),
                          preferred_element_type=F32)
    h3 = h2_ref[...] + moe
    n3 = _rms(h3, gple_ref[...]).astype(BF16)
    gate = _sigmoid(_dot(n3, wpg_ref[...]) + bpg_ref[...])
    p_rows = jnp.where(b == nb - 1, ps_ref[0], _time_major_rows(pp_ref.at[0], prows, batch))
    h4 = h3 + gate * _dot(p_rows.astype(BF16), wpp_ref[...])
    if final:
        out = _rms(h4, gfin_ref[...])

        @pl.when(b < nb - 1)
        def _():
            _batch_major_store(out, orows, outp_ref, batch)

        @pl.when(b == nb - 1)
        def _():
            outs_ref[...] = out
    else:
        hn_ref[...] = h4


def _combine(h2, yg, gw, pos, p_prompt, p_sample, layer, w, g_final, n_exp, final):
    n_total, d = h2.shape
    _, batch, seq, ple = p_prompt.shape
    tb = TOKEN_BLOCK
    nb = n_total // tb
    steps = tb // batch
    const2 = lambda b: (0, 0)
    tok = pl.BlockSpec((SUBLANES, tb), lambda b: (0, b))
    row = pl.BlockSpec((tb, d), lambda b: (b, 0))
    prompt_blk = lambda b: jnp.minimum(b, nb - 2)
    if final:
        out_specs = [pl.BlockSpec((batch, steps, d), lambda b: (0, prompt_blk(b), 0)),
                     pl.BlockSpec((tb, d), const2)]
        out_shape = [jax.ShapeDtypeStruct((batch, seq, d), F32), jax.ShapeDtypeStruct((tb, d), F32)]
        extra_scratch = [pltpu.VMEM((d // LANES, tb, LANES), F32)]
    else:
        out_specs = [row]
        out_shape = [jax.ShapeDtypeStruct((n_total, d), F32)]
        extra_scratch = []
    return pl.pallas_call(
        functools.partial(_combine_kernel, final=final, batch=batch),
        grid=(nb,),
        in_specs=[
            row,
            pl.BlockSpec((_local_rows(n_exp), d), lambda b: (b, 0)),
            tok, tok,
            pl.BlockSpec((1, batch, steps, ple), lambda b: (layer, 0, prompt_blk(b), 0)),
            pl.BlockSpec((1, tb, ple), lambda b: (layer, 0, 0)),
            _layer_spec(layer, d, d), _layer_spec(layer, 1, d), _layer_spec(layer, ple, d),
            _layer_spec(layer, 1, d),
            pl.BlockSpec((1, d), const2),
        ],
        out_specs=out_specs,
        scratch_shapes=[pltpu.VMEM((ple // LANES, tb, LANES), F32)] + extra_scratch,
        out_shape=out_shape,
        compiler_params=pltpu.CompilerParams(dimension_semantics=("arbitrary",), vmem_limit_bytes=VMEM_LIMIT),
        name="moe_combine_ple",
    )(h2, yg, gw, pos, p_prompt, p_sample, w["w_pg"], w["b_pg"], w["w_pp"], w["g_ple"], g_final)


def _prepare_weights(batch, g_mix, w_in, g_v, w_s, b_s, conv_w, conv_b, w_a, b_a, w_x, b_x, lam,
                     g_oa, g_ob, w_out, g_ffn, w_r, b_r, g_ple, w_pg, b_pg, w_pp):
    depth, d = w_in.shape[0], w_in.shape[1]
    dh = d // 2
    gd = dh // GMLP_GROUPS
    n_exp = w_r.shape[2]
    row = lambda a: a.reshape(depth, 1, -1).astype(F32)
    wt = jnp.where(jnp.tril(jnp.ones((CHUNK, CHUNK), bool)), w_s, 0.0).astype(BF16)
    hr = HALF * batch
    row_t = jnp.arange(hr, dtype=I32) // batch
    row_b = jnp.arange(hr, dtype=I32) % batch
    expand_t = (row_t[:, None] == jnp.arange(HALF, dtype=I32)[None, :]).astype(BF16)
    same_b = row_b[:, None] == row_b[None, :]

    def kron_block(blk):
        rows = jnp.einsum("it,lgts->lgis", expand_t, blk, preferred_element_type=F32).astype(BF16)
        full = jnp.einsum("lgis,js->lgij", rows, expand_t, preferred_element_type=F32)
        return jnp.where(same_b, full, 0.0).astype(BF16)

    kd = jnp.stack([kron_block(wt[:, :, :HALF, :HALF]), kron_block(wt[:, :, HALF:, HALF:])], axis=1)
    k10 = kron_block(wt[:, :, HALF:, :HALF])
    bs_rows = jnp.repeat(jnp.repeat(b_s.transpose(0, 2, 1).astype(F32), gd, axis=2), batch, axis=1).reshape(
        depth, 2, hr, dh)
    hd = dh // LRU_HEADS
    head_of = jnp.arange(dh, dtype=I32) // hd
    same_head = head_of[:, None] == head_of[None, :]
    block_diag = lambda w: jnp.where(same_head, jnp.tile(w.astype(F32).reshape(depth, dh, hd), (1, 1, LRU_HEADS)),
                                     0.0)
    wa_f32, wx_f32 = block_diag(w_a), block_diag(w_x)
    return dict(
        g_mix=row(g_mix), w_in=w_in.astype(BF16), w_in_f32=w_in.astype(F32), g_v=row(g_v),
        kd=kd, k10=k10, bs_rows=bs_rows,
        w0_row=row(jnp.repeat(w_s[:, :, 0, 0], gd, axis=1)), b0_row=row(jnp.repeat(b_s[:, :, 0], gd, axis=1)),
        conv_w=conv_w.astype(F32), conv_b=row(conv_b),
        wa=wa_f32.astype(BF16), wa_f32=wa_f32, b_a=row(b_a), wx=wx_f32.astype(BF16), wx_f32=wx_f32,
        b_x=row(b_x), lam=row(lam),
        g_oa=row(g_oa), g_ob=row(g_ob), w_out=w_out.astype(BF16), w_out_f32=w_out.astype(F32),
        g_ffn=row(g_ffn),
        w_r=jnp.pad(w_r.astype(F32), ((0, 0), (0, 0), (0, LANES - n_exp))),
        b_r=b_r.reshape(depth, n_exp, 1).astype(F32),
        g_ple=row(g_ple), w_pg=w_pg.astype(BF16), b_pg=row(b_pg), w_pp=w_pp.astype(BF16))


def _layer_spec(layer, *shape):
    return pl.BlockSpec((None,) + shape, lambda *_: (layer,) + (0,) * len(shape))


def kernel(x_prompt, x_sample, state_conv, state_lru, p_prompt, p_sample, g_mix, w_in, g_v, w_s, b_s, conv_w, conv_b, w_a, b_a, w_x, b_x, lam, g_oa, g_ob, w_out, g_ffn, w_r, b_r, w1, b1, w2, b2, g_ple, w_pg, b_pg, w_pp, g_final):
    batch, seq, d = x_prompt.shape
    dec_batch, dec_seq, _ = x_sample.shape
    depth = w_in.shape[0]
    n_exp = w_r.shape[2]
    dh = d // 2
    ple = p_prompt.shape[-1]
    assert dec_seq == 1 and seq % CHUNK == 0 and batch % SUBLANES == 0
    assert dh == GMLP_GROUPS * LANES and dh % (LRU_HEADS * (LANES // 2)) == 0
    n_prompt = seq * batch
    assert n_prompt % TOKEN_BLOCK == 0
    n_valid = n_prompt + dec_batch
    n_total = _round_up(n_valid, TOKEN_BLOCK)
    s_rows = n_total - n_prompt
    assert s_rows == TOKEN_BLOCK
    nb = n_total // TOKEN_BLOCK
    max_tiles = _max_sorted_rows(n_valid, nb, n_exp) // EXPERT_TILE
    pad_rows = ((0, 0), (0, s_rows - dec_batch), (0, 0))

    h, h_sample, h_sample_block = x_prompt, jnp.pad(x_sample.reshape(1, dec_batch, d), pad_rows)[0], 0
    p_samp = jnp.pad(p_sample.reshape(depth, dec_batch, ple), pad_rows)
    sconv = jnp.pad(state_conv.transpose(0, 2, 1, 3), ((0, 0),) + pad_rows)
    slru = jnp.pad(state_lru, pad_rows)
    g_fin = g_final.reshape(1, d).astype(F32)
    b1_rows = b1.reshape(depth, n_exp, 1, -1)
    b2_rows = b2.reshape(depth, n_exp, 1, -1)

    w = _prepare_weights(batch, g_mix, w_in, g_v, w_s, b_s, conv_w, conv_b, w_a, b_a, w_x, b_x, lam,
                         g_oa, g_ob, w_out, g_ffn, w_r, b_r, g_ple, w_pg, b_pg, w_pp)
    v_p, conv_p, lru_p, v_s, conv_s, lru_s = [], [], [], [], [], []
    for l in range(depth):
        y, vp, cp, lp = _mix_prompt(h, n_prompt, batch, w, l)
        post_out = _post(y, h, w, l, n_total, n_valid, n_exp, batch)
        h2, xg, gw, pos, cnt, vs, cs, ls = _sample_front(
            h_sample, h_sample_block, post_out, nb - 1, w, l, sconv, slru, n_valid, n_exp)
        meta = _route_meta(cnt[:, 0, :].astype(I32), max_tiles)
        yg = _experts(xg, meta, l, w1, b1_rows, w2, b2_rows, n_exp)
        final = l == depth - 1
        res = _combine(h2, yg, gw, pos, p_prompt, p_samp, l, w, g_fin, n_exp, final)
        if final:
            y_prompt, out_sample = res
        else:
            h = h_sample = res[0]
            h_sample_block = nb - 1
        for acc, val in zip((v_p, conv_p, lru_p, v_s, conv_s, lru_s), (vp, cp, lp, vs, cs, ls)):
            acc.append(val)

    y_sample = out_sample[:dec_batch].reshape(dec_batch, 1, d)
    v_prompt = jnp.stack(v_p).reshape(depth, CHUNK, batch, dh).transpose(0, 2, 1, 3)
    conv_prompt = jnp.stack(conv_p).reshape(depth, CONV_W - 1, batch, dh).transpose(0, 2, 1, 3)
    v_sample = jnp.stack(v_s)[:, :dec_batch].reshape(depth, dec_batch, 1, dh)
    conv_sample = jnp.stack(conv_s)[:, :, :dec_batch].transpose(0, 2, 1, 3)
    return (y_prompt, y_sample, v_prompt, conv_prompt, jnp.stack(lru_p),
            v_sample, conv_sample, jnp.stack(lru_s)[:, :dec_batch])
```

```python
import functools

import jax
import jax.numpy as jnp
from jax import lax
from jax.experimental import pallas as pl
from jax.experimental.pallas import tpu as pltpu

F32 = jnp.float32
BF16 = jnp.bfloat16
I32 = jnp.int32

CHUNK = 128
HALF = CHUNK // 2
GMLP_GROUPS = 4
LRU_HEADS = 8
CONV_W = 4
LRU_C = 8.0
TOP_K = 4
SWIGLU_LIMIT = 7.0
SWIGLU_ALPHA = 1.702
EPS = 1e-6

LANES = 128
SUBLANES = 8
TOKEN_BLOCK = 256
EXPERT_TILE = 512
EXPERT_SUBTILE = 128
SEG_ALIGN = SUBLANES
VMEM_LIMIT = 56 * 1024 * 1024


def _round_up(x, m):
    return (x + m - 1) // m * m


def _dot(a, b):
    return jnp.dot(a, b, preferred_element_type=F32)


def _gelu(x):
    return 0.5 * x * (1.0 + jnp.tanh(0.7978845608028654 * (x + 0.044715 * (x * x * x))))


def _sigmoid(x):
    return 1.0 / (1.0 + jnp.exp(-x))


def _softplus(x):
    return jnp.maximum(x, 0.0) + jnp.log1p(jnp.exp(-jnp.abs(x)))


def _rms(x, g):
    ms = jnp.mean(x * x, axis=-1, keepdims=True)
    return x * lax.rsqrt(ms + EPS) * g


def _group_norm_128(x, g):
    outs = []
    for j in range(x.shape[1] // LANES):
        blk = x[:, j * LANES:(j + 1) * LANES]
        ms = jnp.mean(blk * blk, axis=-1, keepdims=True)
        outs.append(blk * lax.rsqrt(ms + EPS))
    return jnp.concatenate(outs, axis=1) * g


def _group_norm_64(x, g):
    half = LANES // 2
    lo_mask = lax.broadcasted_iota(I32, (1, LANES), 1) < half
    outs = []
    for j in range(x.shape[1] // LANES):
        blk = x[:, j * LANES:(j + 1) * LANES]
        sq = blk * blk
        lo = jnp.sum(jnp.where(lo_mask, sq, 0.0), axis=-1, keepdims=True)
        hi = jnp.sum(jnp.where(lo_mask, 0.0, sq), axis=-1, keepdims=True)
        ms = jnp.where(lo_mask, lo, hi) * (1.0 / half)
        outs.append(blk * lax.rsqrt(ms + EPS))
    return jnp.concatenate(outs, axis=1) * g


def _time_major_rows(src_ref, slabs_ref, batch):
    steps = src_ref.shape[1]
    n_slab = slabs_ref.shape[0]
    for b in range(batch):
        blk = src_ref[b]
        for j in range(n_slab):
            slabs_ref[j, pl.ds(b, steps, stride=batch), :] = blk[:, j * LANES:(j + 1) * LANES]
    return jnp.concatenate([slabs_ref[j] for j in range(n_slab)], axis=1)


def _batch_major_store(rows, slabs_ref, dst_ref, batch):
    steps = dst_ref.shape[1]
    n_slab = slabs_ref.shape[0]
    for j in range(n_slab):
        slabs_ref[j] = rows[:, j * LANES:(j + 1) * LANES]
    for b in range(batch):
        dst_ref[b] = jnp.concatenate(
            [slabs_ref[j, pl.ds(b, steps, stride=batch), :] for j in range(n_slab)], axis=1)


def _split_bf16(x):
    hi = x.astype(BF16)
    lo = (x - hi.astype(F32)).astype(BF16)
    return hi, lo


def _dot_bf16(x, w):
    return _dot(x.astype(BF16), w)


def _dot_3pass(x, w):
    x_hi, x_lo = _split_bf16(x)
    w_hi, w_lo = _split_bf16(w)
    return _dot(x_hi, w_hi) + (_dot(x_lo, w_hi) + _dot(x_hi, w_lo))


def _lru_gates(xc, wa_ref, ba_ref, wx_ref, bx_ref, lam_ref, mm):
    r = _sigmoid(mm(xc, wa_ref[...]) + ba_ref[...])
    i = _sigmoid(mm(xc, wx_ref[...]) + bx_ref[...])
    log_a = (-LRU_C * r) * _softplus(-lam_ref[...])
    a = jnp.exp(log_a)
    mult = jnp.sqrt(-jnp.tanh(log_a) * (a * a + 1.0))
    return a, mult, i


def _mix_prompt_kernel(h_ref, gmix_ref, win_ref, gv_ref, kd_ref, k10_ref, bs_ref, cw_ref, cb_ref,
                       wa_ref, ba_ref, wx_ref, bx_ref, lam_ref, goa_ref, gob_ref,
                       y_ref, v_ref, conv_ref, lru_ref,
                       hstate, xpad, vprev, a_s, b_s, hs_s, hrows, z_even, z_odd, *, batch, batch_major):
    c = pl.program_id(0)
    rows = y_ref.shape[0]
    dh = gv_ref.shape[1]
    tail = (CONV_W - 1) * batch

    @pl.when(c == 0)
    def _():
        z_odd[...] = jnp.zeros_like(z_odd)
        vprev[...] = jnp.zeros_like(vprev)

    @pl.when(c <= 1)
    def _():
        hstate[...] = jnp.zeros_like(hstate)
        xpad[0:tail, :] = jnp.zeros((tail, dh), F32)

    def tile_step(z_in, z_out, par):
        if par == 0:
            vprev[...] = jnp.zeros_like(vprev)

        vn = _group_norm_128(_gelu(z_in[:, dh:2 * dh]), gv_ref[...])
        vb = vn.astype(BF16)
        s_parts = []
        for g in range(GMLP_GROUPS):
            sl = slice(g * LANES, (g + 1) * LANES)
            s_parts.append(_dot(kd_ref[par, g], vb[:, sl]) + _dot(k10_ref[g], vprev[:, sl]))
        s = jnp.concatenate(s_parts, axis=1) + bs_ref[par]
        vprev[...] = vb
        v_ref[...] = vn
        y_ref[:, 0:dh] = _group_norm_128(_gelu(z_in[:, 0:dh]) * s, goa_ref[...]).astype(BF16)

        xpad[tail:tail + rows, :] = z_in[:, 2 * dh:3 * dh]
        xc = cb_ref[...] + cw_ref[0:1, :] * xpad[0:rows, :]
        for k in range(1, CONV_W):
            xc = xc + cw_ref[k:k + 1, :] * xpad[k * batch:k * batch + rows, :]
        new_tail = xpad[rows:rows + tail, :]
        xpad[0:tail, :] = new_tail

        a, mult, i = _lru_gates(xc, wa_ref, ba_ref, wx_ref, bx_ref, lam_ref, _dot_bf16)
        row = lax.broadcasted_iota(I32, (rows, 1), 0)
        mult = jnp.where((c == 1) & (row < batch), 1.0, mult)
        a_s[...] = a
        b_s[...] = mult * i * xc

        def step(t, h):
            r0 = t * batch
            h = a_s[r0:r0 + batch, :] * h + b_s[r0:r0 + batch, :]
            hs_s[r0:r0 + batch, :] = h
            return h

        h_last = hstate[...]
        for t in range(rows // batch):
            h_last = step(t, h_last)
        hstate[...] = h_last

        y_ref[:, dh:2 * dh] = _group_norm_64(hs_s[...] * _gelu(z_in[:, 3 * dh:4 * dh]), gob_ref[...]).astype(BF16)
        conv_ref[...] = new_tail
        lru_ref[...] = h_last

        h = _time_major_rows(h_ref, hrows, batch) if batch_major else h_ref[...]
        z_out[...] = _dot(_rms(h, gmix_ref[...]).astype(BF16), win_ref[...])

    @pl.when(c % 2 == 0)
    def _():
        tile_step(z_odd, z_even, 1)

    @pl.when(c % 2 == 1)
    def _():
        tile_step(z_even, z_odd, 0)


def _mix_prompt(h, n_rows, batch, w, layer):
    batch_major = h.ndim == 3
    d = h.shape[-1]
    dh = d // 2
    rows = HALF * batch
    n_steps = n_rows // rows
    tail = (CONV_W - 1) * batch
    const2 = lambda c: (0, 0)
    lw = functools.partial(_layer_spec, layer)
    proj_tile = lambda c: jnp.minimum(c, n_steps - 1)
    mix_tile = lambda c: jnp.maximum(c - 1, 0)
    h_spec = (pl.BlockSpec((batch, HALF, d), lambda c: (0, proj_tile(c), 0)) if batch_major
              else pl.BlockSpec((rows, d), lambda c: (proj_tile(c), 0)))
    return pl.pallas_call(
        functools.partial(_mix_prompt_kernel, batch=batch, batch_major=batch_major),
        grid=(n_steps + 1,),
        in_specs=[
            h_spec,
            lw(1, d), lw(d, 2 * d), lw(1, dh),
            lw(2, GMLP_GROUPS, rows, rows), lw(GMLP_GROUPS, rows, rows), lw(2, rows, dh),
            lw(CONV_W, dh), lw(1, dh),
            lw(dh, dh), lw(1, dh), lw(dh, dh), lw(1, dh),
            lw(1, dh), lw(1, dh), lw(1, dh),
        ],
        out_specs=[
            pl.BlockSpec((rows, d), lambda c: (mix_tile(c), 0)),
            pl.BlockSpec((rows, dh), lambda c: (jnp.maximum(mix_tile(c) - (n_steps - 2), 0), 0)),
            pl.BlockSpec((tail, dh), const2),
            pl.BlockSpec((batch, dh), const2),
        ],
        out_shape=[
            jax.ShapeDtypeStruct((n_rows, d), BF16),
            jax.ShapeDtypeStruct((2 * rows, dh), F32),
            jax.ShapeDtypeStruct((tail, dh), F32),
            jax.ShapeDtypeStruct((batch, dh), F32),
        ],
        scratch_shapes=[
            pltpu.VMEM((batch, dh), F32),
            pltpu.VMEM((tail + rows, dh), F32),
            pltpu.VMEM((rows, dh), BF16),
            pltpu.VMEM((rows, dh), F32),
            pltpu.VMEM((rows, dh), F32),
            pltpu.VMEM((rows, dh), F32),
            pltpu.VMEM((d // LANES, rows, LANES), F32),
            pltpu.VMEM((rows, 2 * d), F32),
            pltpu.VMEM((rows, 2 * d), F32),
        ],
        compiler_params=pltpu.CompilerParams(dimension_semantics=("arbitrary",), vmem_limit_bytes=VMEM_LIMIT),
        name="mix_prompt",
    )(h, w["g_mix"], w["w_in"], w["g_v"], w["kd"], w["k10"], w["bs_rows"], w["conv_w"], w["conv_b"],
      w["wa"], w["b_a"], w["wx"], w["b_x"], w["lam"], w["g_oa"], w["g_ob"])


def _sample_front_kernel(h_ref, h2_any, xg_any, gw_any, pos_any, cnt_any,
                         gmix_ref, win_ref, gv_ref, w0_ref, b0_ref, cw_ref, cb_ref,
                         wa_ref, ba_ref, wx_ref, bx_ref, lam_ref, goa_ref, gob_ref, sconv_ref, slru_ref,
                         wout_ref, gffn_ref, wr_ref, br_ref,
                         h2_ref, xg_ref, gw_ref, pos_ref, cnt_ref, v_ref, conv_ref, lru_ref,
                         *, block, n_valid, n_exp):
    del h2_any, xg_any, gw_any, pos_any, cnt_any
    dh = gv_ref.shape[1]
    h = h_ref[...]
    n = _rms(h, gmix_ref[...])
    proj = lambda lo: _dot_3pass(n, win_ref[:, lo:lo + dh])

    vn = _group_norm_128(_gelu(proj(dh)), gv_ref[...])
    v_ref[...] = vn
    s = vn * w0_ref[...] + b0_ref[...]
    ya = _group_norm_128(_gelu(proj(0)) * s, goa_ref[...])

    xb = proj(2 * dh)
    xc = cb_ref[...] + cw_ref[CONV_W - 1:CONV_W, :] * xb
    for k in range(CONV_W - 1):
        xc = xc + cw_ref[k:k + 1, :] * sconv_ref[k]
    for k in range(CONV_W - 2):
        conv_ref[k] = sconv_ref[k + 1]
    conv_ref[CONV_W - 2] = xb

    a, mult, i = _lru_gates(xc, wa_ref, ba_ref, wx_ref, bx_ref, lam_ref, _dot_3pass)
    h_new = a * slru_ref[...] + mult * i * xc
    lru_ref[...] = h_new
    yb = _group_norm_64(h_new * _gelu(proj(3 * dh)), gob_ref[...])

    h2 = h + _dot_3pass(ya, wout_ref[0:dh, :]) + _dot_3pass(yb, wout_ref[dh:2 * dh, :])
    _route_block(h2, gffn_ref, wr_ref, br_ref, h2_ref, xg_ref, gw_ref, pos_ref, cnt_ref,
                 block=block, n_valid=n_valid, n_exp=n_exp)


def _sample_front(h, h_block, post_out, block, w, layer, sconv, slru, n_valid, n_exp):
    d = h.shape[1]
    dh = d // 2
    rows = TOKEN_BLOCK
    const2 = lambda i: (0, 0)
    const3 = lambda i: (0, 0, 0)
    lw = functools.partial(_layer_spec, layer)
    vec = lw(1, dh)
    anyspace = pl.BlockSpec(memory_space=pl.ANY)
    blk_row = pl.BlockSpec((rows, d), lambda i: (block, 0))
    blk_tok = pl.BlockSpec((SUBLANES, rows), lambda i: (0, block))
    return pl.pallas_call(
        functools.partial(_sample_front_kernel, block=block, n_valid=n_valid, n_exp=n_exp),
        grid=(1,),
        in_specs=[
            pl.BlockSpec((rows, d), lambda i: (h_block, 0)),
            anyspace, anyspace, anyspace, anyspace, anyspace,
            lw(1, d), lw(d, 2 * d),
            vec, vec, vec,
            lw(CONV_W, dh), vec,
            lw(dh, dh), vec, lw(dh, dh), vec,
            vec, vec, vec,
            lw(CONV_W - 1, rows, dh), lw(rows, dh),
            lw(d, d), lw(1, d), lw(d, LANES), lw(n_exp, 1),
        ],
        out_specs=[
            blk_row, pl.BlockSpec((_local_rows(n_exp), d), lambda i: (block, 0)), blk_tok, blk_tok,
            pl.BlockSpec((1, SUBLANES, n_exp), lambda i: (block, 0, 0)),
            pl.BlockSpec((rows, dh), const2),
            pl.BlockSpec((CONV_W - 1, rows, dh), const3),
            pl.BlockSpec((rows, dh), const2),
        ],
        out_shape=[jax.ShapeDtypeStruct(a.shape, a.dtype) for a in post_out] + [
            jax.ShapeDtypeStruct((rows, dh), F32),
            jax.ShapeDtypeStruct((CONV_W - 1, rows, dh), F32),
            jax.ShapeDtypeStruct((rows, dh), F32),
        ],
        input_output_aliases={1 + k: k for k in range(len(post_out))},
        compiler_params=pltpu.CompilerParams(dimension_semantics=("arbitrary",), vmem_limit_bytes=VMEM_LIMIT),
        name="sample_front",
    )(h, *post_out, w["g_mix"], w["w_in_f32"], w["g_v"], w["w0_row"], w["b0_row"], w["conv_w"], w["conv_b"],
      w["wa_f32"], w["b_a"], w["wx_f32"], w["b_x"], w["lam"], w["g_oa"], w["g_ob"], sconv, slru,
      w["w_out_f32"], w["g_ffn"], w["w_r"], w["b_r"])


def _route_block(h2, gffn_ref, wr_ref, br_ref, h2_ref, xg_ref, gw_ref, pos_ref, cnt_ref,
                 *, block, n_valid, n_exp):
    tb = h2.shape[0]
    h2_ref[...] = h2
    n2 = _rms(h2, gffn_ref[...])

    n_hi, n_lo = _split_bf16(n2)
    w_hi, w_lo = _split_bf16(wr_ref[...])
    both = _dot(n_hi, jnp.concatenate([w_hi, w_lo], axis=1))
    logits = both[:, 0:LANES] + (_dot(n_lo, w_hi) + both[:, LANES:2 * LANES])
    lt = logits.T[0:n_exp, :] + br_ref[...]

    eio = lax.broadcasted_iota(I32, (n_exp, tb), 0).astype(F32)
    col = block * tb + lax.broadcasted_iota(I32, (1, tb), 1)
    valid = col < n_valid
    vals, hots = [], []
    for _ in range(TOP_K):
        m = jnp.max(lt, axis=0, keepdims=True)
        sel = jnp.min(jnp.where(lt == m, eio, float(n_exp)), axis=0, keepdims=True)
        hot = eio == sel
        lt = jnp.where(hot, -jnp.inf, lt)
        vals.append(m)
        hots.append(hot)
    exps = [jnp.exp(v - vals[0]) for v in vals]
    den = exps[0]
    for e in exps[1:]:
        den = den + e

    occ = jnp.zeros((n_exp, tb), F32)
    for hot in hots:
        occ = occ + jnp.where(hot & valid, 1.0, 0.0)
    occ_b = occ.astype(BF16)
    before = (lax.broadcasted_iota(I32, (tb, tb), 0) < lax.broadcasted_iota(I32, (tb, tb), 1))
    ranks_e = _dot(occ_b, jnp.where(before, 1.0, 0.0).astype(BF16))
    cnt_ref[0] = lax.dot_general(jnp.ones((SUBLANES, tb), BF16), occ_b, (((1,), (1,)), ((), ())),
                                 preferred_element_type=F32)
    cnt_col = jnp.sum(occ, axis=1, keepdims=True)
    units = jnp.floor((cnt_col + (SEG_ALIGN - 1.0)) * (1.0 / SEG_ALIGN))
    earlier = (lax.broadcasted_iota(I32, (n_exp, n_exp), 1) < lax.broadcasted_iota(I32, (n_exp, n_exp), 0))
    seg_off = _dot(jnp.where(earlier, 1.0, 0.0).astype(BF16),
                   jnp.broadcast_to(units, (n_exp, LANES)).astype(BF16))[:, 0:1] * float(SEG_ALIGN)
    rows_e = ranks_e + seg_off

    rio = lax.broadcasted_iota(I32, (SUBLANES, tb), 0)
    gw_out = jnp.zeros((SUBLANES, tb), F32)
    pos_out = jnp.full((SUBLANES, tb), -1, I32)
    for k in range(TOP_K):
        pos_k = jnp.sum(jnp.where(hots[k], rows_e, 0.0), axis=0, keepdims=True).astype(I32)
        gw_out = jnp.where(rio == k, exps[k] / den, gw_out)
        pos_out = jnp.where(rio == k, jnp.where(valid, pos_k, -1), pos_out)
    gw_ref[...] = gw_out
    pos_ref[...] = pos_out

    jio = lax.broadcasted_iota(I32, (xg_ref.shape[0], tb), 0)
    hit = jio == pos_out[0:1, :]
    for k in range(1, TOP_K):
        hit = hit | (jio == pos_out[k:k + 1, :])
    xg_ref[...] = _dot(jnp.where(hit, 1.0, 0.0).astype(BF16), n_hi).astype(BF16)


def _post_kernel(y_ref, h_ref, wout_ref, gffn_ref, wr_ref, br_ref,
                 h2_ref, xg_ref, gw_ref, pos_ref, cnt_ref, hrows, *, n_valid, n_exp, batch, batch_major):
    h = _time_major_rows(h_ref, hrows, batch) if batch_major else h_ref[...]
    h2 = h + _dot(y_ref[...], wout_ref[...])
    _route_block(h2, gffn_ref, wr_ref, br_ref, h2_ref, xg_ref, gw_ref, pos_ref, cnt_ref,
                 block=pl.program_id(0), n_valid=n_valid, n_exp=n_exp)


def _post(y, h, w, layer, n_total, n_valid, n_exp, batch):
    batch_major = h.ndim == 3
    d = h.shape[-1]
    tb = TOKEN_BLOCK
    nb = n_total // tb
    const2 = lambda b: (0, 0)
    tok = pl.BlockSpec((SUBLANES, tb), lambda b: (0, b))
    h_spec = (pl.BlockSpec((batch, tb // batch, d), lambda b: (0, b, 0)) if batch_major
              else pl.BlockSpec((tb, d), lambda b: (b, 0)))
    return pl.pallas_call(
        functools.partial(_post_kernel, n_valid=n_valid, n_exp=n_exp, batch=batch, batch_major=batch_major),
        grid=(y.shape[0] // tb,),
        in_specs=[
            pl.BlockSpec((tb, d), lambda b: (b, 0)),
            h_spec,
            _layer_spec(layer, d, d), _layer_spec(layer, 1, d), _layer_spec(layer, d, LANES),
            _layer_spec(layer, n_exp, 1),
        ],
        out_specs=[
            pl.BlockSpec((tb, d), lambda b: (b, 0)),
            pl.BlockSpec((_local_rows(n_exp), d), lambda b: (b, 0)),
            tok, tok,
            pl.BlockSpec((1, SUBLANES, n_exp), lambda b: (b, 0, 0)),
        ],
        out_shape=[
            jax.ShapeDtypeStruct((n_total, d), F32),
            jax.ShapeDtypeStruct((_grouped_rows(nb, n_exp), d), BF16),
            jax.ShapeDtypeStruct((SUBLANES, n_total), F32),
            jax.ShapeDtypeStruct((SUBLANES, n_total), I32),
            jax.ShapeDtypeStruct((nb, SUBLANES, n_exp), F32),
        ],
        scratch_shapes=[pltpu.VMEM((d // LANES, tb, LANES), F32)],
        compiler_params=pltpu.CompilerParams(dimension_semantics=("arbitrary",), vmem_limit_bytes=VMEM_LIMIT),
        name="post_router",
    )(y, h, w["w_out"], w["g_ffn"], w["w_r"], w["b_r"])


def _max_sorted_rows(n_valid, nb, n_exp):
    worst = TOP_K * n_valid + (SEG_ALIGN - 1) * nb * n_exp + (EXPERT_TILE - SEG_ALIGN) * n_exp
    return _round_up(worst, EXPERT_TILE)


def _local_rows(n_exp):
    return _round_up(TOP_K * TOKEN_BLOCK + (SEG_ALIGN - 1) * n_exp, LANES)


def _dump_rows(n_exp):
    return n_exp * EXPERT_SUBTILE


def _grouped_rows(nb, n_exp):
    return nb * _local_rows(n_exp) + _dump_rows(n_exp)


def _select_columns(per_block_expert, expert_of):
    n_exp = per_block_expert.shape[1]
    one_hot = (expert_of[:, None] == jnp.arange(n_exp, dtype=I32)[None, :]).astype(F32)
    picked = jnp.dot(one_hot, per_block_expert.T.astype(F32), precision=lax.Precision.HIGHEST)
    return jnp.round(picked).astype(I32)


def _route_meta(cnt, max_tiles):
    nb, n_exp = cnt.shape
    mp = _local_rows(n_exp)
    p8 = (cnt + SEG_ALIGN - 1) // SEG_ALIGN * SEG_ALIGN
    off = jnp.cumsum(p8, axis=1) - p8
    tot = jnp.sum(p8, axis=1)
    seg = jnp.sum(p8, axis=0)
    reg = (seg + EXPERT_TILE - 1) // EXPERT_TILE * EXPERT_TILE
    reg_start = jnp.cumsum(reg) - reg
    tiles_end = jnp.cumsum(reg // EXPERT_TILE)
    tile_ids = jnp.arange(max_tiles, dtype=I32)
    tile_expert = jnp.minimum(jnp.sum((tiles_end[None, :] <= tile_ids[:, None]).astype(I32), axis=1), n_exp - 1)
    of_tile = tile_expert[:, None] == jnp.arange(n_exp, dtype=I32)[None, :]
    pick = lambda per_expert: jnp.sum(jnp.where(of_tile, per_expert[None, :], 0), axis=1)
    used = jnp.clip(pick(seg) - (tile_ids - pick(tiles_end - reg // EXPERT_TILE)) * EXPERT_TILE, 0, EXPERT_TILE)
    tile_rows = (used + EXPERT_SUBTILE - 1) // EXPERT_SUBTILE * EXPERT_SUBTILE

    pieces_per_tile = EXPERT_TILE // SEG_ALIGN
    piece_expert = jnp.repeat(tile_expert, pieces_per_tile)
    piece_tile = jnp.repeat(tile_ids, pieces_per_tile)
    row = jnp.arange(max_tiles * pieces_per_tile, dtype=I32) * SEG_ALIGN
    is_e = piece_expert[:, None] == jnp.arange(n_exp, dtype=I32)[None, :]
    of_expert = lambda per_expert: jnp.sum(jnp.where(is_e, per_expert[None, :], 0), axis=1)
    rin = row - of_expert(reg_start)
    seg_e = of_expert(seg)
    ends = _select_columns(jnp.cumsum(p8, axis=0), piece_expert)
    blk = jnp.minimum(jnp.sum((ends <= rin[:, None]).astype(I32), axis=1), nb - 1)
    is_b = blk[:, None] == jnp.arange(nb, dtype=I32)[None, :]
    of_block = lambda a: jnp.sum(jnp.where(is_b, a, 0), axis=1)
    seg_first = of_block(ends - _select_columns(p8, piece_expert))
    local = of_block(_select_columns(off, piece_expert)) + rin - seg_first
    pad_k = jnp.clip((rin - seg_e) // SEG_ALIGN, 0, EXPERT_SUBTILE // SEG_ALIGN - 1)
    real = (rin < seg_e) & (piece_tile < tiles_end[-1])
    home = blk * mp + local
    src = jnp.where(real, home, mp - SEG_ALIGN)
    dst = jnp.where(real, home, nb * mp + (piece_expert * (EXPERT_SUBTILE // SEG_ALIGN) + pad_k) * SEG_ALIGN)
    look_ahead = jnp.full((pieces_per_tile,), mp - SEG_ALIGN, I32)
    return dict(piece_src=jnp.concatenate([src.astype(I32), look_ahead]), piece_dst=dst.astype(I32),
                tot=tot.astype(I32),
                tile_expert=tile_expert.astype(I32), n_tiles=tiles_end[-1:].astype(I32),
                tiles_end=tiles_end.astype(I32), tile_rows=tile_rows.astype(I32))


def _wait_rows(n_rows, min_rows, max_rows, src_rows, dst_rows, sem):
    size = min_rows
    while size <= max_rows:
        @pl.when((n_rows & size) != 0)
        def _(size=size):
            pltpu.make_async_copy(src_rows(size), dst_rows(size), sem).wait()
        size *= 2


def _expert_kernel(te_ref, nt_ref, tend_ref, rows_ref, src_ref, dst_ref,
                   xg_hbm, w1_hbm, b1_ref, w2_hbm, b2_ref, yg_hbm,
                   xbuf, ybuf, w1s, w2s, w1b, w2b, xsem, ysem, sems, n_loaded, *, layer):
    n_tiles = nt_ref[0]
    dff = w2b.shape[0]
    tm = xbuf.shape[1]
    pieces = tm // SEG_ALIGN

    def weight_copies(expert, slot):
        return (pltpu.make_async_copy(w1_hbm.at[layer, expert], w1s.at[slot], sems.at[0, slot]),
                pltpu.make_async_copy(w2_hbm.at[layer, expert], w2s.at[slot], sems.at[1, slot]))

    def gather(tile, sl):
        for j in range(pieces):
            row = pl.multiple_of(src_ref[tile * pieces + j], SEG_ALIGN)
            pltpu.make_async_copy(xg_hbm.at[pl.ds(row, SEG_ALIGN), :],
                                  xbuf.at[sl, pl.ds(j * SEG_ALIGN, SEG_ALIGN), :], xsem.at[sl]).start()

    def scatter(tile, sl, rows):
        for j in range(rows // SEG_ALIGN):
            row = pl.multiple_of(dst_ref[tile * pieces + j], SEG_ALIGN)
            pltpu.make_async_copy(ybuf.at[sl, pl.ds(j * SEG_ALIGN, SEG_ALIGN), :],
                                  yg_hbm.at[pl.ds(row, SEG_ALIGN), :], ysem.at[sl]).start()

    def wait_gather(sl):
        pltpu.make_async_copy(xg_hbm.at[pl.ds(0, tm), :], xbuf.at[sl], xsem.at[sl]).wait()

    def wait_scatter(tile, sl):
        _wait_rows(rows_ref[tile], EXPERT_SUBTILE, tm, lambda n: ybuf.at[sl, pl.ds(0, n), :],
                   lambda n: yg_hbm.at[pl.ds(0, n), :], ysem.at[sl])

    def tile_step(i, carry):
        slot = i % 2
        expert = te_ref[i]

        @pl.when(i == 0)
        def _():
            n_loaded[0] = 0
            for c in weight_copies(expert, 0):
                c.start()
            gather(0, 0)

        wait_gather(slot)

        @pl.when(i >= 2)
        def _():
            wait_scatter(jnp.maximum(i - 2, 0), slot)

        @pl.when((i == 0) | (expert != te_ref[jnp.maximum(i - 1, 0)]))
        def _():
            slot = n_loaded[0] % 2
            n_loaded[0] = n_loaded[0] + 1
            for c in weight_copies(expert, slot):
                c.wait()
            nxt = tend_ref[expert]

            @pl.when(nxt < n_tiles)
            def _():
                for c in weight_copies(te_ref[jnp.minimum(nxt, n_tiles - 1)], 1 - slot):
                    c.start()
            w1b[...] = w1s[slot].astype(BF16)
            w2b[...] = w2s[slot].astype(BF16)

        for rows in range(EXPERT_SUBTILE, tm + 1, EXPERT_SUBTILE):
            @pl.when(rows_ref[i] == rows)
            def _(rows=rows):
                gather(i + 1, 1 - slot)
                hdn = _dot(xbuf[slot, 0:rows, :], w1b[...]) + b1_ref[expert]
                gate = jnp.minimum(hdn[:, 0:dff], SWIGLU_LIMIT)
                up = jnp.clip(hdn[:, dff:2 * dff], -SWIGLU_LIMIT, SWIGLU_LIMIT)
                act = (up + 1.0) * gate * _sigmoid(SWIGLU_ALPHA * gate)
                ybuf[slot, 0:rows, :] = (_dot(act.astype(BF16), w2b[...]) + b2_ref[expert]).astype(BF16)
                scatter(i, slot, rows)

        @pl.when(i == n_tiles - 1)
        def _():
            wait_gather(1 - slot)
            wait_scatter(i, slot)

            @pl.when(i >= 1)
            def _():
                wait_scatter(jnp.maximum(i - 1, 0), 1 - slot)
        return carry

    lax.fori_loop(0, n_tiles, tile_step, 0)


def _experts(xg, meta, layer, w1, b1, w2, b2, n_exp):
    d = xg.shape[1]
    dff2 = w1.shape[-1]
    dff = dff2 // 2
    tm = EXPERT_TILE
    return pl.pallas_call(
        functools.partial(_expert_kernel, layer=layer),
        grid_spec=pltpu.PrefetchScalarGridSpec(
            num_scalar_prefetch=6,
            grid=(1,),
            in_specs=[
                pl.BlockSpec(memory_space=pl.ANY),
                pl.BlockSpec(memory_space=pl.ANY),
                _layer_spec(layer, n_exp, 1, dff2),
                pl.BlockSpec(memory_space=pl.ANY),
                _layer_spec(layer, n_exp, 1, d),
            ],
            out_specs=pl.BlockSpec(memory_space=pl.ANY),
            scratch_shapes=[
                pltpu.VMEM((2, tm, d), BF16), pltpu.VMEM((2, tm, d), BF16),
                pltpu.VMEM((2, d, dff2), F32), pltpu.VMEM((2, dff, d), F32),
                pltpu.VMEM((d, dff2), BF16), pltpu.VMEM((dff, d), BF16),
                pltpu.SemaphoreType.DMA((2,)), pltpu.SemaphoreType.DMA((2,)),
                pltpu.SemaphoreType.DMA((2, 2)),
                pltpu.SMEM((1,), I32),
            ],
        ),
        out_shape=jax.ShapeDtypeStruct(xg.shape, xg.dtype),
        input_output_aliases={6: 0},
        compiler_params=pltpu.CompilerParams(dimension_semantics=("arbitrary",), vmem_limit_bytes=VMEM_LIMIT),
        name="moe_experts",
    )(meta["tile_expert"], meta["n_tiles"], meta["tiles_end"], meta["tile_rows"], meta["piece_src"],
      meta["piece_dst"],
      xg, w1, b1, w2, b2)


def _combine_kernel(h2_ref, yg_ref, gw_ref, pos_ref, pp_ref, ps_ref, wpg_ref, bpg_ref, wpp_ref, gple_ref, gfin_ref,
                    *rest, final, batch):
    if final:
        outp_ref, outs_ref, prows, orows = rest
    else:
        hn_ref, prows = rest
    b = pl.program_id(0)
    nb = pl.num_programs(0)
    tb = h2_ref.shape[0]
    mp = yg_ref.shape[0]

    pos = pos_ref[...]
    gw = gw_ref[...]
    jio = lax.broadcasted_iota(I32, (mp, tb), 0)
    cg = jnp.where(jio == pos[0:1, :], gw[0:1, :], 0.0)
    for k in range(1, TOP_K):
        cg = cg + jnp.where(jio == pos[k:k + 1, :], gw[k:k + 1, :], 0.0)
    moe = lax.dot_general(cg.astype(BF16), yg_ref[...], (((0,), (0,)), ((), ())),
                          preferred_element_type=F32)
    h3 = h2_ref[...] + moe
    n3 = _rms(h3, gple_ref[...]).astype(BF16)
    gate = _sigmoid(_dot(n3, wpg_ref[...]) + bpg_ref[...])
    p_rows = jnp.where(b == nb - 1, ps_ref[0], _time_major_rows(pp_ref.at[0], prows, batch))
    h4 = h3 + gate * _dot(p_rows.astype(BF16), wpp_ref[...])
    if final:
        out = _rms(h4, gfin_ref[...])

        @pl.when(b < nb - 1)
        def _():
            _batch_major_store(out, orows, outp_ref, batch)

        @pl.when(b == nb - 1)
        def _():
            outs_ref[...] = out
    else:
        hn_ref[...] = h4


def _combine(h2, yg, gw, pos, p_prompt, p_sample, layer, w, g_final, n_exp, final):
    n_total, d = h2.shape
    _, batch, seq, ple = p_prompt.shape
    tb = TOKEN_BLOCK
    nb = n_total // tb
    steps = tb // batch
    const2 = lambda b: (0, 0)
    tok = pl.BlockSpec((SUBLANES, tb), lambda b: (0, b))
    row = pl.BlockSpec((tb, d), lambda b: (b, 0))
    prompt_blk = lambda b: jnp.minimum(b, nb - 2)
    if final:
        out_specs = [pl.BlockSpec((batch, steps, d), lambda b: (0, prompt_blk(b), 0)),
                     pl.BlockSpec((tb, d), const2)]
        out_shape = [jax.ShapeDtypeStruct((batch, seq, d), F32), jax.ShapeDtypeStruct((tb, d), F32)]
        extra_scratch = [pltpu.VMEM((d // LANES, tb, LANES), F32)]
    else:
        out_specs = [row]
        out_shape = [jax.ShapeDtypeStruct((n_total, d), F32)]
        extra_scratch = []
    return pl.pallas_call(
        functools.partial(_combine_kernel, final=final, batch=batch),
        grid=(nb,),
        in_specs=[
            row,
            pl.BlockSpec((_local_rows(n_exp), d), lambda b: (b, 0)),
            tok, tok,
            pl.BlockSpec((1, batch, steps, ple), lambda b: (layer, 0, prompt_blk(b), 0)),
            pl.BlockSpec((1, tb, ple), lambda b: (layer, 0, 0)),
            _layer_spec(layer, d, d), _layer_spec(layer, 1, d), _layer_spec(layer, ple, d),
            _layer_spec(layer, 1, d),
            pl.BlockSpec((1, d), const2),
        ],
        out_specs=out_specs,
        scratch_shapes=[pltpu.VMEM((ple // LANES, tb, LANES), F32)] + extra_scratch,
        out_shape=out_shape,
        compiler_params=pltpu.CompilerParams(dimension_semantics=("arbitrary",), vmem_limit_bytes=VMEM_LIMIT),
        name="moe_combine_ple",
    )(h2, yg, gw, pos, p_prompt, p_sample, w["w_pg"], w["b_pg"], w["w_pp"], w["g_ple"], g_final)


def _prepare_weights(batch, g_mix, w_in, g_v, w_s, b_s, conv_w, conv_b, w_a, b_a, w_x, b_x, lam,
                     g_oa, g_ob, w_out, g_ffn, w_r, b_r, g_ple, w_pg, b_pg, w_pp):
    depth, d = w_in.shape[0], w_in.shape[1]
    dh = d // 2
    gd = dh // GMLP_GROUPS
    n_exp = w_r.shape[2]
    row = lambda a: a.reshape(depth, 1, -1).astype(F32)
    wt = jnp.where(jnp.tril(jnp.ones((CHUNK, CHUNK), bool)), w_s, 0.0).astype(BF16)
    hr = HALF * batch
    row_t = jnp.arange(hr, dtype=I32) // batch
    row_b = jnp.arange(hr, dtype=I32) % batch
    expand_t = (row_t[:, None] == jnp.arange(HALF, dtype=I32)[None, :]).astype(BF16)
    same_b = row_b[:, None] == row_b[None, :]

    def kron_block(blk):
        rows = jnp.einsum("it,lgts->lgis", expand_t, blk, preferred_element_type=F32).astype(BF16)
        full = jnp.einsum("lgis,js->lgij", rows, expand_t, preferred_element_type=F32)
        return jnp.where(same_b, full, 0.0).astype(BF16)

    kd = jnp.stack([kron_block(wt[:, :, :HALF, :HALF]), kron_block(wt[:, :, HALF:, HALF:])], axis=1)
    k10 = kron_block(wt[:, :, HALF:, :HALF])
    bs_rows = jnp.repeat(jnp.repeat(b_s.transpose(0, 2, 1).astype(F32), gd, axis=2), batch, axis=1).reshape(
        depth, 2, hr, dh)
    hd = dh // LRU_HEADS
    head_of = jnp.arange(dh, dtype=I32) // hd
    same_head = head_of[:, None] == head_of[None, :]
    block_diag = lambda w: jnp.where(same_head, jnp.tile(w.astype(F32).reshape(depth, dh, hd), (1, 1, LRU_HEADS)),
                                     0.0)
    wa_f32, wx_f32 = block_diag(w_a), block_diag(w_x)
    return dict(
        g_mix=row(g_mix), w_in=w_in.astype(BF16), w_in_f32=w_in.astype(F32), g_v=row(g_v),
        kd=kd, k10=k10, bs_rows=bs_rows,
        w0_row=row(jnp.repeat(w_s[:, :, 0, 0], gd, axis=1)), b0_row=row(jnp.repeat(b_s[:, :, 0], gd, axis=1)),
        conv_w=conv_w.astype(F32), conv_b=row(conv_b),
        wa=wa_f32.astype(BF16), wa_f32=wa_f32, b_a=row(b_a), wx=wx_f32.astype(BF16), wx_f32=wx_f32,
        b_x=row(b_x), lam=row(lam),
        g_oa=row(g_oa), g_ob=row(g_ob), w_out=w_out.astype(BF16), w_out_f32=w_out.astype(F32),
        g_ffn=row(g_ffn),
        w_r=jnp.pad(w_r.astype(F32), ((0, 0), (0, 0), (0, LANES - n_exp))),
        b_r=b_r.reshape(depth, n_exp, 1).astype(F32),
        g_ple=row(g_ple), w_pg=w_pg.astype(BF16), b_pg=row(b_pg), w_pp=w_pp.astype(BF16))


def _layer_spec(layer, *shape):
    return pl.BlockSpec((None,) + shape, lambda *_: (layer,) + (0,) * len(shape))


def kernel(x_prompt, x_sample, state_conv, state_lru, p_prompt, p_sample, g_mix, w_in, g_v, w_s, b_s, conv_w, conv_b, w_a, b_a, w_x, b_x, lam, g_oa, g_ob, w_out, g_ffn, w_r, b_r, w1, b1, w2, b2, g_ple, w_pg, b_pg, w_pp, g_final):
    batch, seq, d = x_prompt.shape
    dec_batch, dec_seq, _ = x_sample.shape
    depth = w_in.shape[0]
    n_exp = w_r.shape[2]
    dh = d // 2
    ple = p_prompt.shape[-1]
    assert dec_seq == 1 and seq % CHUNK == 0 and batch % SUBLANES == 0
    assert dh == GMLP_GROUPS * LANES and dh % (LRU_HEADS * (LANES // 2)) == 0
    n_prompt = seq * batch
    assert n_prompt % TOKEN_BLOCK == 0
    n_valid = n_prompt + dec_batch
    n_total = _round_up(n_valid, TOKEN_BLOCK)
    s_rows = n_total - n_prompt
    assert s_rows == TOKEN_BLOCK
    nb = n_total // TOKEN_BLOCK
    max_tiles = _max_sorted_rows(n_valid, nb, n_exp) // EXPERT_TILE
    pad_rows = ((0, 0), (0, s_rows - dec_batch), (0, 0))

    h, h_sample, h_sample_block = x_prompt, jnp.pad(x_sample.reshape(1, dec_batch, d), pad_rows)[0], 0
    p_samp = jnp.pad(p_sample.reshape(depth, dec_batch, ple), pad_rows)
    sconv = jnp.pad(state_conv.transpose(0, 2, 1, 3), ((0, 0),) + pad_rows)
    slru = jnp.pad(state_lru, pad_rows)
    g_fin = g_final.reshape(1, d).astype(F32)
    b1_rows = b1.reshape(depth, n_exp, 1, -1)
    b2_rows = b2.reshape(depth, n_exp, 1, -1)

    w = _prepare_weights(batch, g_mix, w_in, g_v, w_s, b_s, conv_w, conv_b, w_a, b_a, w_x, b_x, lam,
                         g_oa, g_ob, w_out, g_ffn, w_r, b_r, g_ple, w_pg, b_pg, w_pp)
    v_p, conv_p, lru_p, v_s, conv_s, lru_s = [], [], [], [], [], []
    for l in range(depth):
        y, vp, cp, lp = _mix_prompt(h, n_prompt, batch, w, l)
        post_out = _post(y, h, w, l, n_total, n_valid, n_exp, batch)
        h2, xg, gw, pos, cnt, vs, cs, ls = _sample_front(
            h_sample, h_sample_block, post_out, nb - 1, w, l, sconv, slru, n_valid, n_exp)
        meta = _route_meta(cnt[:, 0, :].astype(I32), max_tiles)
        yg = _experts(xg, meta, l, w1, b1_rows, w2, b2_rows, n_exp)
        final = l == depth - 1
        res = _combine(h2, yg, gw, pos, p_prompt, p_samp, l, w, g_fin, n_exp, final)
        if final:
            y_prompt, out_sample = res
        else:
            h = h_sample = res[0]
            h_sample_block = nb - 1
        for acc, val in zip((v_p, conv_p, lru_p, v_s, conv_s, lru_s), (vp, cp, lp, vs, cs, ls)):
            acc.append(val)

    y_sample = out_sample[:dec_batch].reshape(dec_batch, 1, d)
    v_prompt = jnp.stack(v_p).reshape(depth, CHUNK, batch, dh).transpose(0, 2, 1, 3)
    conv_prompt = jnp.stack(conv_p).reshape(depth, CONV_W - 1, batch, dh).transpose(0, 2, 1, 3)
    v_sample = jnp.stack(v_s)[:, :dec_batch].reshape(depth, dec_batch, 1, dh)
    conv_sample = jnp.stack(conv_s)[:, :, :dec_batch].transpose(0, 2, 1, 3)
    return (y_prompt, y_sample, v_prompt, conv_prompt, jnp.stack(lru_p),
            v_sample, conv_sample, jnp.stack(lru_s)[:, :dec_batch])
```

```python
import functools

import jax
import jax.numpy as jnp
from jax import lax
from jax.experimental import pallas as pl
from jax.experimental.pallas import tpu as pltpu

F32 = jnp.float32
BF16 = jnp.bfloat16
I32 = jnp.int32

CHUNK = 128
HALF = CHUNK // 2
GMLP_GROUPS = 4
LRU_HEADS = 8
CONV_W = 4
LRU_C = 8.0
TOP_K = 4
SWIGLU_LIMIT = 7.0
SWIGLU_ALPHA = 1.702
EPS = 1e-6

LANES = 128
SUBLANES = 8
TOKEN_BLOCK = 256
EXPERT_TILE = 512
EXPERT_SUBTILE = 128
SEG_ALIGN = SUBLANES
VMEM_LIMIT = 56 * 1024 * 1024


def _round_up(x, m):
    return (x + m - 1) // m * m


def _dot(a, b):
    return jnp.dot(a, b, preferred_element_type=F32)


def _gelu(x):
    return 0.5 * x * (1.0 + jnp.tanh(0.7978845608028654 * (x + 0.044715 * (x * x * x))))


def _sigmoid(x):
    return 1.0 / (1.0 + jnp.exp(-x))


def _softplus(x):
    return jnp.maximum(x, 0.0) + jnp.log1p(jnp.exp(-jnp.abs(x)))


def _rms(x, g):
    ms = jnp.mean(x * x, axis=-1, keepdims=True)
    return x * lax.rsqrt(ms + EPS) * g


def _group_norm_128(x, g):
    outs = []
    for j in range(x.shape[1] // LANES):
        blk = x[:, j * LANES:(j + 1) * LANES]
        ms = jnp.mean(blk * blk, axis=-1, keepdims=True)
        outs.append(blk * lax.rsqrt(ms + EPS))
    return jnp.concatenate(outs, axis=1) * g


def _group_norm_64(x, g):
    half = LANES // 2
    lo_mask = lax.broadcasted_iota(I32, (1, LANES), 1) < half
    outs = []
    for j in range(x.shape[1] // LANES):
        blk = x[:, j * LANES:(j + 1) * LANES]
        sq = blk * blk
        lo = jnp.sum(jnp.where(lo_mask, sq, 0.0), axis=-1, keepdims=True)
        hi = jnp.sum(jnp.where(lo_mask, 0.0, sq), axis=-1, keepdims=True)
        ms = jnp.where(lo_mask, lo, hi) * (1.0 / half)
        outs.append(blk * lax.rsqrt(ms + EPS))
    return jnp.concatenate(outs, axis=1) * g


def _time_major_rows(src_ref, slabs_ref, batch):
    steps = src_ref.shape[1]
    n_slab = slabs_ref.shape[0]
    for b in range(batch):
        blk = src_ref[b]
        for j in range(n_slab):
            slabs_ref[j, pl.ds(b, steps, stride=batch), :] = blk[:, j * LANES:(j + 1) * LANES]
    return jnp.concatenate([slabs_ref[j] for j in range(n_slab)], axis=1)


def _batch_major_store(rows, slabs_ref, dst_ref, batch):
    steps = dst_ref.shape[1]
    n_slab = slabs_ref.shape[0]
    for j in range(n_slab):
        slabs_ref[j] = rows[:, j * LANES:(j + 1) * LANES]
    for b in range(batch):
        dst_ref[b] = jnp.concatenate(
            [slabs_ref[j, pl.ds(b, steps, stride=batch), :] for j in range(n_slab)], axis=1)


def _split_bf16(x):
    hi = x.astype(BF16)
    lo = (x - hi.astype(F32)).astype(BF16)
    return hi, lo


def _dot_bf16(x, w):
    return _dot(x.astype(BF16), w)


def _dot_3pass(x, w):
    x_hi, x_lo = _split_bf16(x)
    w_hi, w_lo = _split_bf16(w)
    return _dot(x_hi, w_hi) + (_dot(x_lo, w_hi) + _dot(x_hi, w_lo))


def _lru_gates(xc, wa_ref, ba_ref, wx_ref, bx_ref, lam_ref, mm):
    r = _sigmoid(mm(xc, wa_ref[...]) + ba_ref[...])
    i = _sigmoid(mm(xc, wx_ref[...]) + bx_ref[...])
    log_a = (-LRU_C * r) * _softplus(-lam_ref[...])
    a = jnp.exp(log_a)
    mult = jnp.sqrt(-jnp.tanh(log_a) * (a * a + 1.0))
    return a, mult, i


def _mix_prompt_kernel(h_ref, gmix_ref, win_ref, gv_ref, kd_ref, k10_ref, bs_ref, cw_ref, cb_ref,
                       wa_ref, ba_ref, wx_ref, bx_ref, lam_ref, goa_ref, gob_ref,
                       y_ref, v_ref, conv_ref, lru_ref,
                       hstate, xpad, vprev, a_s, b_s, hs_s, hrows, z_even, z_odd, *, batch, batch_major):
    c = pl.program_id(0)
    rows = y_ref.shape[0]
    dh = gv_ref.shape[1]
    tail = (CONV_W - 1) * batch

    @pl.when(c == 0)
    def _():
        z_odd[...] = jnp.zeros_like(z_odd)
        vprev[...] = jnp.zeros_like(vprev)

    @pl.when(c <= 1)
    def _():
        hstate[...] = jnp.zeros_like(hstate)
        xpad[0:tail, :] = jnp.zeros((tail, dh), F32)

    def tile_step(z_in, z_out, par):
        if par == 0:
            vprev[...] = jnp.zeros_like(vprev)

        vn = _group_norm_128(_gelu(z_in[:, dh:2 * dh]), gv_ref[...])
        vb = vn.astype(BF16)
        s_parts = []
        for g in range(GMLP_GROUPS):
            sl = slice(g * LANES, (g + 1) * LANES)
            s_parts.append(_dot(kd_ref[par, g], vb[:, sl]) + _dot(k10_ref[g], vprev[:, sl]))
        s = jnp.concatenate(s_parts, axis=1) + bs_ref[par]
        vprev[...] = vb
        v_ref[...] = vn
        y_ref[:, 0:dh] = _group_norm_128(_gelu(z_in[:, 0:dh]) * s, goa_ref[...]).astype(BF16)

        xpad[tail:tail + rows, :] = z_in[:, 2 * dh:3 * dh]
        xc = cb_ref[...] + cw_ref[0:1, :] * xpad[0:rows, :]
        for k in range(1, CONV_W):
            xc = xc + cw_ref[k:k + 1, :] * xpad[k * batch:k * batch + rows, :]
        new_tail = xpad[rows:rows + tail, :]
        xpad[0:tail, :] = new_tail

        a, mult, i = _lru_gates(xc, wa_ref, ba_ref, wx_ref, bx_ref, lam_ref, _dot_bf16)
        row = lax.broadcasted_iota(I32, (rows, 1), 0)
        mult = jnp.where((c == 1) & (row < batch), 1.0, mult)
        a_s[...] = a
        b_s[...] = mult * i * xc

        def step(t, h):
            r0 = t * batch
            h = a_s[r0:r0 + batch, :] * h + b_s[r0:r0 + batch, :]
            hs_s[r0:r0 + batch, :] = h
            return h

        h_last = hstate[...]
        for t in range(rows // batch):
            h_last = step(t, h_last)
        hstate[...] = h_last

        y_ref[:, dh:2 * dh] = _group_norm_64(hs_s[...] * _gelu(z_in[:, 3 * dh:4 * dh]), gob_ref[...]).astype(BF16)
        conv_ref[...] = new_tail
        lru_ref[...] = h_last

        h = _time_major_rows(h_ref, hrows, batch) if batch_major else h_ref[...]
        z_out[...] = _dot(_rms(h, gmix_ref[...]).astype(BF16), win_ref[...])

    @pl.when(c % 2 == 0)
    def _():
        tile_step(z_odd, z_even, 1)

    @pl.when(c % 2 == 1)
    def _():
        tile_step(z_even, z_odd, 0)


def _mix_prompt(h, n_rows, batch, w, layer):
    batch_major = h.ndim == 3
    d = h.shape[-1]
    dh = d // 2
    rows = HALF * batch
    n_steps = n_rows // rows
    tail = (CONV_W - 1) * batch
    const2 = lambda c: (0, 0)
    lw = functools.partial(_layer_spec, layer)
    proj_tile = lambda c: jnp.minimum(c, n_steps - 1)
    mix_tile = lambda c: jnp.maximum(c - 1, 0)
    h_spec = (pl.BlockSpec((batch, HALF, d), lambda c: (0, proj_tile(c), 0)) if batch_major
              else pl.BlockSpec((rows, d), lambda c: (proj_tile(c), 0)))
    return pl.pallas_call(
        functools.partial(_mix_prompt_kernel, batch=batch, batch_major=batch_major),
        grid=(n_steps + 1,),
        in_specs=[
            h_spec,
            lw(1, d), lw(d, 2 * d), lw(1, dh),
            lw(2, GMLP_GROUPS, rows, rows), lw(GMLP_GROUPS, rows, rows), lw(2, rows, dh),
            lw(CONV_W, dh), lw(1, dh),
            lw(dh, dh), lw(1, dh), lw(dh, dh), lw(1, dh),
            lw(1, dh), lw(1, dh), lw(1, dh),
        ],
        out_specs=[
            pl.BlockSpec((rows, d), lambda c: (mix_tile(c), 0)),
            pl.BlockSpec((rows, dh), lambda c: (jnp.maximum(mix_tile(c) - (n_steps - 2), 0), 0)),
            pl.BlockSpec((tail, dh), const2),
            pl.BlockSpec((batch, dh), const2),
        ],
        out_shape=[
            jax.ShapeDtypeStruct((n_rows, d), BF16),
            jax.ShapeDtypeStruct((2 * rows, dh), F32),
            jax.ShapeDtypeStruct((tail, dh), F32),
            jax.ShapeDtypeStruct((batch, dh), F32),
        ],
        scratch_shapes=[
            pltpu.VMEM((batch, dh), F32),
            pltpu.VMEM((tail + rows, dh), F32),
            pltpu.VMEM((rows, dh), BF16),
            pltpu.VMEM((rows, dh), F32),
            pltpu.VMEM((rows, dh), F32),
            pltpu.VMEM((rows, dh), F32),
            pltpu.VMEM((d // LANES, rows, LANES), F32),
            pltpu.VMEM((rows, 2 * d), F32),
            pltpu.VMEM((rows, 2 * d), F32),
        ],
        compiler_params=pltpu.CompilerParams(dimension_semantics=("arbitrary",), vmem_limit_bytes=VMEM_LIMIT),
        name="mix_prompt",
    )(h, w["g_mix"], w["w_in"], w["g_v"], w["kd"], w["k10"], w["bs_rows"], w["conv_w"], w["conv_b"],
      w["wa"], w["b_a"], w["wx"], w["b_x"], w["lam"], w["g_oa"], w["g_ob"])


def _sample_front_kernel(h_ref, h2_any, xg_any, gw_any, pos_any, cnt_any,
                         gmix_ref, win_ref, gv_ref, w0_ref, b0_ref, cw_ref, cb_ref,
                         wa_ref, ba_ref, wx_ref, bx_ref, lam_ref, goa_ref, gob_ref, sconv_ref, slru_ref,
                         wout_ref, gffn_ref, wr_ref, br_ref,
                         h2_ref, xg_ref, gw_ref, pos_ref, cnt_ref, v_ref, conv_ref, lru_ref,
                         *, block, n_valid, n_exp):
    del h2_any, xg_any, gw_any, pos_any, cnt_any
    dh = gv_ref.shape[1]
    h = h_ref[...]
    n = _rms(h, gmix_ref[...])
    proj = lambda lo: _dot_3pass(n, win_ref[:, lo:lo + dh])

    vn = _group_norm_128(_gelu(proj(dh)), gv_ref[...])
    v_ref[...] = vn
    s = vn * w0_ref[...] + b0_ref[...]
    ya = _group_norm_128(_gelu(proj(0)) * s, goa_ref[...])

    xb = proj(2 * dh)
    xc = cb_ref[...] + cw_ref[CONV_W - 1:CONV_W, :] * xb
    for k in range(CONV_W - 1):
        xc = xc + cw_ref[k:k + 1, :] * sconv_ref[k]
    for k in range(CONV_W - 2):
        conv_ref[k] = sconv_ref[k + 1]
    conv_ref[CONV_W - 2] = xb

    a, mult, i = _lru_gates(xc, wa_ref, ba_ref, wx_ref, bx_ref, lam_ref, _dot_3pass)
    h_new = a * slru_ref[...] + mult * i * xc
    lru_ref[...] = h_new
    yb = _group_norm_64(h_new * _gelu(proj(3 * dh)), gob_ref[...])

    h2 = h + _dot_3pass(ya, wout_ref[0:dh, :]) + _dot_3pass(yb, wout_ref[dh:2 * dh, :])
    _route_block(h2, gffn_ref, wr_ref, br_ref, h2_ref, xg_ref, gw_ref, pos_ref, cnt_ref,
                 block=block, n_valid=n_valid, n_exp=n_exp)


def _sample_front(h, h_block, post_out, block, w, layer, sconv, slru, n_valid, n_exp):
    d = h.shape[1]
    dh = d // 2
    rows = TOKEN_BLOCK
    const2 = lambda i: (0, 0)
    const3 = lambda i: (0, 0, 0)
    lw = functools.partial(_layer_spec, layer)
    vec = lw(1, dh)
    anyspace = pl.BlockSpec(memory_space=pl.ANY)
    blk_row = pl.BlockSpec((rows, d), lambda i: (block, 0))
    blk_tok = pl.BlockSpec((SUBLANES, rows), lambda i: (0, block))
    return pl.pallas_call(
        functools.partial(_sample_front_kernel, block=block, n_valid=n_valid, n_exp=n_exp),
        grid=(1,),
        in_specs=[
            pl.BlockSpec((rows, d), lambda i: (h_block, 0)),
            anyspace, anyspace, anyspace, anyspace, anyspace,
            lw(1, d), lw(d, 2 * d),
            vec, vec, vec,
            lw(CONV_W, dh), vec,
            lw(dh, dh), vec, lw(dh, dh), vec,
            vec, vec, vec,
            lw(CONV_W - 1, rows, dh), lw(rows, dh),
            lw(d, d), lw(1, d), lw(d, LANES), lw(n_exp, 1),
        ],
        out_specs=[
            blk_row, pl.BlockSpec((_local_rows(n_exp), d), lambda i: (block, 0)), blk_tok, blk_tok,
            pl.BlockSpec((1, SUBLANES, n_exp), lambda i: (block, 0, 0)),
            pl.BlockSpec((rows, dh), const2),
            pl.BlockSpec((CONV_W - 1, rows, dh), const3),
            pl.BlockSpec((rows, dh), const2),
        ],
        out_shape=[jax.ShapeDtypeStruct(a.shape, a.dtype) for a in post_out] + [
            jax.ShapeDtypeStruct((rows, dh), F32),
            jax.ShapeDtypeStruct((CONV_W - 1, rows, dh), F32),
            jax.ShapeDtypeStruct((rows, dh), F32),
        ],
        input_output_aliases={1 + k: k for k in range(len(post_out))},
        compiler_params=pltpu.CompilerParams(dimension_semantics=("arbitrary",), vmem_limit_bytes=VMEM_LIMIT),
        name="sample_front",
    )(h, *post_out, w["g_mix"], w["w_in_f32"], w["g_v"], w["w0_row"], w["b0_row"], w["conv_w"], w["conv_b"],
      w["wa_f32"], w["b_a"], w["wx_f32"], w["b_x"], w["lam"], w["g_oa"], w["g_ob"], sconv, slru,
      w["w_out_f32"], w["g_ffn"], w["w_r"], w["b_r"])


def _route_block(h2, gffn_ref, wr_ref, br_ref, h2_ref, xg_ref, gw_ref, pos_ref, cnt_ref,
                 *, block, n_valid, n_exp):
    tb = h2.shape[0]
    h2_ref[...] = h2
    n2 = _rms(h2, gffn_ref[...])

    n_hi, n_lo = _split_bf16(n2)
    w_hi, w_lo = _split_bf16(wr_ref[...])
    both = _dot(n_hi, jnp.concatenate([w_hi, w_lo], axis=1))
    logits = both[:, 0:LANES] + (_dot(n_lo, w_hi) + both[:, LANES:2 * LANES])
    lt = logits.T[0:n_exp, :] + br_ref[...]

    eio = lax.broadcasted_iota(I32, (n_exp, tb), 0).astype(F32)
    col = block * tb + lax.broadcasted_iota(I32, (1, tb), 1)
    valid = col < n_valid
    vals, hots = [], []
    for _ in range(TOP_K):
        m = jnp.max(lt, axis=0, keepdims=True)
        sel = jnp.min(jnp.where(lt == m, eio, float(n_exp)), axis=0, keepdims=True)
        hot = eio == sel
        lt = jnp.where(hot, -jnp.inf, lt)
        vals.append(m)
        hots.append(hot)
    exps = [jnp.exp(v - vals[0]) for v in vals]
    den = exps[0]
    for e in exps[1:]:
        den = den + e

    occ = jnp.zeros((n_exp, tb), F32)
    for hot in hots:
        occ = occ + jnp.where(hot & valid, 1.0, 0.0)
    occ_b = occ.astype(BF16)
    before = (lax.broadcasted_iota(I32, (tb, tb), 0) < lax.broadcasted_iota(I32, (tb, tb), 1))
    ranks_e = _dot(occ_b, jnp.where(before, 1.0, 0.0).astype(BF16))
    cnt_ref[0] = lax.dot_general(jnp.ones((SUBLANES, tb), BF16), occ_b, (((1,), (1,)), ((), ())),
                                 preferred_element_type=F32)
    cnt_col = jnp.sum(occ, axis=1, keepdims=True)
    units = jnp.floor((cnt_col + (SEG_ALIGN - 1.0)) * (1.0 / SEG_ALIGN))
    earlier = (lax.broadcasted_iota(I32, (n_exp, n_exp), 1) < lax.broadcasted_iota(I32, (n_exp, n_exp), 0))
    seg_off = _dot(jnp.where(earlier, 1.0, 0.0).astype(BF16),
                   jnp.broadcast_to(units, (n_exp, LANES)).astype(BF16))[:, 0:1] * float(SEG_ALIGN)
    rows_e = ranks_e + seg_off

    rio = lax.broadcasted_iota(I32, (SUBLANES, tb), 0)
    gw_out = jnp.zeros((SUBLANES, tb), F32)
    pos_out = jnp.full((SUBLANES, tb), -1, I32)
    for k in range(TOP_K):
        pos_k = jnp.sum(jnp.where(hots[k], rows_e, 0.0), axis=0, keepdims=True).astype(I32)
        gw_out = jnp.where(rio == k, exps[k] / den, gw_out)
        pos_out = jnp.where(rio == k, jnp.where(valid, pos_k, -1), pos_out)
    gw_ref[...] = gw_out
    pos_ref[...] = pos_out

    jio = lax.broadcasted_iota(I32, (xg_ref.shape[0], tb), 0)
    hit = jio == pos_out[0:1, :]
    for k in range(1, TOP_K):
        hit = hit | (jio == pos_out[k:k + 1, :])
    xg_ref[...] = _dot(jnp.where(hit, 1.0, 0.0).astype(BF16), n_hi).astype(BF16)


def _post_kernel(y_ref, h_ref, wout_ref, gffn_ref, wr_ref, br_ref,
                 h2_ref, xg_ref, gw_ref, pos_ref, cnt_ref, hrows, *, n_valid, n_exp, batch, batch_major):
    h = _time_major_rows(h_ref, hrows, batch) if batch_major else h_ref[...]
    h2 = h + _dot(y_ref[...], wout_ref[...])
    _route_block(h2, gffn_ref, wr_ref, br_ref, h2_ref, xg_ref, gw_ref, pos_ref, cnt_ref,
                 block=pl.program_id(0), n_valid=n_valid, n_exp=n_exp)


def _post(y, h, w, layer, n_total, n_valid, n_exp, batch):
    batch_major = h.ndim == 3
    d = h.shape[-1]
    tb = TOKEN_BLOCK
    nb = n_total // tb
    const2 = lambda b: (0, 0)
    tok = pl.BlockSpec((SUBLANES, tb), lambda b: (0, b))
    h_spec = (pl.BlockSpec((batch, tb // batch, d), lambda b: (0, b, 0)) if batch_major
              else pl.BlockSpec((tb, d), lambda b: (b, 0)))
    return pl.pallas_call(
        functools.partial(_post_kernel, n_valid=n_valid, n_exp=n_exp, batch=batch, batch_major=batch_major),
        grid=(y.shape[0] // tb,),
        in_specs=[
            pl.BlockSpec((tb, d), lambda b: (b, 0)),
            h_spec,
            _layer_spec(layer, d, d), _layer_spec(layer, 1, d), _layer_spec(layer, d, LANES),
            _layer_spec(layer, n_exp, 1),
        ],
        out_specs=[
            pl.BlockSpec((tb, d), lambda b: (b, 0)),
            pl.BlockSpec((_local_rows(n_exp), d), lambda b: (b, 0)),
            tok, tok,
            pl.BlockSpec((1, SUBLANES, n_exp), lambda b: (b, 0, 0)),
        ],
        out_shape=[
            jax.ShapeDtypeStruct((n_total, d), F32),
            jax.ShapeDtypeStruct((_grouped_rows(nb, n_exp), d), BF16),
            jax.ShapeDtypeStruct((SUBLANES, n_total), F32),
            jax.ShapeDtypeStruct((SUBLANES, n_total), I32),
            jax.ShapeDtypeStruct((nb, SUBLANES, n_exp), F32),
        ],
        scratch_shapes=[pltpu.VMEM((d // LANES, tb, LANES), F32)],
        compiler_params=pltpu.CompilerParams(dimension_semantics=("arbitrary",), vmem_limit_bytes=VMEM_LIMIT),
        name="post_router",
    )(y, h, w["w_out"], w["g_ffn"], w["w_r"], w["b_r"])


def _max_sorted_rows(n_valid, nb, n_exp):
    worst = TOP_K * n_valid + (SEG_ALIGN - 1) * nb * n_exp + (EXPERT_TILE - SEG_ALIGN) * n_exp
    return _round_up(worst, EXPERT_TILE)


def _local_rows(n_exp):
    return _round_up(TOP_K * TOKEN_BLOCK + (SEG_ALIGN - 1) * n_exp, LANES)


def _dump_rows(n_exp):
    return n_exp * EXPERT_SUBTILE


def _grouped_rows(nb, n_exp):
    return nb * _local_rows(n_exp) + _dump_rows(n_exp)


def _select_columns(per_block_expert, expert_of):
    n_exp = per_block_expert.shape[1]
    one_hot = (expert_of[:, None] == jnp.arange(n_exp, dtype=I32)[None, :]).astype(F32)
    picked = jnp.dot(one_hot, per_block_expert.T.astype(F32), precision=lax.Precision.HIGHEST)
    return jnp.round(picked).astype(I32)


def _route_meta(cnt, max_tiles):
    nb, n_exp = cnt.shape
    mp = _local_rows(n_exp)
    p8 = (cnt + SEG_ALIGN - 1) // SEG_ALIGN * SEG_ALIGN
    off = jnp.cumsum(p8, axis=1) - p8
    tot = jnp.sum(p8, axis=1)
    seg = jnp.sum(p8, axis=0)
    reg = (seg + EXPERT_TILE - 1) // EXPERT_TILE * EXPERT_TILE
    reg_start = jnp.cumsum(reg) - reg
    tiles_end = jnp.cumsum(reg // EXPERT_TILE)
    tile_ids = jnp.arange(max_tiles, dtype=I32)
    tile_expert = jnp.minimum(jnp.sum((tiles_end[None, :] <= tile_ids[:, None]).astype(I32), axis=1), n_exp - 1)
    of_tile = tile_expert[:, None] == jnp.arange(n_exp, dtype=I32)[None, :]
    pick = lambda per_expert: jnp.sum(jnp.where(of_tile, per_expert[None, :], 0), axis=1)
    used = jnp.clip(pick(seg) - (tile_ids - pick(tiles_end - reg // EXPERT_TILE)) * EXPERT_TILE, 0, EXPERT_TILE)
    tile_rows = (used + EXPERT_SUBTILE - 1) // EXPERT_SUBTILE * EXPERT_SUBTILE

    pieces_per_tile = EXPERT_TILE // SEG_ALIGN
    piece_expert = jnp.repeat(tile_expert, pieces_per_tile)
    piece_tile = jnp.repeat(tile_ids, pieces_per_tile)
    row = jnp.arange(max_tiles * pieces_per_tile, dtype=I32) * SEG_ALIGN
    is_e = piece_expert[:, None] == jnp.arange(n_exp, dtype=I32)[None, :]
    of_expert = lambda per_expert: jnp.sum(jnp.where(is_e, per_expert[None, :], 0), axis=1)
    rin = row - of_expert(reg_start)
    seg_e = of_expert(seg)
    ends = _select_columns(jnp.cumsum(p8, axis=0), piece_expert)
    blk = jnp.minimum(jnp.sum((ends <= rin[:, None]).astype(I32), axis=1), nb - 1)
    is_b = blk[:, None] == jnp.arange(nb, dtype=I32)[None, :]
    of_block = lambda a: jnp.sum(jnp.where(is_b, a, 0), axis=1)
    seg_first = of_block(ends - _select_columns(p8, piece_expert))
    local = of_block(_select_columns(off, piece_expert)) + rin - seg_first
    pad_k = jnp.clip((rin - seg_e) // SEG_ALIGN, 0, EXPERT_SUBTILE // SEG_ALIGN - 1)
    real = (rin < seg_e) & (piece_tile < tiles_end[-1])
    home = blk * mp + local
    src = jnp.where(real, home, mp - SEG_ALIGN)
    dst = jnp.where(real, home, nb * mp + (piece_expert * (EXPERT_SUBTILE // SEG_ALIGN) + pad_k) * SEG_ALIGN)
    look_ahead = jnp.full((pieces_per_tile,), mp - SEG_ALIGN, I32)
    return dict(piece_src=jnp.concatenate([src.astype(I32), look_ahead]), piece_dst=dst.astype(I32),
                tot=tot.astype(I32),
                tile_expert=tile_expert.astype(I32), n_tiles=tiles_end[-1:].astype(I32),
                tiles_end=tiles_end.astype(I32), tile_rows=tile_rows.astype(I32))


def _wait_rows(n_rows, min_rows, max_rows, src_rows, dst_rows, sem):
    size = min_rows
    while size <= max_rows:
        @pl.when((n_rows & size) != 0)
        def _(size=size):
            pltpu.make_async_copy(src_rows(size), dst_rows(size), sem).wait()
        size *= 2


def _expert_kernel(te_ref, nt_ref, tend_ref, rows_ref, src_ref, dst_ref,
                   xg_hbm, w1_hbm, b1_ref, w2_hbm, b2_ref, yg_hbm,
                   xbuf, ybuf, w1s, w2s, w1b, w2b, xsem, ysem, sems, n_loaded, *, layer):
    n_tiles = nt_ref[0]
    dff = w2b.shape[0]
    tm = xbuf.shape[1]
    pieces = tm // SEG_ALIGN

    def weight_copies(expert, slot):
        return (pltpu.make_async_copy(w1_hbm.at[layer, expert], w1s.at[slot], sems.at[0, slot]),
                pltpu.make_async_copy(w2_hbm.at[layer, expert], w2s.at[slot], sems.at[1, slot]))

    def gather(tile, sl):
        for j in range(pieces):
            row = pl.multiple_of(src_ref[tile * pieces + j], SEG_ALIGN)
            pltpu.make_async_copy(xg_hbm.at[pl.ds(row, SEG_ALIGN), :],
                                  xbuf.at[sl, pl.ds(j * SEG_ALIGN, SEG_ALIGN), :], xsem.at[sl]).start()

    def scatter(tile, sl, rows):
        for j in range(rows // SEG_ALIGN):
            row = pl.multiple_of(dst_ref[tile * pieces + j], SEG_ALIGN)
            pltpu.make_async_copy(ybuf.at[sl, pl.ds(j * SEG_ALIGN, SEG_ALIGN), :],
                                  yg_hbm.at[pl.ds(row, SEG_ALIGN), :], ysem.at[sl]).start()

    def wait_gather(sl):
        pltpu.make_async_copy(xg_hbm.at[pl.ds(0, tm), :], xbuf.at[sl], xsem.at[sl]).wait()

    def wait_scatter(tile, sl):
        _wait_rows(rows_ref[tile], EXPERT_SUBTILE, tm, lambda n: ybuf.at[sl, pl.ds(0, n), :],
                   lambda n: yg_hbm.at[pl.ds(0, n), :], ysem.at[sl])

    def tile_step(i, carry):
        slot = i % 2
        expert = te_ref[i]

        @pl.when(i == 0)
        def _():
            n_loaded[0] = 0
            for c in weight_copies(expert, 0):
                c.start()
            gather(0, 0)

        wait_gather(slot)

        @pl.when(i >= 2)
        def _():
            wait_scatter(jnp.maximum(i - 2, 0), slot)

        @pl.when((i == 0) | (expert != te_ref[jnp.maximum(i - 1, 0)]))
        def _():
            slot = n_loaded[0] % 2
            n_loaded[0] = n_loaded[0] + 1
            for c in weight_copies(expert, slot):
                c.wait()
            nxt = tend_ref[expert]

            @pl.when(nxt < n_tiles)
            def _():
                for c in weight_copies(te_ref[jnp.minimum(nxt, n_tiles - 1)], 1 - slot):
                    c.start()
            w1b[...] = w1s[slot].astype(BF16)
            w2b[...] = w2s[slot].astype(BF16)

        for rows in range(EXPERT_SUBTILE, tm + 1, EXPERT_SUBTILE):
            @pl.when(rows_ref[i] == rows)
            def _(rows=rows):
                gather(i + 1, 1 - slot)
                hdn = _dot(xbuf[slot, 0:rows, :], w1b[...]) + b1_ref[expert]
                gate = jnp.minimum(hdn[:, 0:dff], SWIGLU_LIMIT)
                up = jnp.clip(hdn[:, dff:2 * dff], -SWIGLU_LIMIT, SWIGLU_LIMIT)
                act = (up + 1.0) * gate * _sigmoid(SWIGLU_ALPHA * gate)
                ybuf[slot, 0:rows, :] = (_dot(act.astype(BF16), w2b[...]) + b2_ref[expert]).astype(BF16)
                scatter(i, slot, rows)

        @pl.when(i == n_tiles - 1)
        def _():
            wait_gather(1 - slot)
            wait_scatter(i, slot)

            @pl.when(i >= 1)
            def _():
                wait_scatter(jnp.maximum(i - 1, 0), 1 - slot)
        return carry

    lax.fori_loop(0, n_tiles, tile_step, 0)


def _experts(xg, meta, layer, w1, b1, w2, b2, n_exp):
    d = xg.shape[1]
    dff2 = w1.shape[-1]
    dff = dff2 // 2
    tm = EXPERT_TILE
    return pl.pallas_call(
        functools.partial(_expert_kernel, layer=layer),
        grid_spec=pltpu.PrefetchScalarGridSpec(
            num_scalar_prefetch=6,
            grid=(1,),
            in_specs=[
                pl.BlockSpec(memory_space=pl.ANY),
                pl.BlockSpec(memory_space=pl.ANY),
                _layer_spec(layer, n_exp, 1, dff2),
                pl.BlockSpec(memory_space=pl.ANY),
                _layer_spec(layer, n_exp, 1, d),
            ],
            out_specs=pl.BlockSpec(memory_space=pl.ANY),
            scratch_shapes=[
                pltpu.VMEM((2, tm, d), BF16), pltpu.VMEM((2, tm, d), BF16),
                pltpu.VMEM((2, d, dff2), F32), pltpu.VMEM((2, dff, d), F32),
                pltpu.VMEM((d, dff2), BF16), pltpu.VMEM((dff, d), BF16),
                pltpu.SemaphoreType.DMA((2,)), pltpu.SemaphoreType.DMA((2,)),
                pltpu.SemaphoreType.DMA((2, 2)),
                pltpu.SMEM((1,), I32),
            ],
        ),
        out_shape=jax.ShapeDtypeStruct(xg.shape, xg.dtype),
        input_output_aliases={6: 0},
        compiler_params=pltpu.CompilerParams(dimension_semantics=("arbitrary",), vmem_limit_bytes=VMEM_LIMIT),
        name="moe_experts",
    )(meta["tile_expert"], meta["n_tiles"], meta["tiles_end"], meta["tile_rows"], meta["piece_src"],
      meta["piece_dst"],
      xg, w1, b1, w2, b2)


def _combine_kernel(h2_ref, yg_ref, gw_ref, pos_ref, pp_ref, ps_ref, wpg_ref, bpg_ref, wpp_ref, gple_ref, gfin_ref,
                    *rest, final, batch):
    if final:
        outp_ref, outs_ref, prows, orows = rest
    else:
        hn_ref, prows = rest
    b = pl.program_id(0)
    nb = pl.num_programs(0)
    tb = h2_ref.shape[0]
    mp = yg_ref.shape[0]

    pos = pos_ref[...]
    gw = gw_ref[...]
    jio = lax.broadcasted_iota(I32, (mp, tb), 0)
    cg = jnp.zeros((mp, tb), F32)
    for k in range(TOP_K):
        cg = jnp.where(jio == pos[k:k + 1, :], gw[k:k + 1, :], cg)
    moe = lax.dot_general(cg.astype(BF16), yg_ref[...], (((0,), (0,)), ((), ())),
                          preferred_element_type=F32)
    h3 = h2_ref[...] + moe
    n3 = _rms(h3, gple_ref[...]).astype(BF16)
    gate = _sigmoid(_dot(n3, wpg_ref[...]) + bpg_ref[...])
    p_rows = jnp.where(b == nb - 1, ps_ref[0], _time_major_rows(pp_ref.at[0], prows, batch))
    h4 = h3 + gate * _dot(p_rows.astype(BF16), wpp_ref[...])
    if final:
        out = _rms(h4, gfin_ref[...])

        @pl.when(b < nb - 1)
        def _():
            _batch_major_store(out, orows, outp_ref, batch)

        @pl.when(b == nb - 1)
        def _():
            outs_ref[...] = out
    else:
        hn_ref[...] = h4


def _combine(h2, yg, gw, pos, p_prompt, p_sample, layer, w, g_final, n_exp, final):
    n_total, d = h2.shape
    _, batch, seq, ple = p_prompt.shape
    tb = TOKEN_BLOCK
    nb = n_total // tb
    steps = tb // batch
    const2 = lambda b: (0, 0)
    tok = pl.BlockSpec((SUBLANES, tb), lambda b: (0, b))
    row = pl.BlockSpec((tb, d), lambda b: (b, 0))
    prompt_blk = lambda b: jnp.minimum(b, nb - 2)
    if final:
        out_specs = [pl.BlockSpec((batch, steps, d), lambda b: (0, prompt_blk(b), 0)),
                     pl.BlockSpec((tb, d), const2)]
        out_shape = [jax.ShapeDtypeStruct((batch, seq, d), F32), jax.ShapeDtypeStruct((tb, d), F32)]
        extra_scratch = [pltpu.VMEM((d // LANES, tb, LANES), F32)]
    else:
        out_specs = [row]
        out_shape = [jax.ShapeDtypeStruct((n_total, d), F32)]
        extra_scratch = []
    return pl.pallas_call(
        functools.partial(_combine_kernel, final=final, batch=batch),
        grid=(nb,),
        in_specs=[
            row,
            pl.BlockSpec((_local_rows(n_exp), d), lambda b: (b, 0)),
            tok, tok,
            pl.BlockSpec((1, batch, steps, ple), lambda b: (layer, 0, prompt_blk(b), 0)),
            pl.BlockSpec((1, tb, ple), lambda b: (layer, 0, 0)),
            _layer_spec(layer, d, d), _layer_spec(layer, 1, d), _layer_spec(layer, ple, d),
            _layer_spec(layer, 1, d),
            pl.BlockSpec((1, d), const2),
        ],
        out_specs=out_specs,
        scratch_shapes=[pltpu.VMEM((ple // LANES, tb, LANES), F32)] + extra_scratch,
        out_shape=out_shape,
        compiler_params=pltpu.CompilerParams(dimension_semantics=("arbitrary",), vmem_limit_bytes=VMEM_LIMIT),
        name="moe_combine_ple",
    )(h2, yg, gw, pos, p_prompt, p_sample, w["w_pg"], w["b_pg"], w["w_pp"], w["g_ple"], g_final)


def _prepare_weights(batch, g_mix, w_in, g_v, w_s, b_s, conv_w, conv_b, w_a, b_a, w_x, b_x, lam,
                     g_oa, g_ob, w_out, g_ffn, w_r, b_r, g_ple, w_pg, b_pg, w_pp):
    depth, d = w_in.shape[0], w_in.shape[1]
    dh = d // 2
    gd = dh // GMLP_GROUPS
    n_exp = w_r.shape[2]
    row = lambda a: a.reshape(depth, 1, -1).astype(F32)
    wt = jnp.where(jnp.tril(jnp.ones((CHUNK, CHUNK), bool)), w_s, 0.0).astype(BF16)
    hr = HALF * batch
    row_t = jnp.arange(hr, dtype=I32) // batch
    row_b = jnp.arange(hr, dtype=I32) % batch
    expand_t = (row_t[:, None] == jnp.arange(HALF, dtype=I32)[None, :]).astype(BF16)
    same_b = row_b[:, None] == row_b[None, :]

    def kron_block(blk):
        rows = jnp.einsum("it,lgts->lgis", expand_t, blk, preferred_element_type=F32).astype(BF16)
        full = jnp.einsum("lgis,js->lgij", rows, expand_t, preferred_element_type=F32)
        return jnp.where(same_b, full, 0.0).astype(BF16)

    kd = jnp.stack([kron_block(wt[:, :, :HALF, :HALF]), kron_block(wt[:, :, HALF:, HALF:])], axis=1)
    k10 = kron_block(wt[:, :, HALF:, :HALF])
    bs_rows = jnp.repeat(jnp.repeat(b_s.transpose(0, 2, 1).astype(F32), gd, axis=2), batch, axis=1).reshape(
        depth, 2, hr, dh)
    hd = dh // LRU_HEADS
    head_of = jnp.arange(dh, dtype=I32) // hd
    same_head = head_of[:, None] == head_of[None, :]
    block_diag = lambda w: jnp.where(same_head, jnp.tile(w.astype(F32).reshape(depth, dh, hd), (1, 1, LRU_HEADS)),
                                     0.0)
    wa_f32, wx_f32 = block_diag(w_a), block_diag(w_x)
    return dict(
        g_mix=row(g_mix), w_in=w_in.astype(BF16), w_in_f32=w_in.astype(F32), g_v=row(g_v),
        kd=kd, k10=k10, bs_rows=bs_rows,
        w0_row=row(jnp.repeat(w_s[:, :, 0, 0], gd, axis=1)), b0_row=row(jnp.repeat(b_s[:, :, 0], gd, axis=1)),
        conv_w=conv_w.astype(F32), conv_b=row(conv_b),
        wa=wa_f32.astype(BF16), wa_f32=wa_f32, b_a=row(b_a), wx=wx_f32.astype(BF16), wx_f32=wx_f32,
        b_x=row(b_x), lam=row(lam),
        g_oa=row(g_oa), g_ob=row(g_ob), w_out=w_out.astype(BF16), w_out_f32=w_out.astype(F32),
        g_ffn=row(g_ffn),
        w_r=jnp.pad(w_r.astype(F32), ((0, 0), (0, 0), (0, LANES - n_exp))),
        b_r=b_r.reshape(depth, n_exp, 1).astype(F32),
        g_ple=row(g_ple), w_pg=w_pg.astype(BF16), b_pg=row(b_pg), w_pp=w_pp.astype(BF16))


def _layer_spec(layer, *shape):
    return pl.BlockSpec((None,) + shape, lambda *_: (layer,) + (0,) * len(shape))


def kernel(x_prompt, x_sample, state_conv, state_lru, p_prompt, p_sample, g_mix, w_in, g_v, w_s, b_s, conv_w, conv_b, w_a, b_a, w_x, b_x, lam, g_oa, g_ob, w_out, g_ffn, w_r, b_r, w1, b1, w2, b2, g_ple, w_pg, b_pg, w_pp, g_final):
    batch, seq, d = x_prompt.shape
    dec_batch, dec_seq, _ = x_sample.shape
    depth = w_in.shape[0]
    n_exp = w_r.shape[2]
    dh = d // 2
    ple = p_prompt.shape[-1]
    assert dec_seq == 1 and seq % CHUNK == 0 and batch % SUBLANES == 0
    assert dh == GMLP_GROUPS * LANES and dh % (LRU_HEADS * (LANES // 2)) == 0
    n_prompt = seq * batch
    assert n_prompt % TOKEN_BLOCK == 0
    n_valid = n_prompt + dec_batch
    n_total = _round_up(n_valid, TOKEN_BLOCK)
    s_rows = n_total - n_prompt
    assert s_rows == TOKEN_BLOCK
    nb = n_total // TOKEN_BLOCK
    max_tiles = _max_sorted_rows(n_valid, nb, n_exp) // EXPERT_TILE
    pad_rows = ((0, 0), (0, s_rows - dec_batch), (0, 0))

    h, h_sample, h_sample_block = x_prompt, jnp.pad(x_sample.reshape(1, dec_batch, d), pad_rows)[0], 0
    p_samp = jnp.pad(p_sample.reshape(depth, dec_batch, ple), pad_rows)
    sconv = jnp.pad(state_conv.transpose(0, 2, 1, 3), ((0, 0),) + pad_rows)
    slru = jnp.pad(state_lru, pad_rows)
    g_fin = g_final.reshape(1, d).astype(F32)
    b1_rows = b1.reshape(depth, n_exp, 1, -1)
    b2_rows = b2.reshape(depth, n_exp, 1, -1)

    w = _prepare_weights(batch, g_mix, w_in, g_v, w_s, b_s, conv_w, conv_b, w_a, b_a, w_x, b_x, lam,
                         g_oa, g_ob, w_out, g_ffn, w_r, b_r, g_ple, w_pg, b_pg, w_pp)
    v_p, conv_p, lru_p, v_s, conv_s, lru_s = [], [], [], [], [], []
    for l in range(depth):
        y, vp, cp, lp = _mix_prompt(h, n_prompt, batch, w, l)
        post_out = _post(y, h, w, l, n_total, n_valid, n_exp, batch)
        h2, xg, gw, pos, cnt, vs, cs, ls = _sample_front(
            h_sample, h_sample_block, post_out, nb - 1, w, l, sconv, slru, n_valid, n_exp)
        meta = _route_meta(cnt[:, 0, :].astype(I32), max_tiles)
        yg = _experts(xg, meta, l, w1, b1_rows, w2, b2_rows, n_exp)
        final = l == depth - 1
        res = _combine(h2, yg, gw, pos, p_prompt, p_samp, l, w, g_fin, n_exp, final)
        if final:
            y_prompt, out_sample = res
        else:
            h = h_sample = res[0]
            h_sample_block = nb - 1
        for acc, val in zip((v_p, conv_p, lru_p, v_s, conv_s, lru_s), (vp, cp, lp, vs, cs, ls)):
            acc.append(val)

    y_sample = out_sample[:dec_batch].reshape(dec_batch, 1, d)
    v_prompt = jnp.stack(v_p).reshape(depth, CHUNK, batch, dh).transpose(0, 2, 1, 3)
    conv_prompt = jnp.stack(conv_p).reshape(depth, CONV_W - 1, batch, dh).transpose(0, 2, 1, 3)
    v_sample = jnp.stack(v_s)[:, :dec_batch].reshape(depth, dec_batch, 1, dh)
    conv_sample = jnp.stack(conv_s)[:, :, :dec_batch].transpose(0, 2, 1, 3)
    return (y_prompt, y_sample, v_prompt, conv_prompt, jnp.stack(lru_p),
            v_sample, conv_sample, jnp.stack(lru_s)[:, :dec_batch])
```

```python
import functools

import jax
import jax.numpy as jnp
from jax import lax
from jax.experimental import pallas as pl
from jax.experimental.pallas import tpu as pltpu

F32 = jnp.float32
BF16 = jnp.bfloat16
I32 = jnp.int32

CHUNK = 128
HALF = CHUNK // 2
GMLP_GROUPS = 4
LRU_HEADS = 8
CONV_W = 4
LRU_C = 8.0
TOP_K = 4
SWIGLU_LIMIT = 7.0
SWIGLU_ALPHA = 1.702
EPS = 1e-6

LANES = 128
SUBLANES = 8
TOKEN_BLOCK = 256
EXPERT_TILE = 512
EXPERT_SUBTILE = 128
SEG_ALIGN = SUBLANES
WEIGHT_DMA_PRIORITY = 1
VMEM_LIMIT = 56 * 1024 * 1024


def _round_up(x, m):
    return (x + m - 1) // m * m


def _dot(a, b):
    return jnp.dot(a, b, preferred_element_type=F32)


def _gelu(x):
    return 0.5 * x * (1.0 + jnp.tanh(0.7978845608028654 * (x + 0.044715 * (x * x * x))))


def _sigmoid(x):
    return 1.0 / (1.0 + jnp.exp(-x))


def _softplus(x):
    return jnp.maximum(x, 0.0) + jnp.log1p(jnp.exp(-jnp.abs(x)))


def _rms(x, g):
    ms = jnp.mean(x * x, axis=-1, keepdims=True)
    return x * lax.rsqrt(ms + EPS) * g


def _group_norm_128(x, g):
    outs = []
    for j in range(x.shape[1] // LANES):
        blk = x[:, j * LANES:(j + 1) * LANES]
        ms = jnp.mean(blk * blk, axis=-1, keepdims=True)
        outs.append(blk * lax.rsqrt(ms + EPS))
    return jnp.concatenate(outs, axis=1) * g


def _group_norm_64(x, g):
    half = LANES // 2
    lo_mask = lax.broadcasted_iota(I32, (1, LANES), 1) < half
    outs = []
    for j in range(x.shape[1] // LANES):
        blk = x[:, j * LANES:(j + 1) * LANES]
        sq = blk * blk
        lo = jnp.sum(jnp.where(lo_mask, sq, 0.0), axis=-1, keepdims=True)
        hi = jnp.sum(jnp.where(lo_mask, 0.0, sq), axis=-1, keepdims=True)
        ms = jnp.where(lo_mask, lo, hi) * (1.0 / half)
        outs.append(blk * lax.rsqrt(ms + EPS))
    return jnp.concatenate(outs, axis=1) * g


def _time_major_rows(src_ref, slabs_ref, batch):
    steps = src_ref.shape[1]
    n_slab = slabs_ref.shape[0]
    for b in range(batch):
        blk = src_ref[b]
        for j in range(n_slab):
            slabs_ref[j, pl.ds(b, steps, stride=batch), :] = blk[:, j * LANES:(j + 1) * LANES]
    return jnp.concatenate([slabs_ref[j] for j in range(n_slab)], axis=1)


def _batch_major_store(rows, slabs_ref, dst_ref, batch):
    steps = dst_ref.shape[1]
    n_slab = slabs_ref.shape[0]
    for j in range(n_slab):
        slabs_ref[j] = rows[:, j * LANES:(j + 1) * LANES]
    for b in range(batch):
        dst_ref[b] = jnp.concatenate(
            [slabs_ref[j, pl.ds(b, steps, stride=batch), :] for j in range(n_slab)], axis=1)


def _split_bf16(x):
    hi = x.astype(BF16)
    lo = (x - hi.astype(F32)).astype(BF16)
    return hi, lo


def _dot_bf16(x, w):
    return _dot(x.astype(BF16), w)


def _dot_3pass(x, w):
    x_hi, x_lo = _split_bf16(x)
    w_hi, w_lo = _split_bf16(w)
    return _dot(x_hi, w_hi) + (_dot(x_lo, w_hi) + _dot(x_hi, w_lo))


def _lru_gates(xc, wa_ref, ba_ref, wx_ref, bx_ref, lam_ref, mm):
    r = _sigmoid(mm(xc, wa_ref[...]) + ba_ref[...])
    i = _sigmoid(mm(xc, wx_ref[...]) + bx_ref[...])
    log_a = (-LRU_C * r) * _softplus(-lam_ref[...])
    a = jnp.exp(log_a)
    mult = jnp.sqrt(-jnp.tanh(log_a) * (a * a + 1.0))
    return a, mult, i


def _mix_prompt_kernel(h_ref, gmix_ref, win_ref, gv_ref, kd_ref, k10_ref, bs_ref, cw_ref, cb_ref,
                       wa_ref, ba_ref, wx_ref, bx_ref, lam_ref, goa_ref, gob_ref,
                       y_ref, v_ref, conv_ref, lru_ref,
                       hstate, xpad, vprev, a_s, b_s, hs_s, hrows, z_even, z_odd, *, batch, batch_major):
    c = pl.program_id(0)
    rows = y_ref.shape[0]
    dh = gv_ref.shape[1]
    tail = (CONV_W - 1) * batch

    @pl.when(c == 0)
    def _():
        z_odd[...] = jnp.zeros_like(z_odd)
        vprev[...] = jnp.zeros_like(vprev)

    @pl.when(c <= 1)
    def _():
        hstate[...] = jnp.zeros_like(hstate)
        xpad[0:tail, :] = jnp.zeros((tail, dh), F32)

    def tile_step(z_in, z_out, par):
        if par == 0:
            vprev[...] = jnp.zeros_like(vprev)

        vn = _group_norm_128(_gelu(z_in[:, dh:2 * dh]), gv_ref[...])
        vb = vn.astype(BF16)
        s_parts = []
        for g in range(GMLP_GROUPS):
            sl = slice(g * LANES, (g + 1) * LANES)
            s_parts.append(_dot(kd_ref[par, g], vb[:, sl]) + _dot(k10_ref[g], vprev[:, sl]))
        s = jnp.concatenate(s_parts, axis=1) + bs_ref[par]
        vprev[...] = vb
        v_ref[...] = vn
        y_ref[:, 0:dh] = _group_norm_128(_gelu(z_in[:, 0:dh]) * s, goa_ref[...]).astype(BF16)

        xpad[tail:tail + rows, :] = z_in[:, 2 * dh:3 * dh]
        xc = cb_ref[...] + cw_ref[0:1, :] * xpad[0:rows, :]
        for k in range(1, CONV_W):
            xc = xc + cw_ref[k:k + 1, :] * xpad[k * batch:k * batch + rows, :]
        new_tail = xpad[rows:rows + tail, :]
        xpad[0:tail, :] = new_tail

        a, mult, i = _lru_gates(xc, wa_ref, ba_ref, wx_ref, bx_ref, lam_ref, _dot_bf16)
        row = lax.broadcasted_iota(I32, (rows, 1), 0)
        mult = jnp.where((c == 1) & (row < batch), 1.0, mult)
        a_s[...] = a
        b_s[...] = mult * i * xc

        def step(t, h):
            r0 = t * batch
            h = a_s[r0:r0 + batch, :] * h + b_s[r0:r0 + batch, :]
            hs_s[r0:r0 + batch, :] = h
            return h

        h_last = hstate[...]
        for t in range(rows // batch):
            h_last = step(t, h_last)
        hstate[...] = h_last

        y_ref[:, dh:2 * dh] = _group_norm_64(hs_s[...] * _gelu(z_in[:, 3 * dh:4 * dh]), gob_ref[...]).astype(BF16)
        conv_ref[...] = new_tail
        lru_ref[...] = h_last

        h = _time_major_rows(h_ref, hrows, batch) if batch_major else h_ref[...]
        z_out[...] = _dot(_rms(h, gmix_ref[...]).astype(BF16), win_ref[...])

    @pl.when(c % 2 == 0)
    def _():
        tile_step(z_odd, z_even, 1)

    @pl.when(c % 2 == 1)
    def _():
        tile_step(z_even, z_odd, 0)


def _mix_prompt(h, n_rows, batch, w, layer):
    batch_major = h.ndim == 3
    d = h.shape[-1]
    dh = d // 2
    rows = HALF * batch
    n_steps = n_rows // rows
    tail = (CONV_W - 1) * batch
    const2 = lambda c: (0, 0)
    lw = functools.partial(_layer_spec, layer)
    proj_tile = lambda c: jnp.minimum(c, n_steps - 1)
    mix_tile = lambda c: jnp.maximum(c - 1, 0)
    h_spec = (pl.BlockSpec((batch, HALF, d), lambda c: (0, proj_tile(c), 0)) if batch_major
              else pl.BlockSpec((rows, d), lambda c: (proj_tile(c), 0)))
    return pl.pallas_call(
        functools.partial(_mix_prompt_kernel, batch=batch, batch_major=batch_major),
        grid=(n_steps + 1,),
        in_specs=[
            h_spec,
            lw(1, d), lw(d, 2 * d), lw(1, dh),
            lw(2, GMLP_GROUPS, rows, rows), lw(GMLP_GROUPS, rows, rows), lw(2, rows, dh),
            lw(CONV_W, dh), lw(1, dh),
            lw(dh, dh), lw(1, dh), lw(dh, dh), lw(1, dh),
            lw(1, dh), lw(1, dh), lw(1, dh),
        ],
        out_specs=[
            pl.BlockSpec((rows, d), lambda c: (mix_tile(c), 0)),
            pl.BlockSpec((rows, dh), lambda c: (jnp.maximum(mix_tile(c) - (n_steps - 2), 0), 0)),
            pl.BlockSpec((tail, dh), const2),
            pl.BlockSpec((batch, dh), const2),
        ],
        out_shape=[
            jax.ShapeDtypeStruct((n_rows, d), BF16),
            jax.ShapeDtypeStruct((2 * rows, dh), F32),
            jax.ShapeDtypeStruct((tail, dh), F32),
            jax.ShapeDtypeStruct((batch, dh), F32),
        ],
        scratch_shapes=[
            pltpu.VMEM((batch, dh), F32),
            pltpu.VMEM((tail + rows, dh), F32),
            pltpu.VMEM((rows, dh), BF16),
            pltpu.VMEM((rows, dh), F32),
            pltpu.VMEM((rows, dh), F32),
            pltpu.VMEM((rows, dh), F32),
            pltpu.VMEM((d // LANES, rows, LANES), F32),
            pltpu.VMEM((rows, 2 * d), F32),
            pltpu.VMEM((rows, 2 * d), F32),
        ],
        compiler_params=pltpu.CompilerParams(dimension_semantics=("arbitrary",), vmem_limit_bytes=VMEM_LIMIT),
        name="mix_prompt",
    )(h, w["g_mix"], w["w_in"], w["g_v"], w["kd"], w["k10"], w["bs_rows"], w["conv_w"], w["conv_b"],
      w["wa"], w["b_a"], w["wx"], w["b_x"], w["lam"], w["g_oa"], w["g_ob"])


def _sample_front_kernel(h_ref, h2_any, xg_any, gw_any, pos_any, cnt_any,
                         gmix_ref, win_ref, gv_ref, w0_ref, b0_ref, cw_ref, cb_ref,
                         wa_ref, ba_ref, wx_ref, bx_ref, lam_ref, goa_ref, gob_ref, sconv_ref, slru_ref,
                         wout_ref, gffn_ref, wr_ref, br_ref,
                         h2_ref, xg_ref, gw_ref, pos_ref, cnt_ref, v_ref, conv_ref, lru_ref,
                         *, block, n_valid, n_exp):
    del h2_any, xg_any, gw_any, pos_any, cnt_any
    dh = gv_ref.shape[1]
    h = h_ref[...]
    n = _rms(h, gmix_ref[...])
    proj = lambda lo: _dot_3pass(n, win_ref[:, lo:lo + dh])

    vn = _group_norm_128(_gelu(proj(dh)), gv_ref[...])
    v_ref[...] = vn
    s = vn * w0_ref[...] + b0_ref[...]
    ya = _group_norm_128(_gelu(proj(0)) * s, goa_ref[...])

    xb = proj(2 * dh)
    xc = cb_ref[...] + cw_ref[CONV_W - 1:CONV_W, :] * xb
    for k in range(CONV_W - 1):
        xc = xc + cw_ref[k:k + 1, :] * sconv_ref[k]
    for k in range(CONV_W - 2):
        conv_ref[k] = sconv_ref[k + 1]
    conv_ref[CONV_W - 2] = xb

    a, mult, i = _lru_gates(xc, wa_ref, ba_ref, wx_ref, bx_ref, lam_ref, _dot_3pass)
    h_new = a * slru_ref[...] + mult * i * xc
    lru_ref[...] = h_new
    yb = _group_norm_64(h_new * _gelu(proj(3 * dh)), gob_ref[...])

    h2 = h + _dot_3pass(ya, wout_ref[0:dh, :]) + _dot_3pass(yb, wout_ref[dh:2 * dh, :])
    _route_block(h2, gffn_ref, wr_ref, br_ref, h2_ref, xg_ref, gw_ref, pos_ref, cnt_ref,
                 block=block, n_valid=n_valid, n_exp=n_exp)


def _sample_front(h, h_block, post_out, block, w, layer, sconv, slru, n_valid, n_exp):
    d = h.shape[1]
    dh = d // 2
    rows = TOKEN_BLOCK
    const2 = lambda i: (0, 0)
    const3 = lambda i: (0, 0, 0)
    lw = functools.partial(_layer_spec, layer)
    vec = lw(1, dh)
    anyspace = pl.BlockSpec(memory_space=pl.ANY)
    blk_row = pl.BlockSpec((rows, d), lambda i: (block, 0))
    blk_tok = pl.BlockSpec((SUBLANES, rows), lambda i: (0, block))
    return pl.pallas_call(
        functools.partial(_sample_front_kernel, block=block, n_valid=n_valid, n_exp=n_exp),
        grid=(1,),
        in_specs=[
            pl.BlockSpec((rows, d), lambda i: (h_block, 0)),
            anyspace, anyspace, anyspace, anyspace, anyspace,
            lw(1, d), lw(d, 2 * d),
            vec, vec, vec,
            lw(CONV_W, dh), vec,
            lw(dh, dh), vec, lw(dh, dh), vec,
            vec, vec, vec,
            lw(CONV_W - 1, rows, dh), lw(rows, dh),
            lw(d, d), lw(1, d), lw(d, LANES), lw(n_exp, 1),
        ],
        out_specs=[
            blk_row, pl.BlockSpec((_local_rows(n_exp), d), lambda i: (block, 0)), blk_tok, blk_tok,
            pl.BlockSpec((1, SUBLANES, n_exp), lambda i: (block, 0, 0)),
            pl.BlockSpec((rows, dh), const2),
            pl.BlockSpec((CONV_W - 1, rows, dh), const3),
            pl.BlockSpec((rows, dh), const2),
        ],
        out_shape=[jax.ShapeDtypeStruct(a.shape, a.dtype) for a in post_out] + [
            jax.ShapeDtypeStruct((rows, dh), F32),
            jax.ShapeDtypeStruct((CONV_W - 1, rows, dh), F32),
            jax.ShapeDtypeStruct((rows, dh), F32),
        ],
        input_output_aliases={1 + k: k for k in range(len(post_out))},
        compiler_params=pltpu.CompilerParams(dimension_semantics=("arbitrary",), vmem_limit_bytes=VMEM_LIMIT),
        name="sample_front",
    )(h, *post_out, w["g_mix"], w["w_in_f32"], w["g_v"], w["w0_row"], w["b0_row"], w["conv_w"], w["conv_b"],
      w["wa_f32"], w["b_a"], w["wx_f32"], w["b_x"], w["lam"], w["g_oa"], w["g_ob"], sconv, slru,
      w["w_out_f32"], w["g_ffn"], w["w_r"], w["b_r"])


def _route_block(h2, gffn_ref, wr_ref, br_ref, h2_ref, xg_ref, gw_ref, pos_ref, cnt_ref,
                 *, block, n_valid, n_exp):
    tb = h2.shape[0]
    h2_ref[...] = h2
    n2 = _rms(h2, gffn_ref[...])

    n_hi, n_lo = _split_bf16(n2)
    w_hi, w_lo = _split_bf16(wr_ref[...])
    both = _dot(n_hi, jnp.concatenate([w_hi, w_lo], axis=1))
    logits = both[:, 0:LANES] + (_dot(n_lo, w_hi) + both[:, LANES:2 * LANES])
    lt = logits.T[0:n_exp, :] + br_ref[...]

    eio = lax.broadcasted_iota(I32, (n_exp, tb), 0).astype(F32)
    col = block * tb + lax.broadcasted_iota(I32, (1, tb), 1)
    valid = col < n_valid
    vals, hots = [], []
    for _ in range(TOP_K):
        m = jnp.max(lt, axis=0, keepdims=True)
        sel = jnp.min(jnp.where(lt == m, eio, float(n_exp)), axis=0, keepdims=True)
        hot = eio == sel
        lt = jnp.where(hot, -jnp.inf, lt)
        vals.append(m)
        hots.append(hot)
    exps = [jnp.exp(v - vals[0]) for v in vals]
    den = exps[0]
    for e in exps[1:]:
        den = den + e

    occ = jnp.zeros((n_exp, tb), F32)
    for hot in hots:
        occ = occ + jnp.where(hot & valid, 1.0, 0.0)
    occ_b = occ.astype(BF16)
    before = (lax.broadcasted_iota(I32, (tb, tb), 0) < lax.broadcasted_iota(I32, (tb, tb), 1))
    ranks_e = _dot(occ_b, jnp.where(before, 1.0, 0.0).astype(BF16))
    cnt_ref[0] = lax.dot_general(jnp.ones((SUBLANES, tb), BF16), occ_b, (((1,), (1,)), ((), ())),
                                 preferred_element_type=F32)
    cnt_col = jnp.sum(occ, axis=1, keepdims=True)
    units = jnp.floor((cnt_col + (SEG_ALIGN - 1.0)) * (1.0 / SEG_ALIGN))
    earlier = (lax.broadcasted_iota(I32, (n_exp, n_exp), 1) < lax.broadcasted_iota(I32, (n_exp, n_exp), 0))
    seg_off = _dot(jnp.where(earlier, 1.0, 0.0).astype(BF16),
                   jnp.broadcast_to(units, (n_exp, LANES)).astype(BF16))[:, 0:1] * float(SEG_ALIGN)
    rows_e = ranks_e + seg_off

    rio = lax.broadcasted_iota(I32, (SUBLANES, tb), 0)
    gw_out = jnp.zeros((SUBLANES, tb), F32)
    pos_out = jnp.full((SUBLANES, tb), -1, I32)
    for k in range(TOP_K):
        pos_k = jnp.sum(jnp.where(hots[k], rows_e, 0.0), axis=0, keepdims=True).astype(I32)
        gw_out = jnp.where(rio == k, exps[k] / den, gw_out)
        pos_out = jnp.where(rio == k, jnp.where(valid, pos_k, -1), pos_out)
    gw_ref[...] = gw_out
    pos_ref[...] = pos_out

    jio = lax.broadcasted_iota(I32, (xg_ref.shape[0], tb), 0)
    hit = jio == pos_out[0:1, :]
    for k in range(1, TOP_K):
        hit = hit | (jio == pos_out[k:k + 1, :])
    xg_ref[...] = _dot(jnp.where(hit, 1.0, 0.0).astype(BF16), n_hi).astype(BF16)


def _post_kernel(y_ref, h_ref, wout_ref, gffn_ref, wr_ref, br_ref,
                 h2_ref, xg_ref, gw_ref, pos_ref, cnt_ref, hrows, *, n_valid, n_exp, batch, batch_major):
    h = _time_major_rows(h_ref, hrows, batch) if batch_major else h_ref[...]
    h2 = h + _dot(y_ref[...], wout_ref[...])
    _route_block(h2, gffn_ref, wr_ref, br_ref, h2_ref, xg_ref, gw_ref, pos_ref, cnt_ref,
                 block=pl.program_id(0), n_valid=n_valid, n_exp=n_exp)


def _post(y, h, w, layer, n_total, n_valid, n_exp, batch):
    batch_major = h.ndim == 3
    d = h.shape[-1]
    tb = TOKEN_BLOCK
    nb = n_total // tb
    const2 = lambda b: (0, 0)
    tok = pl.BlockSpec((SUBLANES, tb), lambda b: (0, b))
    h_spec = (pl.BlockSpec((batch, tb // batch, d), lambda b: (0, b, 0)) if batch_major
              else pl.BlockSpec((tb, d), lambda b: (b, 0)))
    return pl.pallas_call(
        functools.partial(_post_kernel, n_valid=n_valid, n_exp=n_exp, batch=batch, batch_major=batch_major),
        grid=(y.shape[0] // tb,),
        in_specs=[
            pl.BlockSpec((tb, d), lambda b: (b, 0)),
            h_spec,
            _layer_spec(layer, d, d), _layer_spec(layer, 1, d), _layer_spec(layer, d, LANES),
            _layer_spec(layer, n_exp, 1),
        ],
        out_specs=[
            pl.BlockSpec((tb, d), lambda b: (b, 0)),
            pl.BlockSpec((_local_rows(n_exp), d), lambda b: (b, 0)),
            tok, tok,
            pl.BlockSpec((1, SUBLANES, n_exp), lambda b: (b, 0, 0)),
        ],
        out_shape=[
            jax.ShapeDtypeStruct((n_total, d), F32),
            jax.ShapeDtypeStruct((_grouped_rows(nb, n_exp), d), BF16),
            jax.ShapeDtypeStruct((SUBLANES, n_total), F32),
            jax.ShapeDtypeStruct((SUBLANES, n_total), I32),
            jax.ShapeDtypeStruct((nb, SUBLANES, n_exp), F32),
        ],
        scratch_shapes=[pltpu.VMEM((d // LANES, tb, LANES), F32)],
        compiler_params=pltpu.CompilerParams(dimension_semantics=("arbitrary",), vmem_limit_bytes=VMEM_LIMIT),
        name="post_router",
    )(y, h, w["w_out"], w["g_ffn"], w["w_r"], w["b_r"])


def _max_sorted_rows(n_valid, nb, n_exp):
    worst = TOP_K * n_valid + (SEG_ALIGN - 1) * nb * n_exp + (EXPERT_TILE - SEG_ALIGN) * n_exp
    return _round_up(worst, EXPERT_TILE)


def _local_rows(n_exp):
    return _round_up(TOP_K * TOKEN_BLOCK + (SEG_ALIGN - 1) * n_exp, LANES)


def _dump_rows(n_exp):
    return n_exp * EXPERT_SUBTILE


def _grouped_rows(nb, n_exp):
    return nb * _local_rows(n_exp) + _dump_rows(n_exp)


def _select_columns(per_block_expert, expert_of):
    n_exp = per_block_expert.shape[1]
    one_hot = (expert_of[:, None] == jnp.arange(n_exp, dtype=I32)[None, :]).astype(F32)
    picked = jnp.dot(one_hot, per_block_expert.T.astype(F32), precision=lax.Precision.HIGHEST)
    return jnp.round(picked).astype(I32)


def _route_meta(cnt, max_tiles):
    nb, n_exp = cnt.shape
    mp = _local_rows(n_exp)
    p8 = (cnt + SEG_ALIGN - 1) // SEG_ALIGN * SEG_ALIGN
    off = jnp.cumsum(p8, axis=1) - p8
    tot = jnp.sum(p8, axis=1)
    seg = jnp.sum(p8, axis=0)
    reg = (seg + EXPERT_TILE - 1) // EXPERT_TILE * EXPERT_TILE
    reg_start = jnp.cumsum(reg) - reg
    tiles_end = jnp.cumsum(reg // EXPERT_TILE)
    tile_ids = jnp.arange(max_tiles, dtype=I32)
    tile_expert = jnp.minimum(jnp.sum((tiles_end[None, :] <= tile_ids[:, None]).astype(I32), axis=1), n_exp - 1)
    of_tile = tile_expert[:, None] == jnp.arange(n_exp, dtype=I32)[None, :]
    pick = lambda per_expert: jnp.sum(jnp.where(of_tile, per_expert[None, :], 0), axis=1)
    used = jnp.clip(pick(seg) - (tile_ids - pick(tiles_end - reg // EXPERT_TILE)) * EXPERT_TILE, 0, EXPERT_TILE)
    tile_rows = (used + EXPERT_SUBTILE - 1) // EXPERT_SUBTILE * EXPERT_SUBTILE

    pieces_per_tile = EXPERT_TILE // SEG_ALIGN
    piece_expert = jnp.repeat(tile_expert, pieces_per_tile)
    piece_tile = jnp.repeat(tile_ids, pieces_per_tile)
    row = jnp.arange(max_tiles * pieces_per_tile, dtype=I32) * SEG_ALIGN
    is_e = piece_expert[:, None] == jnp.arange(n_exp, dtype=I32)[None, :]
    of_expert = lambda per_expert: jnp.sum(jnp.where(is_e, per_expert[None, :], 0), axis=1)
    rin = row - of_expert(reg_start)
    seg_e = of_expert(seg)
    ends = _select_columns(jnp.cumsum(p8, axis=0), piece_expert)
    blk = jnp.minimum(jnp.sum((ends <= rin[:, None]).astype(I32), axis=1), nb - 1)
    is_b = blk[:, None] == jnp.arange(nb, dtype=I32)[None, :]
    of_block = lambda a: jnp.sum(jnp.where(is_b, a, 0), axis=1)
    seg_first = of_block(ends - _select_columns(p8, piece_expert))
    local = of_block(_select_columns(off, piece_expert)) + rin - seg_first
    pad_k = jnp.clip((rin - seg_e) // SEG_ALIGN, 0, EXPERT_SUBTILE // SEG_ALIGN - 1)
    real = (rin < seg_e) & (piece_tile < tiles_end[-1])
    home = blk * mp + local
    src = jnp.where(real, home, mp - SEG_ALIGN)
    dst = jnp.where(real, home, nb * mp + (piece_expert * (EXPERT_SUBTILE // SEG_ALIGN) + pad_k) * SEG_ALIGN)
    look_ahead = jnp.full((pieces_per_tile,), mp - SEG_ALIGN, I32)
    return dict(piece_src=jnp.concatenate([src.astype(I32), look_ahead]), piece_dst=dst.astype(I32),
                tot=tot.astype(I32),
                tile_expert=tile_expert.astype(I32), n_tiles=tiles_end[-1:].astype(I32),
                tiles_end=tiles_end.astype(I32), tile_rows=tile_rows.astype(I32))


def _wait_rows(n_rows, min_rows, max_rows, src_rows, dst_rows, sem):
    size = min_rows
    while size <= max_rows:
        @pl.when((n_rows & size) != 0)
        def _(size=size):
            pltpu.make_async_copy(src_rows(size), dst_rows(size), sem).wait()
        size *= 2


def _expert_kernel(te_ref, nt_ref, tend_ref, rows_ref, src_ref, dst_ref,
                   xg_hbm, w1_hbm, b1_ref, w2_hbm, b2_ref, yg_hbm,
                   xbuf, ybuf, w1s, w2s, w1b, w2b, xsem, ysem, sems, n_loaded, *, layer):
    n_tiles = nt_ref[0]
    dff = w2b.shape[0]
    tm = xbuf.shape[1]
    pieces = tm // SEG_ALIGN

    def weight_copies(expert, slot):
        return (pltpu.make_async_copy(w1_hbm.at[layer, expert], w1s.at[slot], sems.at[0, slot]),
                pltpu.make_async_copy(w2_hbm.at[layer, expert], w2s.at[slot], sems.at[1, slot]))

    def gather(tile, sl):
        for j in range(pieces):
            row = pl.multiple_of(src_ref[tile * pieces + j], SEG_ALIGN)
            pltpu.make_async_copy(xg_hbm.at[pl.ds(row, SEG_ALIGN), :],
                                  xbuf.at[sl, pl.ds(j * SEG_ALIGN, SEG_ALIGN), :], xsem.at[sl]).start()

    def scatter(tile, sl, rows):
        for j in range(rows // SEG_ALIGN):
            row = pl.multiple_of(dst_ref[tile * pieces + j], SEG_ALIGN)
            pltpu.make_async_copy(ybuf.at[sl, pl.ds(j * SEG_ALIGN, SEG_ALIGN), :],
                                  yg_hbm.at[pl.ds(row, SEG_ALIGN), :], ysem.at[sl]).start()

    def wait_gather(sl):
        pltpu.make_async_copy(xg_hbm.at[pl.ds(0, tm), :], xbuf.at[sl], xsem.at[sl]).wait()

    def wait_scatter(tile, sl):
        _wait_rows(rows_ref[tile], EXPERT_SUBTILE, tm, lambda n: ybuf.at[sl, pl.ds(0, n), :],
                   lambda n: yg_hbm.at[pl.ds(0, n), :], ysem.at[sl])

    def tile_step(i, carry):
        slot = i % 2
        expert = te_ref[i]

        @pl.when(i == 0)
        def _():
            n_loaded[0] = 0
            for c in weight_copies(expert, 0):
                c.start(priority=WEIGHT_DMA_PRIORITY)
            gather(0, 0)

        wait_gather(slot)

        @pl.when(i >= 2)
        def _():
            wait_scatter(jnp.maximum(i - 2, 0), slot)

        @pl.when((i == 0) | (expert != te_ref[jnp.maximum(i - 1, 0)]))
        def _():
            slot = n_loaded[0] % 2
            n_loaded[0] = n_loaded[0] + 1
            for c in weight_copies(expert, slot):
                c.wait()
            nxt = tend_ref[expert]

            @pl.when(nxt < n_tiles)
            def _():
                for c in weight_copies(te_ref[jnp.minimum(nxt, n_tiles - 1)], 1 - slot):
                    c.start(priority=WEIGHT_DMA_PRIORITY)
            w1b[...] = w1s[slot].astype(BF16)
            w2b[...] = w2s[slot].astype(BF16)

        for rows in range(EXPERT_SUBTILE, tm + 1, EXPERT_SUBTILE):
            @pl.when(rows_ref[i] == rows)
            def _(rows=rows):
                gather(i + 1, 1 - slot)
                hdn = _dot(xbuf[slot, 0:rows, :], w1b[...]) + b1_ref[expert]
                gate = jnp.minimum(hdn[:, 0:dff], SWIGLU_LIMIT)
                up = jnp.clip(hdn[:, dff:2 * dff], -SWIGLU_LIMIT, SWIGLU_LIMIT)
                act = (up + 1.0) * gate * _sigmoid(SWIGLU_ALPHA * gate)
                ybuf[slot, 0:rows, :] = (_dot(act.astype(BF16), w2b[...]) + b2_ref[expert]).astype(BF16)
                scatter(i, slot, rows)

        @pl.when(i == n_tiles - 1)
        def _():
            wait_gather(1 - slot)
            wait_scatter(i, slot)

            @pl.when(i >= 1)
            def _():
                wait_scatter(jnp.maximum(i - 1, 0), 1 - slot)
        return carry

    lax.fori_loop(0, n_tiles, tile_step, 0)


def _experts(xg, meta, layer, w1, b1, w2, b2, n_exp):
    d = xg.shape[1]
    dff2 = w1.shape[-1]
    dff = dff2 // 2
    tm = EXPERT_TILE
    return pl.pallas_call(
        functools.partial(_expert_kernel, layer=layer),
        grid_spec=pltpu.PrefetchScalarGridSpec(
            num_scalar_prefetch=6,
            grid=(1,),
            in_specs=[
                pl.BlockSpec(memory_space=pl.ANY),
                pl.BlockSpec(memory_space=pl.ANY),
                _layer_spec(layer, n_exp, 1, dff2),
                pl.BlockSpec(memory_space=pl.ANY),
                _layer_spec(layer, n_exp, 1, d),
            ],
            out_specs=pl.BlockSpec(memory_space=pl.ANY),
            scratch_shapes=[
                pltpu.VMEM((2, tm, d), BF16), pltpu.VMEM((2, tm, d), BF16),
                pltpu.VMEM((2, d, dff2), F32), pltpu.VMEM((2, dff, d), F32),
                pltpu.VMEM((d, dff2), BF16), pltpu.VMEM((dff, d), BF16),
                pltpu.SemaphoreType.DMA((2,)), pltpu.SemaphoreType.DMA((2,)),
                pltpu.SemaphoreType.DMA((2, 2)),
                pltpu.SMEM((1,), I32),
            ],
        ),
        out_shape=jax.ShapeDtypeStruct(xg.shape, xg.dtype),
        input_output_aliases={6: 0},
        compiler_params=pltpu.CompilerParams(dimension_semantics=("arbitrary",), vmem_limit_bytes=VMEM_LIMIT),
        name="moe_experts",
    )(meta["tile_expert"], meta["n_tiles"], meta["tiles_end"], meta["tile_rows"], meta["piece_src"],
      meta["piece_dst"],
      xg, w1, b1, w2, b2)


def _combine_kernel(h2_ref, yg_ref, gw_ref, pos_ref, pp_ref, ps_ref, wpg_ref, bpg_ref, wpp_ref, gple_ref, gfin_ref,
                    *rest, final, batch):
    if final:
        outp_ref, outs_ref, prows, orows = rest
    else:
        hn_ref, prows = rest
    b = pl.program_id(0)
    nb = pl.num_programs(0)
    tb = h2_ref.shape[0]
    mp = yg_ref.shape[0]

    pos = pos_ref[...]
    gw = gw_ref[...]
    jio = lax.broadcasted_iota(I32, (mp, tb), 0)
    cg = jnp.zeros((mp, tb), F32)
    for k in range(TOP_K):
        cg = jnp.where(jio == pos[k:k + 1, :], gw[k:k + 1, :], cg)
    moe = lax.dot_general(cg.astype(BF16), yg_ref[...], (((0,), (0,)), ((), ())),
                          preferred_element_type=F32)
    h3 = h2_ref[...] + moe
    n3 = _rms(h3, gple_ref[...]).astype(BF16)
    gate = _sigmoid(_dot(n3, wpg_ref[...]) + bpg_ref[...])
    p_rows = jnp.where(b == nb - 1, ps_ref[0], _time_major_rows(pp_ref.at[0], prows, batch))
    h4 = h3 + gate * _dot(p_rows.astype(BF16), wpp_ref[...])
    if final:
        out = _rms(h4, gfin_ref[...])

        @pl.when(b < nb - 1)
        def _():
            _batch_major_store(out, orows, outp_ref, batch)

        @pl.when(b == nb - 1)
        def _():
            outs_ref[...] = out
    else:
        hn_ref[...] = h4


def _combine(h2, yg, gw, pos, p_prompt, p_sample, layer, w, g_final, n_exp, final):
    n_total, d = h2.shape
    _, batch, seq, ple = p_prompt.shape
    tb = TOKEN_BLOCK
    nb = n_total // tb
    steps = tb // batch
    const2 = lambda b: (0, 0)
    tok = pl.BlockSpec((SUBLANES, tb), lambda b: (0, b))
    row = pl.BlockSpec((tb, d), lambda b: (b, 0))
    prompt_blk = lambda b: jnp.minimum(b, nb - 2)
    if final:
        out_specs = [pl.BlockSpec((batch, steps, d), lambda b: (0, prompt_blk(b), 0)),
                     pl.BlockSpec((tb, d), const2)]
        out_shape = [jax.ShapeDtypeStruct((batch, seq, d), F32), jax.ShapeDtypeStruct((tb, d), F32)]
        extra_scratch = [pltpu.VMEM((d // LANES, tb, LANES), F32)]
    else:
        out_specs = [row]
        out_shape = [jax.ShapeDtypeStruct((n_total, d), F32)]
        extra_scratch = []
    return pl.pallas_call(
        functools.partial(_combine_kernel, final=final, batch=batch),
        grid=(nb,),
        in_specs=[
            row,
            pl.BlockSpec((_local_rows(n_exp), d), lambda b: (b, 0)),
            tok, tok,
            pl.BlockSpec((1, batch, steps, ple), lambda b: (layer, 0, prompt_blk(b), 0)),
            pl.BlockSpec((1, tb, ple), lambda b: (layer, 0, 0)),
            _layer_spec(layer, d, d), _layer_spec(layer, 1, d), _layer_spec(layer, ple, d),
            _layer_spec(layer, 1, d),
            pl.BlockSpec((1, d), const2),
        ],
        out_specs=out_specs,
        scratch_shapes=[pltpu.VMEM((ple // LANES, tb, LANES), F32)] + extra_scratch,
        out_shape=out_shape,
        compiler_params=pltpu.CompilerParams(dimension_semantics=("arbitrary",), vmem_limit_bytes=VMEM_LIMIT),
        name="moe_combine_ple",
    )(h2, yg, gw, pos, p_prompt, p_sample, w["w_pg"], w["b_pg"], w["w_pp"], w["g_ple"], g_final)


def _prepare_weights(batch, g_mix, w_in, g_v, w_s, b_s, conv_w, conv_b, w_a, b_a, w_x, b_x, lam,
                     g_oa, g_ob, w_out, g_ffn, w_r, b_r, g_ple, w_pg, b_pg, w_pp):
    depth, d = w_in.shape[0], w_in.shape[1]
    dh = d // 2
    gd = dh // GMLP_GROUPS
    n_exp = w_r.shape[2]
    row = lambda a: a.reshape(depth, 1, -1).astype(F32)
    wt = jnp.where(jnp.tril(jnp.ones((CHUNK, CHUNK), bool)), w_s, 0.0).astype(BF16)
    hr = HALF * batch
    row_t = jnp.arange(hr, dtype=I32) // batch
    row_b = jnp.arange(hr, dtype=I32) % batch
    expand_t = (row_t[:, None] == jnp.arange(HALF, dtype=I32)[None, :]).astype(BF16)
    same_b = row_b[:, None] == row_b[None, :]

    def kron_block(blk):
        rows = jnp.einsum("it,lgts->lgis", expand_t, blk, preferred_element_type=F32).astype(BF16)
        full = jnp.einsum("lgis,js->lgij", rows, expand_t, preferred_element_type=F32)
        return jnp.where(same_b, full, 0.0).astype(BF16)

    kd = jnp.stack([kron_block(wt[:, :, :HALF, :HALF]), kron_block(wt[:, :, HALF:, HALF:])], axis=1)
    k10 = kron_block(wt[:, :, HALF:, :HALF])
    bs_rows = jnp.repeat(jnp.repeat(b_s.transpose(0, 2, 1).astype(F32), gd, axis=2), batch, axis=1).reshape(
        depth, 2, hr, dh)
    hd = dh // LRU_HEADS
    head_of = jnp.arange(dh, dtype=I32) // hd
    same_head = head_of[:, None] == head_of[None, :]
    block_diag = lambda w: jnp.where(same_head, jnp.tile(w.astype(F32).reshape(depth, dh, hd), (1, 1, LRU_HEADS)),
                                     0.0)
    wa_f32, wx_f32 = block_diag(w_a), block_diag(w_x)
    return dict(
        g_mix=row(g_mix), w_in=w_in.astype(BF16), w_in_f32=w_in.astype(F32), g_v=row(g_v),
        kd=kd, k10=k10, bs_rows=bs_rows,
        w0_row=row(jnp.repeat(w_s[:, :, 0, 0], gd, axis=1)), b0_row=row(jnp.repeat(b_s[:, :, 0], gd, axis=1)),
        conv_w=conv_w.astype(F32), conv_b=row(conv_b),
        wa=wa_f32.astype(BF16), wa_f32=wa_f32, b_a=row(b_a), wx=wx_f32.astype(BF16), wx_f32=wx_f32,
        b_x=row(b_x), lam=row(lam),
        g_oa=row(g_oa), g_ob=row(g_ob), w_out=w_out.astype(BF16), w_out_f32=w_out.astype(F32),
        g_ffn=row(g_ffn),
        w_r=jnp.pad(w_r.astype(F32), ((0, 0), (0, 0), (0, LANES - n_exp))),
        b_r=b_r.reshape(depth, n_exp, 1).astype(F32),
        g_ple=row(g_ple), w_pg=w_pg.astype(BF16), b_pg=row(b_pg), w_pp=w_pp.astype(BF16))


def _layer_spec(layer, *shape):
    return pl.BlockSpec((None,) + shape, lambda *_: (layer,) + (0,) * len(shape))


def kernel(x_prompt, x_sample, state_conv, state_lru, p_prompt, p_sample, g_mix, w_in, g_v, w_s, b_s, conv_w, conv_b, w_a, b_a, w_x, b_x, lam, g_oa, g_ob, w_out, g_ffn, w_r, b_r, w1, b1, w2, b2, g_ple, w_pg, b_pg, w_pp, g_final):
    batch, seq, d = x_prompt.shape
    dec_batch, dec_seq, _ = x_sample.shape
    depth = w_in.shape[0]
    n_exp = w_r.shape[2]
    dh = d // 2
    ple = p_prompt.shape[-1]
    assert dec_seq == 1 and seq % CHUNK == 0 and batch % SUBLANES == 0
    assert dh == GMLP_GROUPS * LANES and dh % (LRU_HEADS * (LANES // 2)) == 0
    n_prompt = seq * batch
    assert n_prompt % TOKEN_BLOCK == 0
    n_valid = n_prompt + dec_batch
    n_total = _round_up(n_valid, TOKEN_BLOCK)
    s_rows = n_total - n_prompt
    assert s_rows == TOKEN_BLOCK
    nb = n_total // TOKEN_BLOCK
    max_tiles = _max_sorted_rows(n_valid, nb, n_exp) // EXPERT_TILE
    pad_rows = ((0, 0), (0, s_rows - dec_batch), (0, 0))

    h, h_sample, h_sample_block = x_prompt, jnp.pad(x_sample.reshape(1, dec_batch, d), pad_rows)[0], 0
    p_samp = jnp.pad(p_sample.reshape(depth, dec_batch, ple), pad_rows)
    sconv = jnp.pad(state_conv.transpose(0, 2, 1, 3), ((0, 0),) + pad_rows)
    slru = jnp.pad(state_lru, pad_rows)
    g_fin = g_final.reshape(1, d).astype(F32)
    b1_rows = b1.reshape(depth, n_exp, 1, -1)
    b2_rows = b2.reshape(depth, n_exp, 1, -1)

    w = _prepare_weights(batch, g_mix, w_in, g_v, w_s, b_s, conv_w, conv_b, w_a, b_a, w_x, b_x, lam,
                         g_oa, g_ob, w_out, g_ffn, w_r, b_r, g_ple, w_pg, b_pg, w_pp)
    v_p, conv_p, lru_p, v_s, conv_s, lru_s = [], [], [], [], [], []
    for l in range(depth):
        y, vp, cp, lp = _mix_prompt(h, n_prompt, batch, w, l)
        post_out = _post(y, h, w, l, n_total, n_valid, n_exp, batch)
        h2, xg, gw, pos, cnt, vs, cs, ls = _sample_front(
            h_sample, h_sample_block, post_out, nb - 1, w, l, sconv, slru, n_valid, n_exp)
        meta = _route_meta(cnt[:, 0, :].astype(I32), max_tiles)
        yg = _experts(xg, meta, l, w1, b1_rows, w2, b2_rows, n_exp)
        final = l == depth - 1
        res = _combine(h2, yg, gw, pos, p_prompt, p_samp, l, w, g_fin, n_exp, final)
        if final:
            y_prompt, out_sample = res
        else:
            h = h_sample = res[0]
            h_sample_block = nb - 1
        for acc, val in zip((v_p, conv_p, lru_p, v_s, conv_s, lru_s), (vp, cp, lp, vs, cs, ls)):
            acc.append(val)

    y_sample = out_sample[:dec_batch].reshape(dec_batch, 1, d)
    v_prompt = jnp.stack(v_p).reshape(depth, CHUNK, batch, dh).transpose(0, 2, 1, 3)
    conv_prompt = jnp.stack(conv_p).reshape(depth, CONV_W - 1, batch, dh).transpose(0, 2, 1, 3)
    v_sample = jnp.stack(v_s)[:, :dec_batch].reshape(depth, dec_batch, 1, dh)
    conv_sample = jnp.stack(conv_s)[:, :, :dec_batch].transpose(0, 2, 1, 3)
    return (y_prompt, y_sample, v_prompt, conv_prompt, jnp.stack(lru_p),
            v_sample, conv_sample, jnp.stack(lru_s)[:, :dec_batch])
```
